```python
import math
import jax, jax.numpy as jnp
from jax import lax
import numpy as np

D_MODEL = 2048
BATCH = 4
SEQ = 4096
DEPTH = 4

MEM_LEN = 256
ROPE_THETA = 500000.0
ROPE_FRAC = 4
NORM_EPS = 1e-6

GLA_HEADS = 4
GLA_DK = 64
GLA_DV = 128
GLA_GATE_RANK = 16
GLA_TAU = 16.0
GLA_CHUNK = 64

S5_GROUPS = 32
S5_CH = 16
S5_STATE = 64
S5_WIDTH = S5_GROUPS * S5_CH

DSA_HEADS = 4
DSA_HD = 128
DSA_IDX_HEADS = 16
DSA_IDX_DIM = 64
DSA_TOPK = 256
DSA_QBLOCK = 128

MEM_HEADS = 4
MEM_HD = 128

GLA_WIDTH = GLA_HEADS * GLA_DV
DSA_WIDTH = DSA_HEADS * DSA_HD
MEM_WIDTH = MEM_HEADS * MEM_HD
MIX_WIDTH = GLA_WIDTH + S5_WIDTH + DSA_WIDTH + MEM_WIDTH
OUT_HEAD_DIM = 128

IN_SPLITS = (GLA_HEADS * GLA_DK, GLA_HEADS * GLA_DK, GLA_WIDTH, GLA_GATE_RANK, GLA_WIDTH,
             S5_WIDTH, S5_WIDTH,
             DSA_WIDTH, DSA_HD, DSA_HD, DSA_IDX_HEADS * DSA_IDX_DIM, DSA_IDX_DIM, DSA_IDX_HEADS, DSA_WIDTH,
             MEM_WIDTH, MEM_WIDTH)
IN_WIDTH = sum(IN_SPLITS)

kernel_name = "hybrid_gla_s5_dsa_memxattn_trunk"

F32 = jnp.float32


def rms_norm(x, g):
    xf = x.astype(F32)
    y = xf * lax.rsqrt(jnp.mean(xf * xf, axis=-1, keepdims=True) + NORM_EPS)
    return y.astype(x.dtype) * g


def split_cols(p):
    idx, acc = [], 0
    for w in IN_SPLITS[:-1]:
        acc += w
        idx.append(acc)
    return jnp.split(p, idx, axis=-1)


def partial_rope(x, pos):
    d = x.shape[-1]
    rd = d // ROPE_FRAC
    half = rd // 2
    inv_freq = ROPE_THETA ** (-jnp.arange(half, dtype=F32) * 2.0 / rd)
    ang = pos.astype(F32)[:, :, None] * inv_freq
    cos = jnp.cos(ang)[:, :, None, :]
    sin = jnp.sin(ang)[:, :, None, :]
    xr = x[..., :rd].astype(F32)
    x1, x2 = xr[..., :half], xr[..., half:]
    rot = jnp.concatenate([x1 * cos - x2 * sin, x2 * cos + x1 * sin], axis=-1).astype(x.dtype)
    return jnp.concatenate([rot, x[..., rd:]], axis=-1)


def gla_chunked(q, k, v, log_a):
    B, S, H, dk = q.shape
    dv = v.shape[-1]
    C = GLA_CHUNK
    N = S // C
    q, k, v, log_a = (t.astype(F32).reshape(B, N, C, H, -1) for t in (q, k, v, log_a))
    b = jnp.cumsum(log_a, axis=2)
    b_last = b[:, :, -1]
    m = b[:, :, C // 2:C // 2 + 1]
    attn = jnp.einsum('bnthd,bnshd->bnhts', q * jnp.exp(b - m), k * jnp.exp(m - b))
    tril = jnp.tril(jnp.ones((C, C), dtype=bool))
    attn = jnp.where(tril, attn, 0.0)
    o_intra = jnp.einsum('bnhts,bnshe->bnthe', attn, v)
    d_state = jnp.einsum('bnshd,bnshe->bnhde', k * jnp.exp(b_last[:, :, None] - b), v)
    chunk_decay = jnp.exp(b_last)

    def step(state, inp):
        ds, dec = inp
        return dec[..., None] * state + ds, state

    _, s_prev = lax.scan(step, jnp.zeros((B, H, dk, dv), F32),
                         (jnp.moveaxis(d_state, 1, 0), jnp.moveaxis(chunk_decay, 1, 0)))
    s_prev = jnp.moveaxis(s_prev, 0, 1)
    o_inter = jnp.einsum('bnthd,bnhde->bnthe', q * jnp.exp(b), s_prev)
    return (o_intra + o_inter).reshape(B, S, H, dv)


def gla_branch(gq, gk, gv, glr, w_up, b_up):
    B, S, _ = gq.shape
    q = gq.reshape(B, S, GLA_HEADS, GLA_DK) * (GLA_DK ** -0.5)
    k = gk.reshape(B, S, GLA_HEADS, GLA_DK)
    v = gv.reshape(B, S, GLA_HEADS, GLA_DV)
    log_a = jax.nn.log_sigmoid((glr @ w_up + b_up).astype(F32)) / GLA_TAU
    log_a = log_a.reshape(B, S, GLA_HEADS, GLA_DK)
    o = gla_chunked(q, k, v, log_a)
    return o.reshape(B, S, GLA_WIDTH).astype(gq.dtype)


def s5_branch(u, a_re, a_im, log_step, b_re, b_im, c_re, c_im, d_skip, w_glu):
    B, S, _ = u.shape
    uf = u.astype(F32).reshape(B, S, S5_GROUPS, S5_CH)
    lam = lax.complex(a_re.astype(F32), a_im.astype(F32))
    step = jnp.exp(log_step.astype(F32))[:, None]
    lam_bar = jnp.exp(lam * step)
    b_bar = ((lam_bar - 1.0) / lam)[..., None] * lax.complex(b_re.astype(F32), b_im.astype(F32))
    bu = jnp.einsum('bsgc,gpc->bsgp', uf.astype(jnp.complex64), b_bar)
    a = jnp.broadcast_to(lam_bar, bu.shape)

    def combine(e1, e2):
        a1, b1 = e1
        a2, b2 = e2
        return a1 * a2, a2 * b1 + b2

    _, xs = lax.associative_scan(combine, (a, bu), axis=1)
    c = lax.complex(c_re.astype(F32), c_im.astype(F32))
    y = jnp.real(jnp.einsum('gcp,bsgp->bsgc', c, xs)) + d_skip.astype(F32).reshape(S5_GROUPS, S5_CH) * uf
    y = jax.nn.gelu(y.reshape(B, S, S5_WIDTH))
    y = y * jax.nn.sigmoid(y @ w_glu.astype(F32))
    return y.astype(u.dtype)


def dsa_branch(dq, dk, dv, diq, dik, diw, q_gain, k_gain, pos):
    B, S, _ = dq.shape
    k_sel = min(DSA_TOPK, S // 4)
    q = partial_rope(rms_norm(dq.reshape(B, S, DSA_HEADS, DSA_HD), q_gain), pos)
    k = partial_rope(rms_norm(dk.reshape(B, S, 1, DSA_HD), k_gain), pos)[:, :, 0]
    kv = jnp.concatenate([k, dv], axis=-1)
    iq = partial_rope(diq.reshape(B, S, DSA_IDX_HEADS, DSA_IDX_DIM), pos)
    ik = partial_rope(dik.reshape(B, S, 1, DSA_IDX_DIM), pos)[:, :, 0]
    iw = diw * (DSA_IDX_HEADS ** -0.5 * DSA_IDX_DIM ** -0.5)
    s_pos = jnp.arange(S)
    scale = DSA_HD ** -0.5

    def block(i):
        t0 = i * DSA_QBLOCK
        qb = lax.dynamic_slice_in_dim(q, t0, DSA_QBLOCK, axis=1)
        iqb = lax.dynamic_slice_in_dim(iq, t0, DSA_QBLOCK, axis=1)
        iwb = lax.dynamic_slice_in_dim(iw, t0, DSA_QBLOCK, axis=1)
        t_pos = t0 + jnp.arange(DSA_QBLOCK)
        rel = jax.nn.relu(jnp.einsum('bthd,bsd->bths', iqb, ik))
        score = jnp.einsum('bths,bth->bts', rel, iwb).astype(F32)
        causal = s_pos[None, :] <= t_pos[:, None]
        score = jnp.where(causal[None], score, -jnp.inf)
        _, idx = lax.top_k(score, k_sel)
        kv_sel = jax.vmap(lambda kk, ii: kk[ii])(kv, idx)
        ks, vs = kv_sel[..., :DSA_HD], kv_sel[..., DSA_HD:]
        valid = idx <= t_pos[None, :, None]
        logits = jnp.einsum('bthd,btkd->bhtk', qb, ks).astype(F32) * scale
        logits = jnp.where(valid[:, None], logits, -jnp.inf)
        p = jax.nn.softmax(logits, axis=-1).astype(vs.dtype)
        return jnp.einsum('bhtk,btkd->bthd', p, vs)

    out = lax.map(block, jnp.arange(S // DSA_QBLOCK))
    return jnp.moveaxis(out, 0, 1).reshape(B, S, DSA_WIDTH)


def mem_branch(mq, mem, mem_gain, w_kv, q_gain, k_gain):
    B, S, _ = mq.shape
    q = rms_norm(mq.reshape(B, S, MEM_HEADS, MEM_HD), q_gain)
    kv = rms_norm(mem, mem_gain) @ w_kv
    mk, mv = jnp.split(kv, 2, axis=-1)
    mk = rms_norm(mk.reshape(B, -1, MEM_HEADS, MEM_HD), k_gain)
    mv = mv.reshape(B, -1, MEM_HEADS, MEM_HD)
    logits = jnp.einsum('bthd,bmhd->bhtm', q, mk).astype(F32) * (MEM_HD ** -0.5)
    p = jax.nn.softmax(logits, axis=-1).astype(mv.dtype)
    return jnp.einsum('bhtm,bmhd->bthd', p, mv).reshape(B, S, MEM_WIDTH)


def setup_inputs(seed: int = 0) -> dict:
    key = jax.random.key(seed)
    ks = iter(jax.random.split(key, 40))
    L = DEPTH
    G, P, CH = S5_GROUPS, S5_STATE, S5_CH

    def nrm(shape, scale):
        return jax.random.normal(next(ks), shape, F32) * scale

    def gain(shape):
        return 1.0 + nrm(shape, 0.02)

    x = nrm((BATCH, SEQ, D_MODEL), 1.0)
    mem = nrm((BATCH, MEM_LEN, D_MODEL), 1.0)
    offset = jax.random.randint(next(ks), (BATCH, 1), 0, 4096, dtype=jnp.int32)
    positions = offset + jnp.arange(SEQ, dtype=jnp.int32)[None, :]
    ln_gain = gain((L, D_MODEL))
    w_in = nrm((L, D_MODEL, IN_WIDTH), D_MODEL ** -0.5)
    gla_w_up = nrm((L, GLA_GATE_RANK, GLA_HEADS * GLA_DK), GLA_GATE_RANK ** -0.5)
    gla_b_up = nrm((L, GLA_HEADS * GLA_DK), 0.1)
    s5_a_re = -0.5 + nrm((L, G, P), 0.01)
    s5_a_im = jnp.pi * jnp.arange(P, dtype=F32) + nrm((L, G, P), 0.01)
    s5_log_step = jax.random.uniform(next(ks), (L, G), F32, math.log(1e-3), math.log(1e-1))
    s5_b_re = nrm((L, G, P, CH), (2.0 * CH) ** -0.5)
    s5_b_im = nrm((L, G, P, CH), (2.0 * CH) ** -0.5)
    s5_c_re = nrm((L, G, CH, P), (2.0 * P) ** -0.5)
    s5_c_im = nrm((L, G, CH, P), (2.0 * P) ** -0.5)
    s5_d = nrm((L, S5_WIDTH), 1.0)
    s5_w_glu = nrm((L, S5_WIDTH, S5_WIDTH), S5_WIDTH ** -0.5)
    dsa_q_norm = gain((L, DSA_HD))
    dsa_k_norm = gain((L, DSA_HD))
    mem_norm = gain((L, D_MODEL))
    w_mem_kv = nrm((L, D_MODEL, 2 * MEM_WIDTH), D_MODEL ** -0.5)
    mem_q_norm = gain((L, MEM_HD))
    mem_k_norm = gain((L, MEM_HD))
    out_norm = gain((L, MIX_WIDTH))
    w_out = nrm((L, MIX_WIDTH, D_MODEL), (2.0 * DEPTH * MIX_WIDTH) ** -0.5)
    return {"x": x, "mem": mem, "positions": positions, "ln_gain": ln_gain, "w_in": w_in,
            "gla_w_up": gla_w_up, "gla_b_up": gla_b_up,
            "s5_a_re": s5_a_re, "s5_a_im": s5_a_im, "s5_log_step": s5_log_step,
            "s5_b_re": s5_b_re, "s5_b_im": s5_b_im, "s5_c_re": s5_c_re, "s5_c_im": s5_c_im,
            "s5_d": s5_d, "s5_w_glu": s5_w_glu,
            "dsa_q_norm": dsa_q_norm, "dsa_k_norm": dsa_k_norm,
            "mem_norm": mem_norm, "w_mem_kv": w_mem_kv, "mem_q_norm": mem_q_norm, "mem_k_norm": mem_k_norm,
            "out_norm": out_norm, "w_out": w_out}


def reference(x, mem, positions, ln_gain, w_in, gla_w_up, gla_b_up,
              s5_a_re, s5_a_im, s5_log_step, s5_b_re, s5_b_im, s5_c_re, s5_c_im, s5_d, s5_w_glu,
              dsa_q_norm, dsa_k_norm, mem_norm, w_mem_kv, mem_q_norm, mem_k_norm,
              out_norm, w_out):
    B, S, _ = x.shape
    n_out_heads = MIX_WIDTH // OUT_HEAD_DIM
    for l in range(DEPTH):
        h = rms_norm(x, ln_gain[l])
        p = h @ w_in[l]
        (gq, gk, gv, glr, gz, su, sz, dq, dk, dv, diq, dik, diw, dz, mq, mz) = split_cols(p)
        o_gla = gla_branch(gq, gk, gv, glr, gla_w_up[l], gla_b_up[l])
        o_s5 = s5_branch(su, s5_a_re[l], s5_a_im[l], s5_log_step[l], s5_b_re[l], s5_b_im[l],
                         s5_c_re[l], s5_c_im[l], s5_d[l], s5_w_glu[l])
        o_dsa = dsa_branch(dq, dk, dv, diq, dik, diw, dsa_q_norm[l], dsa_k_norm[l], positions)
        o_mem = mem_branch(mq, mem, mem_norm[l], w_mem_kv[l], mem_q_norm[l], mem_k_norm[l])
        o = jnp.concatenate([o_gla, o_s5, o_dsa, o_mem], axis=-1)
        o = rms_norm(o.reshape(B, S, n_out_heads, OUT_HEAD_DIM),
                     out_norm[l].reshape(n_out_heads, OUT_HEAD_DIM)).reshape(B, S, MIX_WIDTH)
        z = jnp.concatenate([gz, sz, dz, mz], axis=-1)
        x = x + (o * jax.nn.silu(z)) @ w_out[l]
    return x
```

```python
import functools
import math

import numpy as np
import jax
import jax.numpy as jnp
from jax import lax
from jax.experimental import pallas as pl
from jax.experimental.pallas import tpu as pltpu

F32 = jnp.float32
BF16 = jnp.bfloat16

D_MODEL = 2048
MEM_LEN = 256
ROPE_THETA = 500000.0
ROPE_FRAC = 4
NORM_EPS = 1e-6

GLA_HEADS = 4
GLA_DK = 64
GLA_DV = 128
GLA_GATE_RANK = 16
GLA_TAU = 16.0
GLA_CHUNK = 64

S5_GROUPS = 32
S5_CH = 16
S5_STATE = 64
S5_WIDTH = S5_GROUPS * S5_CH
S5_BLOCK = 16

DSA_HEADS = 4
DSA_HD = 128
DSA_IDX_HEADS = 16
DSA_IDX_DIM = 64
DSA_TOPK = 256
DSA_QBLOCK = 128
DSA_KCHUNK = 256

MEM_HEADS = 4
MEM_HD = 128
HEAD = 128
MIX_WIDTH = 2048

VMEM_LIMIT_BYTES = 52 * 1024 * 1024

_ORIG = dict(gq=(0, 256), gk=(256, 512), gv=(512, 1024), glr=(1024, 1040), gz=(1040, 1552),
             su=(1552, 2064), sz=(2064, 2576), dq=(2576, 3088), dk=(3088, 3216), dv=(3216, 3344),
             diq=(3344, 4368), dik=(4368, 4432), diw=(4432, 4448), dz=(4448, 4960),
             mq=(4960, 5472), mz=(5472, 5984))
_ORDER = ("gz", "sz", "dz", "mz", "diq", "gv", "su", "dq", "mq", "gq", "gk", "dk", "dv",
          "dik", "glr", "diw")
_COL = {}
_c = 0
for _n in _ORDER:
    _COL[_n] = _c
    _c += _ORIG[_n][1] - _ORIG[_n][0]
IN_PAD = 6144
MISC_COL = _COL["dik"]
GLR_LANE = 64
DIW_LANE = 80
INT_MIN = -(2 ** 31)


def _cparams(*sem):
    return pltpu.CompilerParams(dimension_semantics=sem, vmem_limit_bytes=VMEM_LIMIT_BYTES)


def _dot(a, b):
    return jnp.dot(a, b, preferred_element_type=F32)


def _dot_nt(a, b):
    return lax.dot_general(a, b, (((1,), (1,)), ((), ())), preferred_element_type=F32)


def _rms(x, gain):
    return x * lax.rsqrt(jnp.mean(x * x, axis=-1, keepdims=True) + NORM_EPS) * gain


def _in_proj_kernel(x_ref, g_ref, w_ref, o_ref, h_ref):
    @pl.when(pl.program_id(1) == 0)
    def _():
        rows = 128

        def body(r, c):
            sl = pl.ds(pl.multiple_of(r * rows, rows), rows)
            h_ref[sl, :] = _rms(x_ref[sl, :], g_ref[...]).astype(BF16)
            return c

        lax.fori_loop(0, x_ref.shape[0] // rows, body, 0)

    o_ref[...] = _dot(h_ref[...], w_ref[...]).astype(o_ref.dtype)


def _in_proj(x2d, gain, w):
    T = x2d.shape[0]
    tm = min(1024, T)
    tn = 512
    return pl.pallas_call(
        _in_proj_kernel,
        grid=(T // tm, IN_PAD // tn),
        in_specs=[pl.BlockSpec((tm, D_MODEL), lambda i, j: (i, 0)),
                  pl.BlockSpec((1, D_MODEL), lambda i, j: (0, 0)),
                  pl.BlockSpec((D_MODEL, tn), lambda i, j: (0, j))],
        out_specs=pl.BlockSpec((tm, tn), lambda i, j: (i, j)),
        out_shape=jax.ShapeDtypeStruct((T, IN_PAD), BF16),
        scratch_shapes=[pltpu.VMEM((tm, D_MODEL), BF16)],
        compiler_params=_cparams("arbitrary", "arbitrary"),
        name="in_proj",
    )(x2d, gain, w)


def _gla_kernel(q_ref, k_ref, v_ref, misc_ref, wup_ref, bup_ref, o_ref, st_ref):
    @pl.when(pl.program_id(1) == 0)
    def _():
        st_ref[...] = jnp.zeros_like(st_ref)

    C = GLA_CHUNK
    x = _dot(misc_ref[...], wup_ref[...]) + bup_ref[...]
    log_a = (jnp.minimum(x, 0.0) - jnp.log(1.0 + jnp.exp(-jnp.abs(x)))) * (1.0 / GLA_TAU)
    row = lax.broadcasted_iota(jnp.int32, (C, C), 0)
    col = lax.broadcasted_iota(jnp.int32, (C, C), 1)
    tril = row >= col
    tri = tril.astype(F32)
    lane = lax.broadcasted_iota(jnp.int32, (1, 2 * GLA_DK), 1)
    low = lane < GLA_DK
    for c in range(q_ref.shape[0] // C):
        rs = slice(c * C, (c + 1) * C)
        b = jnp.dot(tri, log_a[rs], preferred_element_type=F32, precision=lax.Precision.HIGHEST)
        b_last = b[C - 1:C]
        mid = b[C // 2:C // 2 + 1]
        qf = q_ref[rs, :].astype(F32) * (GLA_DK ** -0.5)
        kf = k_ref[rs, :].astype(F32)
        qe = qf * jnp.exp(b - mid)
        ke = kf * jnp.exp(mid - b)
        kd = kf * jnp.exp(b_last - b)
        qi = qf * jnp.exp(b)
        dec = jnp.exp(b_last)
        for j in range(GLA_HEADS // 2):
            ls = slice(j * 128, (j + 1) * 128)
            ke_p = ke[:, ls].astype(BF16)
            kd_p = kd[:, ls].astype(BF16)
            for hh in range(2):
                h = 2 * j + hh
                sel = low if hh == 0 else jnp.logical_not(low)
                qe_h = jnp.where(sel, qe[:, ls], 0.0).astype(BF16)
                qi_h = jnp.where(sel, qi[:, ls], 0.0).astype(BF16)
                attn = jnp.where(tril, _dot_nt(qe_h, ke_p), 0.0)
                v_h = v_ref[rs, h * GLA_DV:(h + 1) * GLA_DV]
                st = st_ref[h]
                o = _dot(attn.astype(BF16), v_h) + _dot_nt(qi_h, st.astype(BF16))
                v_t = v_h.astype(F32).T.astype(BF16)
                st_ref[h] = st * dec[:, ls] + _dot(v_t, kd_p)
                o_ref[rs, h * GLA_DV:(h + 1) * GLA_DV] = o.astype(o_ref.dtype)


def _gla(p, wup_pad, bup, B, S):
    tc = min(256, S)
    nc = S // tc
    row = lambda b, c: b * nc + c
    return pl.pallas_call(
        _gla_kernel,
        grid=(B, nc),
        in_specs=[pl.BlockSpec((tc, 256), lambda b, c: (row(b, c), _COL["gq"] // 256)),
                  pl.BlockSpec((tc, 256), lambda b, c: (row(b, c), _COL["gk"] // 256)),
                  pl.BlockSpec((tc, 512), lambda b, c: (row(b, c), _COL["gv"] // 512)),
                  pl.BlockSpec((tc, 128), lambda b, c: (row(b, c), MISC_COL // 128)),
                  pl.BlockSpec((128, 256), lambda b, c: (0, 0)),
                  pl.BlockSpec((1, 256), lambda b, c: (0, 0))],
        out_specs=pl.BlockSpec((tc, 512), lambda b, c: (row(b, c), 0)),
        out_shape=jax.ShapeDtypeStruct((B * S, GLA_HEADS * GLA_DV), BF16),
        scratch_shapes=[pltpu.VMEM((GLA_HEADS, GLA_DV, 2 * GLA_DK), F32)],
        compiler_params=_cparams("arbitrary", "arbitrary"),
        name="gla",
    )(p, p, p, p, wup_pad, bup)


def _s5_mats(a_re, a_im, log_step, b_re, b_im, c_re, c_im, d_skip):
    L, P, CH = S5_BLOCK, S5_STATE, S5_CH
    step = jnp.exp(log_step.astype(F32))[:, None]
    are, aim = a_re.astype(F32), a_im.astype(F32)
    j = jnp.arange(L + 1, dtype=F32)[None, :, None]
    mag = jnp.exp(are[:, None, :] * step[:, None, :] * j)
    ang = aim[:, None, :] * step[:, None, :] * j
    pw_re, pw_im = mag * jnp.cos(ang), mag * jnp.sin(ang)
    lb_re, lb_im = pw_re[:, 1], pw_im[:, 1]
    den = are * are + aim * aim
    f_re = ((lb_re - 1.0) * are + lb_im * aim) / den
    f_im = (lb_im * are - (lb_re - 1.0) * aim) / den
    bb_re = f_re[..., None] * b_re - f_im[..., None] * b_im
    bb_im = f_re[..., None] * b_im + f_im[..., None] * b_re
    lb_b_re = pw_re[..., None] * bb_re[:, None] - pw_im[..., None] * bb_im[:, None]
    lb_b_im = pw_re[..., None] * bb_im[:, None] + pw_im[..., None] * bb_re[:, None]
    kj = jnp.einsum("gcp,gjpd->gjcd", c_re, lb_b_re) - jnp.einsum("gcp,gjpd->gjcd", c_im, lb_b_im)
    t = np.arange(L)
    lag = t[None, :] - t[:, None]
    m = kj[:, np.clip(lag, 0, L)]
    m = jnp.where((lag >= 0)[None, :, :, None, None], m, 0.0)
    m = jnp.transpose(m, (0, 1, 4, 2, 3))
    eye = (lag == 0)[None, :, None, :, None] * jnp.eye(CH, dtype=F32)[None, None, :, None, :]
    m = m + eye * d_skip.reshape(S5_GROUPS, 1, 1, 1, CH)
    m = m.reshape(S5_GROUPS, L * CH, L * CH)
    p_re = jnp.transpose(lb_b_re[:, L - 1 - t], (0, 1, 3, 2)).reshape(S5_GROUPS, L * CH, P)
    p_im = jnp.transpose(lb_b_im[:, L - 1 - t], (0, 1, 3, 2)).reshape(S5_GROUPS, L * CH, P)
    pm = jnp.concatenate([p_re, p_im, p_im, p_re], axis=-1)
    cl_re = c_re[:, None] * pw_re[:, 1:, None, :] - c_im[:, None] * pw_im[:, 1:, None, :]
    cl_im = c_re[:, None] * pw_im[:, 1:, None, :] + c_im[:, None] * pw_re[:, 1:, None, :]
    q_re = jnp.transpose(cl_re, (0, 3, 1, 2)).reshape(S5_GROUPS, P, L * CH)
    q_im = -jnp.transpose(cl_im, (0, 3, 1, 2)).reshape(S5_GROUPS, P, L * CH)
    qm = jnp.concatenate([q_re, q_im], axis=1)
    al_re, al_im = pw_re[:, L], pw_im[:, L]
    ar = jnp.concatenate([al_re, al_re], axis=-1).reshape(1, S5_GROUPS * 2 * P)
    ai_a = jnp.concatenate([-al_im, al_im], axis=-1).reshape(1, S5_GROUPS * 2 * P)
    ai_b = jnp.concatenate([al_im, -al_im], axis=-1).reshape(1, S5_GROUPS * 2 * P)
    return m.astype(BF16), pm.astype(BF16), qm.astype(BF16), ar, ai_a, ai_b


def _s5_kernel(u_ref, m_ref, p_ref, q_ref, ar_ref, aia_ref, aib_ref, y_ref, va_ref, vb_ref, x_ref):
    GH = u_ref.shape[1]
    NB = u_ref.shape[2]
    W = 2 * S5_STATE
    for g in range(GH):
        v2 = _dot(u_ref[0, g], p_ref[g])
        va_ref[:, g * W:(g + 1) * W] = v2[:, :W]
        vb_ref[:, g * W:(g + 1) * W] = v2[:, W:]
    ar, aia, aib = ar_ref[...], aia_ref[...], aib_ref[...]

    def step(n, carry):
        xa, xb = carry
        x_ref[pl.ds(n, 1), :] = xa
        na = xa * ar + xb * aia + va_ref[pl.ds(n, 1), :]
        nb = xb * ar + xa * aib + vb_ref[pl.ds(n, 1), :]
        return na, nb

    z = jnp.zeros((1, GH * W), F32)
    lax.fori_loop(0, NB, step, (z, z), unroll=8)
    for g in range(GH):
        y = _dot(u_ref[0, g], m_ref[g]) + _dot(x_ref[:, g * W:(g + 1) * W].astype(BF16), q_ref[g])
        y_ref[0, g] = y.astype(y_ref.dtype)


def _s5(p, mats, B, S):
    m, pm, qm, ar, aia, aib = mats
    L, G, CH = S5_BLOCK, S5_GROUPS, S5_CH
    NB = S // L
    su = p[:, _COL["su"]:_COL["su"] + S5_WIDTH]
    u = su.reshape(B, NB, L, G, CH).transpose(0, 3, 1, 2, 4).reshape(B, G, NB, L * CH)
    GH = 16
    W = 2 * S5_STATE
    y = pl.pallas_call(
        _s5_kernel,
        grid=(B, G // GH),
        in_specs=[pl.BlockSpec((1, GH, NB, L * CH), lambda b, h: (b, h, 0, 0)),
                  pl.BlockSpec((GH, L * CH, L * CH), lambda b, h: (h, 0, 0)),
                  pl.BlockSpec((GH, L * CH, 2 * W), lambda b, h: (h, 0, 0)),
                  pl.BlockSpec((GH, W, L * CH), lambda b, h: (h, 0, 0)),
                  pl.BlockSpec((1, GH * W), lambda b, h: (0, h)),
                  pl.BlockSpec((1, GH * W), lambda b, h: (0, h)),
                  pl.BlockSpec((1, GH * W), lambda b, h: (0, h))],
        out_specs=pl.BlockSpec((1, GH, NB, L * CH), lambda b, h: (b, h, 0, 0)),
        out_shape=jax.ShapeDtypeStruct((B, G, NB, L * CH), BF16),
        scratch_shapes=[pltpu.VMEM((NB, GH * W), F32), pltpu.VMEM((NB, GH * W), F32),
                        pltpu.VMEM((NB, GH * W), F32)],
        compiler_params=_cparams("arbitrary", "arbitrary"),
        name="s5_scan",
    )(u, m, pm, qm, ar, aia, aib)
    return y.reshape(B, G, NB, L, CH).transpose(0, 2, 3, 1, 4).reshape(B * S, S5_WIDTH)


def _rope_kernel(pos_ref, f128_ref, f64_ref, c128_ref, s128_ref, c64_ref, s64_ref):
    pos = pos_ref[...]
    a = pos * f128_ref[...]
    c128_ref[...] = jnp.cos(a)
    s128_ref[...] = jnp.sin(a)
    a = pos * f64_ref[...]
    c64_ref[...] = jnp.cos(a)
    s64_ref[...] = jnp.sin(a)


def _rope_freq(d, width):
    rd = d // ROPE_FRAC
    half = rd // 2
    inv = ROPE_THETA ** (-jnp.arange(half, dtype=F32) * 2.0 / rd)
    per_head = jnp.concatenate([inv, inv, jnp.zeros((d - rd,), F32)])
    return jnp.tile(per_head, width // d).reshape(1, width)


def _rope_tables(positions):
    T = positions.size
    pos = positions.astype(F32).reshape(T, 1)
    tq = min(512, T)
    out = jax.ShapeDtypeStruct((T, 128), F32)
    return pl.pallas_call(
        _rope_kernel,
        grid=(T // tq,),
        in_specs=[pl.BlockSpec((tq, 1), lambda i: (i, 0)),
                  pl.BlockSpec((1, 128), lambda i: (0, 0)),
                  pl.BlockSpec((1, 128), lambda i: (0, 0))],
        out_specs=[pl.BlockSpec((tq, 128), lambda i: (i, 0))] * 4,
        out_shape=[out] * 4,
        compiler_params=_cparams("arbitrary"),
        name="rope_tables",
    )(pos, _rope_freq(DSA_HD, 128), _rope_freq(DSA_IDX_DIM, 128))


def _rope(x, cos, sin, head_dim):
    width = x.shape[-1]
    half = head_dim // ROPE_FRAC // 2
    lane = lax.broadcasted_iota(jnp.int32, (1, width), 1)
    per = lane % head_dim
    sgn_lo = jnp.where(per < half, -1.0, 0.0)
    sgn_hi = jnp.where((per >= half) & (per < 2 * half), 1.0, 0.0)
    up = pltpu.roll(x, width - half, axis=1)
    dn = pltpu.roll(x, half, axis=1)
    return x * cos + sin * (up * sgn_lo + dn * sgn_hi)


def _dsa_prep_kernel(dq_ref, dk_ref, diq_ref, misc_ref, c128_ref, s128_ref, c64_ref, s64_ref,
                     qg_ref, kg_ref, q_out, k_out, iq_out, ik_out):
    c128, s128 = c128_ref[...], s128_ref[...]
    c64, s64 = c64_ref[...], s64_ref[...]
    for h in range(DSA_HEADS):
        sl = slice(h * DSA_HD, (h + 1) * DSA_HD)
        q = _rope(_rms(dq_ref[:, sl].astype(F32), qg_ref[...]), c128, s128, DSA_HD)
        q_out[:, sl] = (q * (DSA_HD ** -0.5)).astype(q_out.dtype)
    k = _rope(_rms(dk_ref[...].astype(F32), kg_ref[...]), c128, s128, DSA_HD)
    k_out[...] = k.astype(k_out.dtype)
    lane = lax.broadcasted_iota(jnp.int32, (1, 128), 1)
    low = lane < DSA_IDX_DIM
    for j in range(DSA_IDX_HEADS // 2):
        x = _rope(diq_ref[:, j * 128:(j + 1) * 128].astype(F32), c64, s64, DSA_IDX_DIM)
        iq_out[:, (2 * j) * 128:(2 * j + 1) * 128] = jnp.where(low, x, 0.0).astype(iq_out.dtype)
        xs = pltpu.roll(x, DSA_IDX_DIM, axis=1)
        iq_out[:, (2 * j + 1) * 128:(2 * j + 2) * 128] = jnp.where(low, xs, 0.0).astype(iq_out.dtype)
    ik = _rope(misc_ref[...].astype(F32), c64, s64, DSA_IDX_DIM)
    ik_out[...] = jnp.where(low, ik, 0.0).astype(ik_out.dtype)


def _dsa_prep(p, tables, q_gain, k_gain):
    T = p.shape[0]
    tq = min(512, T)
    tab = pl.BlockSpec((tq, 128), lambda i: (i, 0))
    gain = pl.BlockSpec((1, 128), lambda i: (0, 0))
    return pl.pallas_call(
        _dsa_prep_kernel,
        grid=(T // tq,),
        in_specs=[pl.BlockSpec((tq, 512), lambda i: (i, _COL["dq"] // 512)),
                  pl.BlockSpec((tq, 128), lambda i: (i, _COL["dk"] // 128)),
                  pl.BlockSpec((tq, 1024), lambda i: (i, _COL["diq"] // 1024)),
                  pl.BlockSpec((tq, 128), lambda i: (i, MISC_COL // 128)),
                  tab, tab, tab, tab, gain, gain],
        out_specs=[pl.BlockSpec((tq, 512), lambda i: (i, 0)),
                   pl.BlockSpec((tq, 128), lambda i: (i, 0)),
                   pl.BlockSpec((tq, 2048), lambda i: (i, 0)),
                   pl.BlockSpec((tq, 128), lambda i: (i, 0))],
        out_shape=[jax.ShapeDtypeStruct((T, 512), BF16), jax.ShapeDtypeStruct((T, 128), BF16),
                   jax.ShapeDtypeStruct((T, 2048), BF16), jax.ShapeDtypeStruct((T, 128), BF16)],
        compiler_params=_cparams("arbitrary"),
        name="dsa_prep",
    )(p, p, p, p, *tables, q_gain, k_gain)


def _dsa_kernel(iq_ref, misc_ref, q_ref, ikt_ref, kt_ref, v_ref, o_ref,
                wb_ref, key_ref, lg_ref, acc_ref, *, ksel):
    TQ, CK = DSA_QBLOCK, DSA_KCHUNK
    i = pl.program_id(1)
    nk = (i * TQ + TQ + CK - 1) // CK
    t_idx = i * TQ + lax.broadcasted_iota(jnp.int32, (TQ, 1), 0)

    mf = misc_ref[...].astype(F32) * (DSA_IDX_HEADS ** -0.5 * DSA_IDX_DIM ** -0.5)
    for h in range(DSA_IDX_HEADS):
        wb_ref[h] = jnp.broadcast_to(mf[:, DIW_LANE + h:DIW_LANE + h + 1], (TQ, 128))

    def score_chunk(kc, carry):
        rhs = ikt_ref[0, kc]
        for half in range(CK // 128):
            rh = rhs[:, half * 128:(half + 1) * 128]
            acc = jnp.zeros((TQ, 128), F32)
            for h in range(DSA_IDX_HEADS):
                a = _dot(iq_ref[:, h * 128:(h + 1) * 128], rh)
                acc = acc + wb_ref[h] * jnp.maximum(a, 0.0)
            s_idx = kc * CK + half * 128 + lax.broadcasted_iota(jnp.int32, (1, 128), 1)
            bits = lax.bitcast_convert_type(acc, jnp.int32)
            key = bits ^ ((bits >> 31) & 0x7FFFFFFF)
            key_ref[kc, :, half * 128:(half + 1) * 128] = jnp.where(s_idx <= t_idx, key, INT_MIN)
        return carry

    lax.fori_loop(0, nk, score_chunk, 0)

    def bit_step(bi, thr):
        cand = thr + jnp.left_shift(jnp.int32(1), 31 - bi)

        cb = jnp.broadcast_to(cand, (TQ, 128))

        def count(kc, acc):
            keys = key_ref[kc]
            for half in range(CK // 128):
                acc = acc + jnp.where(keys[:, half * 128:(half + 1) * 128] >= cb, 1.0, 0.0)
            return acc

        acc = lax.fori_loop(0, nk, count, jnp.zeros((TQ, 128), F32))
        cnt = jnp.sum(acc, axis=1, keepdims=True)
        return jnp.where(cnt >= ksel, cand, thr)

    thr = lax.fori_loop(0, 32, bit_step, jnp.full((TQ, 1), INT_MIN, jnp.int32))
    thr = jnp.maximum(thr, INT_MIN + 1)

    def logit_chunk(kc, mx):
        sel = key_ref[kc] >= thr
        kt = kt_ref[0, kc]
        new = []
        for h in range(DSA_HEADS):
            s = jnp.where(sel, _dot(q_ref[:, h * DSA_HD:(h + 1) * DSA_HD], kt), -1e30)
            lg_ref[h, kc] = s
            new.append(jnp.maximum(mx[h], jnp.max(s, axis=1, keepdims=True)))
        return tuple(new)

    mx = lax.fori_loop(0, nk, logit_chunk, tuple(jnp.full((TQ, 1), -1e30, F32) for _ in range(DSA_HEADS)))
    acc_ref[...] = jnp.zeros_like(acc_ref)

    def pv_chunk(kc, ls):
        vv = v_ref[0, kc]
        new = []
        for h in range(DSA_HEADS):
            pr = jnp.exp(lg_ref[h, kc] - mx[h])
            new.append(ls[h] + jnp.sum(pr, axis=1, keepdims=True))
            acc_ref[:, h * DSA_HD:(h + 1) * DSA_HD] += _dot(pr.astype(BF16), vv)
        return tuple(new)

    ls = lax.fori_loop(0, nk, pv_chunk, tuple(jnp.zeros((TQ, 1), F32) for _ in range(DSA_HEADS)))
    for h in range(DSA_HEADS):
        sl = slice(h * DSA_HD, (h + 1) * DSA_HD)
        o_ref[:, sl] = (acc_ref[:, sl] / ls[h]).astype(o_ref.dtype)


def _dsa(p, q_rot, k_rot, iq_m, ik_pad, B, S):
    TQ, CK = DSA_QBLOCK, DSA_KCHUNK
    NC = S // CK
    nq = S // TQ
    ksel = min(DSA_TOPK, S // 4)
    ikt = ik_pad.reshape(B, NC, CK, 128).transpose(0, 1, 3, 2)
    kt = k_rot.reshape(B, NC, CK, 128).transpose(0, 1, 3, 2)
    v = p[:, _COL["dv"]:_COL["dv"] + 128].reshape(B, NC, CK, 128)
    row = lambda b, i: b * nq + i
    return pl.pallas_call(
        functools.partial(_dsa_kernel, ksel=ksel),
        grid=(B, nq),
        in_specs=[pl.BlockSpec((TQ, 2048), lambda b, i: (row(b, i), 0)),
                  pl.BlockSpec((TQ, 128), lambda b, i: (row(b, i), MISC_COL // 128)),
                  pl.BlockSpec((TQ, 512), lambda b, i: (row(b, i), 0)),
                  pl.BlockSpec((1, NC, 128, CK), lambda b, i: (b, 0, 0, 0)),
                  pl.BlockSpec((1, NC, 128, CK), lambda b, i: (b, 0, 0, 0)),
                  pl.BlockSpec((1, NC, CK, 128), lambda b, i: (b, 0, 0, 0))],
        out_specs=pl.BlockSpec((TQ, 512), lambda b, i: (row(b, i), 0)),
        out_shape=jax.ShapeDtypeStruct((B * S, DSA_HEADS * DSA_HD), BF16),
        scratch_shapes=[pltpu.VMEM((DSA_IDX_HEADS, TQ, 128), F32),
                        pltpu.VMEM((NC, TQ, CK), jnp.int32),
                        pltpu.VMEM((DSA_HEADS, NC, TQ, CK), F32),
                        pltpu.VMEM((TQ, DSA_HEADS * DSA_HD), F32)],
        compiler_params=_cparams("arbitrary", "arbitrary"),
        name="dsa_attn",
    )(iq_m, p, q_rot, ikt, kt, v)


def _mem_kv_kernel(mem_ref, g_ref, w_ref, kg_ref, k_out, v_out):
    h = _rms(mem_ref[...], g_ref[...]).astype(BF16)
    kv = _dot(h, w_ref[...])
    width = MEM_HEADS * MEM_HD
    for hd in range(MEM_HEADS):
        sl = slice(hd * MEM_HD, (hd + 1) * MEM_HD)
        k_out[:, sl] = _rms(kv[:, sl], kg_ref[...]).astype(k_out.dtype)
    v_out[...] = kv[:, width:].astype(v_out.dtype)


def _mem_kv(mem2d, gain, w_kv, k_gain, B):
    width = MEM_HEADS * MEM_HD
    out = jax.ShapeDtypeStruct((B * MEM_LEN, width), BF16)
    return pl.pallas_call(
        _mem_kv_kernel,
        grid=(B,),
        in_specs=[pl.BlockSpec((MEM_LEN, D_MODEL), lambda b: (b, 0)),
                  pl.BlockSpec((1, D_MODEL), lambda b: (0, 0)),
                  pl.BlockSpec((D_MODEL, 2 * width), lambda b: (0, 0)),
                  pl.BlockSpec((1, MEM_HD), lambda b: (0, 0))],
        out_specs=[pl.BlockSpec((MEM_LEN, width), lambda b: (b, 0))] * 2,
        out_shape=[out, out],
        compiler_params=_cparams("arbitrary"),
        name="mem_kv",
    )(mem2d, gain, w_kv, k_gain)


def _mem_attn_kernel(q_ref, k_ref, v_ref, qg_ref, o_ref):
    for h in range(MEM_HEADS):
        sl = slice(h * MEM_HD, (h + 1) * MEM_HD)
        q = _rms(q_ref[:, sl].astype(F32), qg_ref[...]) * (MEM_HD ** -0.5)
        s = _dot_nt(q.astype(BF16), k_ref[:, sl])
        pr = jnp.exp(s - jnp.max(s, axis=1, keepdims=True))
        o = _dot(pr.astype(BF16), v_ref[:, sl]) / jnp.sum(pr, axis=1, keepdims=True)
        o_ref[:, sl] = o.astype(o_ref.dtype)


def _mem_attn(p, mk, mv, q_gain, B, S):
    tq = min(512, S)
    nq = S // tq
    width = MEM_HEADS * MEM_HD
    return pl.pallas_call(
        _mem_attn_kernel,
        grid=(B, nq),
        in_specs=[pl.BlockSpec((tq, width), lambda b, i: (b * nq + i, _COL["mq"] // width)),
                  pl.BlockSpec((MEM_LEN, width), lambda b, i: (b, 0)),
                  pl.BlockSpec((MEM_LEN, width), lambda b, i: (b, 0)),
                  pl.BlockSpec((1, MEM_HD), lambda b, i: (0, 0))],
        out_specs=pl.BlockSpec((tq, width), lambda b, i: (b * nq + i, 0)),
        out_shape=jax.ShapeDtypeStruct((B * S, width), BF16),
        compiler_params=_cparams("arbitrary", "arbitrary"),
        name="mem_attn",
    )(p, mk, mv, q_gain)


def _out_proj_kernel(og_ref, ys_ref, od_ref, om_ref, z_ref, x_ref, gn_ref, wglu_ref, w_ref, o_ref, lhs_ref):
    y = jax.nn.gelu(ys_ref[...].astype(F32))
    o_s5 = y * jax.nn.sigmoid(_dot(y.astype(BF16), wglu_ref[...]))
    branches = (og_ref[...].astype(F32), o_s5, od_ref[...].astype(F32), om_ref[...].astype(F32))
    for bi, o in enumerate(branches):
        for h in range(o.shape[1] // HEAD):
            c0 = bi * 512 + h * HEAD
            z = z_ref[:, c0:c0 + HEAD].astype(F32)
            n = _rms(o[:, h * HEAD:(h + 1) * HEAD], gn_ref[:, c0:c0 + HEAD])
            lhs_ref[:, c0:c0 + HEAD] = (n * (z * jax.nn.sigmoid(z))).astype(BF16)
    o_ref[...] = x_ref[...] + _dot(lhs_ref[...], w_ref[...])


def _out_proj(o_gla, y_s5, o_dsa, o_mem, p, x2d, out_gain, w_glu, w_out):
    T = x2d.shape[0]
    tm = min(256, T)
    branch = pl.BlockSpec((tm, 512), lambda i: (i, 0))
    return pl.pallas_call(
        _out_proj_kernel,
        grid=(T // tm,),
        in_specs=[branch, branch, branch, branch,
                  pl.BlockSpec((tm, MIX_WIDTH), lambda i: (i, 0)),
                  pl.BlockSpec((tm, D_MODEL), lambda i: (i, 0)),
                  pl.BlockSpec((1, MIX_WIDTH), lambda i: (0, 0)),
                  pl.BlockSpec((S5_WIDTH, S5_WIDTH), lambda i: (0, 0)),
                  pl.BlockSpec((MIX_WIDTH, D_MODEL), lambda i: (0, 0))],
        out_specs=pl.BlockSpec((tm, D_MODEL), lambda i: (i, 0)),
        out_shape=jax.ShapeDtypeStruct((T, D_MODEL), F32),
        scratch_shapes=[pltpu.VMEM((tm, MIX_WIDTH), BF16)],
        compiler_params=_cparams("arbitrary"),
        name="out_proj",
    )(o_gla, y_s5, o_dsa, o_mem, p, x2d, out_gain, w_glu, w_out)


def _permute_w_in(w_in):
    pieces = [w_in[..., _ORIG[n][0]:_ORIG[n][1]] for n in _ORDER]
    used = sum(_ORIG[n][1] - _ORIG[n][0] for n in _ORDER)
    pad = jnp.zeros(w_in.shape[:-1] + (IN_PAD - used,), w_in.dtype)
    return jnp.concatenate(pieces + [pad], axis=-1).astype(BF16)


def kernel(x, mem, positions, ln_gain, w_in, gla_w_up, gla_b_up, s5_a_re, s5_a_im, s5_log_step, s5_b_re, s5_b_im, s5_c_re, s5_c_im, s5_d, s5_w_glu, dsa_q_norm, dsa_k_norm, mem_norm, w_mem_kv, mem_q_norm, mem_k_norm, out_norm, w_out):
    B, S, _ = x.shape
    depth = w_in.shape[0]
    T = B * S
    x2d = x.reshape(T, D_MODEL)
    mem2d = mem.reshape(B * mem.shape[1], D_MODEL)
    w_in_p = _permute_w_in(w_in)
    wup_pad = jnp.zeros((depth, 128, GLA_HEADS * GLA_DK), F32)
    wup_pad = wup_pad.at[:, GLR_LANE:GLR_LANE + GLA_GATE_RANK].set(gla_w_up).astype(BF16)
    w_kv = w_mem_kv.astype(BF16)
    w_glu = s5_w_glu.astype(BF16)
    w_o = w_out.astype(BF16)
    tables = _rope_tables(positions)
    for l in range(depth):
        p = _in_proj(x2d, ln_gain[l][None], w_in_p[l])
        o_gla = _gla(p, wup_pad[l], gla_b_up[l][None], B, S)
        mats = _s5_mats(s5_a_re[l], s5_a_im[l], s5_log_step[l], s5_b_re[l], s5_b_im[l],
                        s5_c_re[l], s5_c_im[l], s5_d[l])
        y_s5 = _s5(p, mats, B, S)
        q_rot, k_rot, iq_m, ik_pad = _dsa_prep(p, tables, dsa_q_norm[l][None], dsa_k_norm[l][None])
        o_dsa = _dsa(p, q_rot, k_rot, iq_m, ik_pad, B, S)
        mk, mv = _mem_kv(mem2d, mem_norm[l][None], w_kv[l], mem_k_norm[l][None], B)
        o_mem = _mem_attn(p, mk, mv, mem_q_norm[l][None], B, S)
        x2d = _out_proj(o_gla, y_s5, o_dsa, o_mem, p, x2d, out_norm[l][None], w_glu[l], w_o[l])
    return x2d.reshape(B, S, D_MODEL)
```

```python
import functools
import math

import numpy as np
import jax
import jax.numpy as jnp
from jax import lax
from jax.experimental import pallas as pl
from jax.experimental.pallas import tpu as pltpu

F32 = jnp.float32
BF16 = jnp.bfloat16

D_MODEL = 2048
MEM_LEN = 256
ROPE_THETA = 500000.0
ROPE_FRAC = 4
NORM_EPS = 1e-6

GLA_HEADS = 4
GLA_DK = 64
GLA_DV = 128
GLA_GATE_RANK = 16
GLA_TAU = 16.0
GLA_CHUNK = 64

S5_GROUPS = 32
S5_CH = 16
S5_STATE = 64
S5_WIDTH = S5_GROUPS * S5_CH
S5_BLOCK = 16

DSA_HEADS = 4
DSA_HD = 128
DSA_IDX_HEADS = 16
DSA_IDX_DIM = 64
DSA_TOPK = 256
DSA_QBLOCK = 256
DSA_KCHUNK = 256

MEM_HEADS = 4
MEM_HD = 128
HEAD = 128
MIX_WIDTH = 2048

VMEM_LIMIT_BYTES = 52 * 1024 * 1024

_ORIG = dict(gq=(0, 256), gk=(256, 512), gv=(512, 1024), glr=(1024, 1040), gz=(1040, 1552),
             su=(1552, 2064), sz=(2064, 2576), dq=(2576, 3088), dk=(3088, 3216), dv=(3216, 3344),
             diq=(3344, 4368), dik=(4368, 4432), diw=(4432, 4448), dz=(4448, 4960),
             mq=(4960, 5472), mz=(5472, 5984))
_ORDER = ("gz", "sz", "dz", "mz", "diq", "gv", "su", "dq", "mq", "gq", "gk", "dk", "dv",
          "dik", "glr", "diw")
_COL = {}
_c = 0
for _n in _ORDER:
    _COL[_n] = _c
    _c += _ORIG[_n][1] - _ORIG[_n][0]
IN_PAD = 6144
MISC_COL = _COL["dik"]
GLR_LANE = 64
DIW_LANE = 80
INT_MIN = -(2 ** 31)
I16_MIN = -(2 ** 15)


def _cparams(*sem):
    return pltpu.CompilerParams(dimension_semantics=sem, vmem_limit_bytes=VMEM_LIMIT_BYTES)


def _dot(a, b):
    return jnp.dot(a, b, preferred_element_type=F32)


def _dot_nt(a, b):
    return lax.dot_general(a, b, (((1,), (1,)), ((), ())), preferred_element_type=F32)


def _rms(x, gain):
    return x * lax.rsqrt(jnp.mean(x * x, axis=-1, keepdims=True) + NORM_EPS) * gain


def _in_proj_kernel(x_ref, g_ref, w_ref, o_ref, h_ref):
    @pl.when(pl.program_id(1) == 0)
    def _():
        rows = 128

        def body(r, c):
            sl = pl.ds(pl.multiple_of(r * rows, rows), rows)
            h_ref[sl, :] = _rms(x_ref[sl, :], g_ref[...]).astype(BF16)
            return c

        lax.fori_loop(0, x_ref.shape[0] // rows, body, 0)

    o_ref[...] = _dot(h_ref[...], w_ref[...]).astype(o_ref.dtype)


def _in_proj(x2d, gain, w):
    T = x2d.shape[0]
    tm = min(1024, T)
    tn = 512
    return pl.pallas_call(
        _in_proj_kernel,
        grid=(T // tm, IN_PAD // tn),
        in_specs=[pl.BlockSpec((tm, D_MODEL), lambda i, j: (i, 0)),
                  pl.BlockSpec((1, D_MODEL), lambda i, j: (0, 0)),
                  pl.BlockSpec((D_MODEL, tn), lambda i, j: (0, j))],
        out_specs=pl.BlockSpec((tm, tn), lambda i, j: (i, j)),
        out_shape=jax.ShapeDtypeStruct((T, IN_PAD), BF16),
        scratch_shapes=[pltpu.VMEM((tm, D_MODEL), BF16)],
        compiler_params=_cparams("arbitrary", "arbitrary"),
        name="in_proj",
    )(x2d, gain, w)


def _gla_kernel(q_ref, k_ref, v_ref, misc_ref, wup_ref, bup_ref, o_ref, st_ref):
    @pl.when(pl.program_id(1) == 0)
    def _():
        st_ref[...] = jnp.zeros_like(st_ref)

    C = GLA_CHUNK
    x = _dot(misc_ref[...], wup_ref[...]) + bup_ref[...]
    log_a = (jnp.minimum(x, 0.0) - jnp.log(1.0 + jnp.exp(-jnp.abs(x)))) * (1.0 / GLA_TAU)
    row = lax.broadcasted_iota(jnp.int32, (C, C), 0)
    col = lax.broadcasted_iota(jnp.int32, (C, C), 1)
    tril = row >= col
    tri = tril.astype(F32)
    lane = lax.broadcasted_iota(jnp.int32, (1, 2 * GLA_DK), 1)
    low = lane < GLA_DK
    for c in range(q_ref.shape[0] // C):
        rs = slice(c * C, (c + 1) * C)
        b = jnp.dot(tri, log_a[rs], preferred_element_type=F32, precision=lax.Precision.HIGHEST)
        b_last = b[C - 1:C]
        mid = b[C // 2:C // 2 + 1]
        qf = q_ref[rs, :].astype(F32) * (GLA_DK ** -0.5)
        kf = k_ref[rs, :].astype(F32)
        qe = qf * jnp.exp(b - mid)
        ke = kf * jnp.exp(mid - b)
        kd = kf * jnp.exp(b_last - b)
        qi = qf * jnp.exp(b)
        dec = jnp.exp(b_last)
        for j in range(GLA_HEADS // 2):
            ls = slice(j * 128, (j + 1) * 128)
            ke_p = ke[:, ls].astype(BF16)
            kd_p = kd[:, ls].astype(BF16)
            for hh in range(2):
                h = 2 * j + hh
                sel = low if hh == 0 else jnp.logical_not(low)
                qe_h = jnp.where(sel, qe[:, ls], 0.0).astype(BF16)
                qi_h = jnp.where(sel, qi[:, ls], 0.0).astype(BF16)
                attn = jnp.where(tril, _dot_nt(qe_h, ke_p), 0.0)
                v_h = v_ref[rs, h * GLA_DV:(h + 1) * GLA_DV]
                st = st_ref[h]
                o = _dot(attn.astype(BF16), v_h) + _dot_nt(qi_h, st.astype(BF16))
                v_t = v_h.astype(F32).T.astype(BF16)
                st_ref[h] = st * dec[:, ls] + _dot(v_t, kd_p)
                o_ref[rs, h * GLA_DV:(h + 1) * GLA_DV] = o.astype(o_ref.dtype)


def _gla(p, wup_pad, bup, B, S):
    tc = min(256, S)
    nc = S // tc
    row = lambda b, c: b * nc + c
    return pl.pallas_call(
        _gla_kernel,
        grid=(B, nc),
        in_specs=[pl.BlockSpec((tc, 256), lambda b, c: (row(b, c), _COL["gq"] // 256)),
                  pl.BlockSpec((tc, 256), lambda b, c: (row(b, c), _COL["gk"] // 256)),
                  pl.BlockSpec((tc, 512), lambda b, c: (row(b, c), _COL["gv"] // 512)),
                  pl.BlockSpec((tc, 128), lambda b, c: (row(b, c), MISC_COL // 128)),
                  pl.BlockSpec((128, 256), lambda b, c: (0, 0)),
                  pl.BlockSpec((1, 256), lambda b, c: (0, 0))],
        out_specs=pl.BlockSpec((tc, 512), lambda b, c: (row(b, c), 0)),
        out_shape=jax.ShapeDtypeStruct((B * S, GLA_HEADS * GLA_DV), BF16),
        scratch_shapes=[pltpu.VMEM((GLA_HEADS, GLA_DV, 2 * GLA_DK), F32)],
        compiler_params=_cparams("arbitrary", "arbitrary"),
        name="gla",
    )(p, p, p, p, wup_pad, bup)


def _s5_mats(a_re, a_im, log_step, b_re, b_im, c_re, c_im, d_skip):
    L, P, CH = S5_BLOCK, S5_STATE, S5_CH
    step = jnp.exp(log_step.astype(F32))[:, None]
    are, aim = a_re.astype(F32), a_im.astype(F32)
    j = jnp.arange(L + 1, dtype=F32)[None, :, None]
    mag = jnp.exp(are[:, None, :] * step[:, None, :] * j)
    ang = aim[:, None, :] * step[:, None, :] * j
    pw_re, pw_im = mag * jnp.cos(ang), mag * jnp.sin(ang)
    lb_re, lb_im = pw_re[:, 1], pw_im[:, 1]
    den = are * are + aim * aim
    f_re = ((lb_re - 1.0) * are + lb_im * aim) / den
    f_im = (lb_im * are - (lb_re - 1.0) * aim) / den
    bb_re = f_re[..., None] * b_re - f_im[..., None] * b_im
    bb_im = f_re[..., None] * b_im + f_im[..., None] * b_re
    lb_b_re = pw_re[..., None] * bb_re[:, None] - pw_im[..., None] * bb_im[:, None]
    lb_b_im = pw_re[..., None] * bb_im[:, None] + pw_im[..., None] * bb_re[:, None]
    kj = jnp.einsum("gcp,gjpd->gjcd", c_re, lb_b_re) - jnp.einsum("gcp,gjpd->gjcd", c_im, lb_b_im)
    t = np.arange(L)
    lag = t[None, :] - t[:, None]
    m = kj[:, np.clip(lag, 0, L)]
    m = jnp.where((lag >= 0)[None, :, :, None, None], m, 0.0)
    m = jnp.transpose(m, (0, 1, 4, 2, 3))
    eye = (lag == 0)[None, :, None, :, None] * jnp.eye(CH, dtype=F32)[None, None, :, None, :]
    m = m + eye * d_skip.reshape(S5_GROUPS, 1, 1, 1, CH)
    m = m.reshape(S5_GROUPS, L * CH, L * CH)
    p_re = jnp.transpose(lb_b_re[:, L - 1 - t], (0, 1, 3, 2)).reshape(S5_GROUPS, L * CH, P)
    p_im = jnp.transpose(lb_b_im[:, L - 1 - t], (0, 1, 3, 2)).reshape(S5_GROUPS, L * CH, P)
    pm = jnp.concatenate([p_re, p_im, p_im, p_re], axis=-1)
    cl_re = c_re[:, None] * pw_re[:, 1:, None, :] - c_im[:, None] * pw_im[:, 1:, None, :]
    cl_im = c_re[:, None] * pw_im[:, 1:, None, :] + c_im[:, None] * pw_re[:, 1:, None, :]
    q_re = jnp.transpose(cl_re, (0, 3, 1, 2)).reshape(S5_GROUPS, P, L * CH)
    q_im = -jnp.transpose(cl_im, (0, 3, 1, 2)).reshape(S5_GROUPS, P, L * CH)
    qm = jnp.concatenate([q_re, q_im], axis=1)
    al_re, al_im = pw_re[:, L], pw_im[:, L]
    ar = jnp.concatenate([al_re, al_re], axis=-1).reshape(1, S5_GROUPS * 2 * P)
    ai_a = jnp.concatenate([-al_im, al_im], axis=-1).reshape(1, S5_GROUPS * 2 * P)
    ai_b = jnp.concatenate([al_im, -al_im], axis=-1).reshape(1, S5_GROUPS * 2 * P)
    return m.astype(BF16), pm.astype(BF16), qm.astype(BF16), ar, ai_a, ai_b


def _s5_kernel(u_ref, m_ref, p_ref, q_ref, ar_ref, aia_ref, aib_ref, y_ref, va_ref, vb_ref, x_ref):
    GH = u_ref.shape[1]
    NB = u_ref.shape[2]
    W = 2 * S5_STATE
    for g in range(GH):
        v2 = _dot(u_ref[0, g], p_ref[g])
        va_ref[:, g * W:(g + 1) * W] = v2[:, :W]
        vb_ref[:, g * W:(g + 1) * W] = v2[:, W:]
    ar, aia, aib = ar_ref[...], aia_ref[...], aib_ref[...]

    def step(n, carry):
        xa, xb = carry
        x_ref[pl.ds(n, 1), :] = xa
        na = xa * ar + xb * aia + va_ref[pl.ds(n, 1), :]
        nb = xb * ar + xa * aib + vb_ref[pl.ds(n, 1), :]
        return na, nb

    z = jnp.zeros((1, GH * W), F32)
    lax.fori_loop(0, NB, step, (z, z), unroll=8)
    for g in range(GH):
        y = _dot(u_ref[0, g], m_ref[g]) + _dot(x_ref[:, g * W:(g + 1) * W].astype(BF16), q_ref[g])
        y_ref[0, g] = y.astype(y_ref.dtype)


def _s5(p, mats, B, S):
    m, pm, qm, ar, aia, aib = mats
    L, G, CH = S5_BLOCK, S5_GROUPS, S5_CH
    NB = S // L
    su = p[:, _COL["su"]:_COL["su"] + S5_WIDTH]
    u = su.reshape(B, NB, L, G, CH).transpose(0, 3, 1, 2, 4).reshape(B, G, NB, L * CH)
    GH = 16
    W = 2 * S5_STATE
    y = pl.pallas_call(
        _s5_kernel,
        grid=(B, G // GH),
        in_specs=[pl.BlockSpec((1, GH, NB, L * CH), lambda b, h: (b, h, 0, 0)),
                  pl.BlockSpec((GH, L * CH, L * CH), lambda b, h: (h, 0, 0)),
                  pl.BlockSpec((GH, L * CH, 2 * W), lambda b, h: (h, 0, 0)),
                  pl.BlockSpec((GH, W, L * CH), lambda b, h: (h, 0, 0)),
                  pl.BlockSpec((1, GH * W), lambda b, h: (0, h)),
                  pl.BlockSpec((1, GH * W), lambda b, h: (0, h)),
                  pl.BlockSpec((1, GH * W), lambda b, h: (0, h))],
        out_specs=pl.BlockSpec((1, GH, NB, L * CH), lambda b, h: (b, h, 0, 0)),
        out_shape=jax.ShapeDtypeStruct((B, G, NB, L * CH), BF16),
        scratch_shapes=[pltpu.VMEM((NB, GH * W), F32), pltpu.VMEM((NB, GH * W), F32),
                        pltpu.VMEM((NB, GH * W), F32)],
        compiler_params=_cparams("arbitrary", "arbitrary"),
        name="s5_scan",
    )(u, m, pm, qm, ar, aia, aib)
    return y.reshape(B, G, NB, L, CH).transpose(0, 2, 3, 1, 4).reshape(B * S, S5_WIDTH)


def _rope_kernel(pos_ref, f128_ref, f64_ref, c128_ref, s128_ref, c64_ref, s64_ref):
    pos = pos_ref[...]
    a = pos * f128_ref[...]
    c128_ref[...] = jnp.cos(a)
    s128_ref[...] = jnp.sin(a)
    a = pos * f64_ref[...]
    c64_ref[...] = jnp.cos(a)
    s64_ref[...] = jnp.sin(a)


def _rope_freq(d, width):
    rd = d // ROPE_FRAC
    half = rd // 2
    inv = ROPE_THETA ** (-jnp.arange(half, dtype=F32) * 2.0 / rd)
    per_head = jnp.concatenate([inv, inv, jnp.zeros((d - rd,), F32)])
    return jnp.tile(per_head, width // d).reshape(1, width)


def _rope_tables(positions):
    T = positions.size
    pos = positions.astype(F32).reshape(T, 1)
    tq = min(512, T)
    out = jax.ShapeDtypeStruct((T, 128), F32)
    return pl.pallas_call(
        _rope_kernel,
        grid=(T // tq,),
        in_specs=[pl.BlockSpec((tq, 1), lambda i: (i, 0)),
                  pl.BlockSpec((1, 128), lambda i: (0, 0)),
                  pl.BlockSpec((1, 128), lambda i: (0, 0))],
        out_specs=[pl.BlockSpec((tq, 128), lambda i: (i, 0))] * 4,
        out_shape=[out] * 4,
        compiler_params=_cparams("arbitrary"),
        name="rope_tables",
    )(pos, _rope_freq(DSA_HD, 128), _rope_freq(DSA_IDX_DIM, 128))


def _rope(x, cos, sin, head_dim):
    width = x.shape[-1]
    half = head_dim // ROPE_FRAC // 2
    lane = lax.broadcasted_iota(jnp.int32, (1, width), 1)
    per = lane % head_dim
    sgn_lo = jnp.where(per < half, -1.0, 0.0)
    sgn_hi = jnp.where((per >= half) & (per < 2 * half), 1.0, 0.0)
    up = pltpu.roll(x, width - half, axis=1)
    dn = pltpu.roll(x, half, axis=1)
    return x * cos + sin * (up * sgn_lo + dn * sgn_hi)


def _dsa_prep_kernel(dq_ref, dk_ref, diq_ref, misc_ref, c128_ref, s128_ref, c64_ref, s64_ref,
                     qg_ref, kg_ref, q_out, k_out, iq_out, ik_out):
    c128, s128 = c128_ref[...], s128_ref[...]
    c64, s64 = c64_ref[...], s64_ref[...]
    for h in range(DSA_HEADS):
        sl = slice(h * DSA_HD, (h + 1) * DSA_HD)
        q = _rope(_rms(dq_ref[:, sl].astype(F32), qg_ref[...]), c128, s128, DSA_HD)
        q_out[:, sl] = (q * (DSA_HD ** -0.5)).astype(q_out.dtype)
    k = _rope(_rms(dk_ref[...].astype(F32), kg_ref[...]), c128, s128, DSA_HD)
    k_out[...] = k.astype(k_out.dtype)
    for j in range(DSA_IDX_HEADS // 2):
        sl = slice(j * 128, (j + 1) * 128)
        iq_out[:, sl] = _rope(diq_ref[:, sl].astype(F32), c64, s64, DSA_IDX_DIM).astype(iq_out.dtype)
    ik_out[...] = _rope(misc_ref[...].astype(F32), c64, s64, DSA_IDX_DIM).astype(ik_out.dtype)


def _dsa_prep(p, tables, q_gain, k_gain):
    T = p.shape[0]
    tq = min(512, T)
    tab = pl.BlockSpec((tq, 128), lambda i: (i, 0))
    gain = pl.BlockSpec((1, 128), lambda i: (0, 0))
    return pl.pallas_call(
        _dsa_prep_kernel,
        grid=(T // tq,),
        in_specs=[pl.BlockSpec((tq, 512), lambda i: (i, _COL["dq"] // 512)),
                  pl.BlockSpec((tq, 128), lambda i: (i, _COL["dk"] // 128)),
                  pl.BlockSpec((tq, 1024), lambda i: (i, _COL["diq"] // 1024)),
                  pl.BlockSpec((tq, 128), lambda i: (i, MISC_COL // 128)),
                  tab, tab, tab, tab, gain, gain],
        out_specs=[pl.BlockSpec((tq, 512), lambda i: (i, 0)),
                   pl.BlockSpec((tq, 128), lambda i: (i, 0)),
                   pl.BlockSpec((tq, 1024), lambda i: (i, 0)),
                   pl.BlockSpec((tq, 128), lambda i: (i, 0))],
        out_shape=[jax.ShapeDtypeStruct((T, 512), BF16), jax.ShapeDtypeStruct((T, 128), BF16),
                   jax.ShapeDtypeStruct((T, 1024), BF16), jax.ShapeDtypeStruct((T, 128), BF16)],
        compiler_params=_cparams("arbitrary"),
        name="dsa_prep",
    )(p, p, p, p, *tables, q_gain, k_gain)


def _dsa_kernel(iqt_ref, wt_ref, qt_ref, ik_ref, k_ref, vt_ref, o_ref,
                key_ref, half_ref, bias_ref, p_ref, acc_ref, *, ksel):
    TQ, CK = DSA_QBLOCK, DSA_KCHUNK
    HK = CK // 2
    i = pl.program_id(1)
    nk = (i * TQ + TQ + CK - 1) // CK
    t_idx = i * TQ + lax.broadcasted_iota(jnp.int32, (1, TQ), 1)
    w_all = wt_ref[0] * (DSA_IDX_HEADS ** -0.5 * DSA_IDX_DIM ** -0.5)

    def score_chunk(kc, carry):
        for half in range(2):
            lhs = ik_ref[0, kc, half * HK:(half + 1) * HK, :]
            acc = None
            for h in range(DSA_IDX_HEADS):
                term = w_all[h:h + 1] * jnp.maximum(_dot(lhs, iqt_ref[0, h]), 0.0)
                acc = term if acc is None else acc + term
            s_idx = kc * CK + half * HK + lax.broadcasted_iota(jnp.int32, (HK, 1), 0)
            bits = lax.bitcast_convert_type(acc, jnp.int32)
            key = bits ^ ((bits >> 31) & 0x7FFFFFFF)
            key = jnp.where(s_idx <= t_idx, key, INT_MIN)
            key_ref[kc, half * HK:(half + 1) * HK, :] = key
            half_ref[kc, half * HK:(half + 1) * HK, :] = (key >> 16).astype(jnp.int16)
        return carry

    lax.fori_loop(0, nk, score_chunk, 0)
    npair = (nk + 1) // 2

    @pl.when(nk % 2 == 1)
    def _():
        key_ref[nk] = jnp.full((CK, TQ), INT_MIN, jnp.int32)
        half_ref[nk] = jnp.full((CK, TQ), I16_MIN, jnp.int16)

    chains = 4
    groups = 2 * CK // (16 * chains)

    def count_ge(cand):
        cb = jnp.broadcast_to(cand.astype(jnp.int16), (16, TQ))[None, None]

        def count(j, acc):
            keys = half_ref[pl.ds(2 * j, 2)].reshape(groups, chains, 16, TQ)
            for g in range(groups):
                acc = acc + jnp.where(keys[g:g + 1] >= cb, jnp.int16(1), jnp.int16(0))[0]
            return acc

        acc = lax.fori_loop(0, npair, count, jnp.zeros((chains, 16, TQ), jnp.int16))
        return jnp.sum(jnp.sum(acc.astype(jnp.int32), axis=0), axis=0, keepdims=True)

    def select16(need):
        def bit_step(bi, thr):
            cand = thr + jnp.left_shift(jnp.int32(1), 15 - bi)
            return jnp.where(count_ge(cand) >= need, cand, thr)

        return lax.fori_loop(0, 16, bit_step, jnp.full((1, TQ), I16_MIN, jnp.int32))

    thr_hi = select16(ksel)
    above = jnp.where(thr_hi >= -I16_MIN - 1, 0, count_ge(jnp.minimum(thr_hi + 1, -I16_MIN - 1)))

    def low_half(j, carry):
        keys = key_ref[pl.ds(2 * j, 2)]
        low = (keys & 0xFFFF) + I16_MIN
        half_ref[pl.ds(2 * j, 2)] = jnp.where((keys >> 16) == thr_hi, low, I16_MIN).astype(jnp.int16)
        return carry

    lax.fori_loop(0, npair, low_half, 0)
    thr_lo = select16(ksel - above)
    thr = thr_hi * 65536 + (thr_lo - I16_MIN)
    thr = jnp.maximum(thr, INT_MIN + 1)

    kg = CK // 16

    def max_chunk(kc, mxs):
        bias_ref[...] = jnp.where(key_ref[kc] >= thr, 0.0, -1e30)
        kk = k_ref[0, kc]
        new = []
        for h in range(DSA_HEADS):
            s = _dot(kk, qt_ref[0, h]) + bias_ref[...]
            new.append(jnp.maximum(mxs[h], jnp.max(s.reshape(kg, 2, 8, TQ), axis=0)))
        return tuple(new)

    mxs = lax.fori_loop(0, nk, max_chunk, tuple(jnp.full((2, 8, TQ), -1e30, F32) for _ in range(DSA_HEADS)))
    mb = [jnp.broadcast_to(jnp.max(jnp.max(m, axis=0), axis=0, keepdims=True), (8, TQ)) for m in mxs]
    acc_ref[...] = jnp.zeros_like(acc_ref)

    unit = 64

    def pv_chunk(kc, ls):
        vt = vt_ref[0, kc]
        new = list(ls)
        for u in range(CK // unit):
            rows = slice(u * unit, (u + 1) * unit)
            bias = jnp.where(key_ref[kc, rows, :] >= thr, 0.0, -1e30)
            kk = k_ref[0, kc, rows, :]
            for h in range(DSA_HEADS):
                s = _dot(kk, qt_ref[0, h]) + bias
                pr = jnp.exp(s.reshape(unit // 8, 8, TQ) - mb[h][None])
                new[h] = new[h] + jnp.sum(pr.reshape(unit // 16, 2, 8, TQ), axis=0)
                p_ref[h, rows, :] = pr.reshape(unit, TQ).astype(BF16)
        for h in range(DSA_HEADS):
            acc_ref[h] += _dot(vt, p_ref[h])
        return tuple(new)

    ls = lax.fori_loop(0, nk, pv_chunk, tuple(jnp.zeros((2, 8, TQ), F32) for _ in range(DSA_HEADS)))
    for h in range(DSA_HEADS):
        denom = jnp.sum(jnp.sum(ls[h], axis=0), axis=0, keepdims=True)
        o_ref[:, h * DSA_HD:(h + 1) * DSA_HD] = (acc_ref[h] / denom).T.astype(o_ref.dtype)


def _dsa(p, q_rot, k_rot, iq_rot, ik_rot, B, S):
    TQ, CK = DSA_QBLOCK, DSA_KCHUNK
    NC = S // CK
    nq = S // TQ
    assert NC % 2 == 0 and TQ % CK == 0
    ksel = min(DSA_TOPK, S // 4)
    iqt = iq_rot.reshape(B, S, DSA_IDX_HEADS, DSA_IDX_DIM).transpose(0, 2, 3, 1)
    qt = q_rot.reshape(B, S, DSA_HEADS, DSA_HD).transpose(0, 2, 3, 1)
    wt = p[:, MISC_COL + DIW_LANE:MISC_COL + DIW_LANE + DSA_IDX_HEADS].astype(F32)
    wt = wt.reshape(B, S, DSA_IDX_HEADS).transpose(0, 2, 1)
    ik = ik_rot[:, :DSA_IDX_DIM].reshape(B, NC, CK, DSA_IDX_DIM)
    k = k_rot.reshape(B, NC, CK, DSA_HD)
    vt = p[:, _COL["dv"]:_COL["dv"] + DSA_HD].reshape(B, NC, CK, DSA_HD).transpose(0, 1, 3, 2)
    return pl.pallas_call(
        functools.partial(_dsa_kernel, ksel=ksel),
        grid=(B, nq),
        in_specs=[pl.BlockSpec((1, DSA_IDX_HEADS, DSA_IDX_DIM, TQ), lambda b, i: (b, 0, 0, i)),
                  pl.BlockSpec((1, DSA_IDX_HEADS, TQ), lambda b, i: (b, 0, i)),
                  pl.BlockSpec((1, DSA_HEADS, DSA_HD, TQ), lambda b, i: (b, 0, 0, i)),
                  pl.BlockSpec((1, NC, CK, DSA_IDX_DIM), lambda b, i: (b, 0, 0, 0)),
                  pl.BlockSpec((1, NC, CK, DSA_HD), lambda b, i: (b, 0, 0, 0)),
                  pl.BlockSpec((1, NC, DSA_HD, CK), lambda b, i: (b, 0, 0, 0))],
        out_specs=pl.BlockSpec((TQ, DSA_HEADS * DSA_HD), lambda b, i: (b * nq + i, 0)),
        out_shape=jax.ShapeDtypeStruct((B * S, DSA_HEADS * DSA_HD), BF16),
        scratch_shapes=[pltpu.VMEM((NC, CK, TQ), jnp.int32),
                        pltpu.VMEM((NC, CK, TQ), jnp.int16),
                        pltpu.VMEM((CK, TQ), F32),
                        pltpu.VMEM((DSA_HEADS, CK, TQ), BF16),
                        pltpu.VMEM((DSA_HEADS, DSA_HD, TQ), F32)],
        compiler_params=_cparams("arbitrary", "arbitrary"),
        name="dsa_attn",
    )(iqt, wt, qt, ik, k, vt)


def _mem_kv_kernel(mem_ref, g_ref, w_ref, kg_ref, k_out, v_out):
    h = _rms(mem_ref[...], g_ref[...]).astype(BF16)
    kv = _dot(h, w_ref[...])
    width = MEM_HEADS * MEM_HD
    for hd in range(MEM_HEADS):
        sl = slice(hd * MEM_HD, (hd + 1) * MEM_HD)
        k_out[:, sl] = _rms(kv[:, sl], kg_ref[...]).astype(k_out.dtype)
    v_out[...] = kv[:, width:].astype(v_out.dtype)


def _mem_kv(mem2d, gain, w_kv, k_gain, B):
    width = MEM_HEADS * MEM_HD
    out = jax.ShapeDtypeStruct((B * MEM_LEN, width), BF16)
    return pl.pallas_call(
        _mem_kv_kernel,
        grid=(B,),
        in_specs=[pl.BlockSpec((MEM_LEN, D_MODEL), lambda b: (b, 0)),
                  pl.BlockSpec((1, D_MODEL), lambda b: (0, 0)),
                  pl.BlockSpec((D_MODEL, 2 * width), lambda b: (0, 0)),
                  pl.BlockSpec((1, MEM_HD), lambda b: (0, 0))],
        out_specs=[pl.BlockSpec((MEM_LEN, width), lambda b: (b, 0))] * 2,
        out_shape=[out, out],
        compiler_params=_cparams("arbitrary"),
        name="mem_kv",
    )(mem2d, gain, w_kv, k_gain)


def _mem_attn_kernel(q_ref, k_ref, v_ref, qg_ref, o_ref):
    for h in range(MEM_HEADS):
        sl = slice(h * MEM_HD, (h + 1) * MEM_HD)
        q = _rms(q_ref[:, sl].astype(F32), qg_ref[...]) * (MEM_HD ** -0.5)
        s = _dot_nt(q.astype(BF16), k_ref[:, sl])
        pr = jnp.exp(s - jnp.max(s, axis=1, keepdims=True))
        o = _dot(pr.astype(BF16), v_ref[:, sl]) / jnp.sum(pr, axis=1, keepdims=True)
        o_ref[:, sl] = o.astype(o_ref.dtype)


def _mem_attn(p, mk, mv, q_gain, B, S):
    tq = min(512, S)
    nq = S // tq
    width = MEM_HEADS * MEM_HD
    return pl.pallas_call(
        _mem_attn_kernel,
        grid=(B, nq),
        in_specs=[pl.BlockSpec((tq, width), lambda b, i: (b * nq + i, _COL["mq"] // width)),
                  pl.BlockSpec((MEM_LEN, width), lambda b, i: (b, 0)),
                  pl.BlockSpec((MEM_LEN, width), lambda b, i: (b, 0)),
                  pl.BlockSpec((1, MEM_HD), lambda b, i: (0, 0))],
        out_specs=pl.BlockSpec((tq, width), lambda b, i: (b * nq + i, 0)),
        out_shape=jax.ShapeDtypeStruct((B * S, width), BF16),
        compiler_params=_cparams("arbitrary", "arbitrary"),
        name="mem_attn",
    )(p, mk, mv, q_gain)


def _out_proj_kernel(og_ref, ys_ref, od_ref, om_ref, z_ref, x_ref, gn_ref, wglu_ref, w_ref, o_ref, lhs_ref):
    y = jax.nn.gelu(ys_ref[...].astype(F32))
    o_s5 = y * jax.nn.sigmoid(_dot(y.astype(BF16), wglu_ref[...]))
    branches = (og_ref[...].astype(F32), o_s5, od_ref[...].astype(F32), om_ref[...].astype(F32))
    for bi, o in enumerate(branches):
        for h in range(o.shape[1] // HEAD):
            c0 = bi * 512 + h * HEAD
            z = z_ref[:, c0:c0 + HEAD].astype(F32)
            n = _rms(o[:, h * HEAD:(h + 1) * HEAD], gn_ref[:, c0:c0 + HEAD])
            lhs_ref[:, c0:c0 + HEAD] = (n * (z * jax.nn.sigmoid(z))).astype(BF16)
    o_ref[...] = x_ref[...] + _dot(lhs_ref[...], w_ref[...])


def _out_proj(o_gla, y_s5, o_dsa, o_mem, p, x2d, out_gain, w_glu, w_out):
    T = x2d.shape[0]
    tm = min(256, T)
    branch = pl.BlockSpec((tm, 512), lambda i: (i, 0))
    return pl.pallas_call(
        _out_proj_kernel,
        grid=(T // tm,),
        in_specs=[branch, branch, branch, branch,
                  pl.BlockSpec((tm, MIX_WIDTH), lambda i: (i, 0)),
                  pl.BlockSpec((tm, D_MODEL), lambda i: (i, 0)),
                  pl.BlockSpec((1, MIX_WIDTH), lambda i: (0, 0)),
                  pl.BlockSpec((S5_WIDTH, S5_WIDTH), lambda i: (0, 0)),
                  pl.BlockSpec((MIX_WIDTH, D_MODEL), lambda i: (0, 0))],
        out_specs=pl.BlockSpec((tm, D_MODEL), lambda i: (i, 0)),
        out_shape=jax.ShapeDtypeStruct((T, D_MODEL), F32),
        scratch_shapes=[pltpu.VMEM((tm, MIX_WIDTH), BF16)],
        compiler_params=_cparams("arbitrary"),
        name="out_proj",
    )(o_gla, y_s5, o_dsa, o_mem, p, x2d, out_gain, w_glu, w_out)


def _permute_w_in(w_in):
    pieces = [w_in[..., _ORIG[n][0]:_ORIG[n][1]] for n in _ORDER]
    used = sum(_ORIG[n][1] - _ORIG[n][0] for n in _ORDER)
    pad = jnp.zeros(w_in.shape[:-1] + (IN_PAD - used,), w_in.dtype)
    return jnp.concatenate(pieces + [pad], axis=-1).astype(BF16)


def kernel(x, mem, positions, ln_gain, w_in, gla_w_up, gla_b_up, s5_a_re, s5_a_im, s5_log_step, s5_b_re, s5_b_im, s5_c_re, s5_c_im, s5_d, s5_w_glu, dsa_q_norm, dsa_k_norm, mem_norm, w_mem_kv, mem_q_norm, mem_k_norm, out_norm, w_out):
    B, S, _ = x.shape
    depth = w_in.shape[0]
    T = B * S
    x2d = x.reshape(T, D_MODEL)
    mem2d = mem.reshape(B * mem.shape[1], D_MODEL)
    w_in_p = _permute_w_in(w_in)
    wup_pad = jnp.zeros((depth, 128, GLA_HEADS * GLA_DK), F32)
    wup_pad = wup_pad.at[:, GLR_LANE:GLR_LANE + GLA_GATE_RANK].set(gla_w_up).astype(BF16)
    w_kv = w_mem_kv.astype(BF16)
    w_glu = s5_w_glu.astype(BF16)
    w_o = w_out.astype(BF16)
    tables = _rope_tables(positions)
    for l in range(depth):
        p = _in_proj(x2d, ln_gain[l][None], w_in_p[l])
        o_gla = _gla(p, wup_pad[l], gla_b_up[l][None], B, S)
        mats = _s5_mats(s5_a_re[l], s5_a_im[l], s5_log_step[l], s5_b_re[l], s5_b_im[l],
                        s5_c_re[l], s5_c_im[l], s5_d[l])
        y_s5 = _s5(p, mats, B, S)
        q_rot, k_rot, iq_m, ik_pad = _dsa_prep(p, tables, dsa_q_norm[l][None], dsa_k_norm[l][None])
        o_dsa = _dsa(p, q_rot, k_rot, iq_m, ik_pad, B, S)
        mk, mv = _mem_kv(mem2d, mem_norm[l][None], w_kv[l], mem_k_norm[l][None], B)
        o_mem = _mem_attn(p, mk, mv, mem_q_norm[l][None], B, S)
        x2d = _out_proj(o_gla, y_s5, o_dsa, o_mem, p, x2d, out_norm[l][None], w_glu[l], w_o[l])
    return x2d.reshape(B, S, D_MODEL)
```

```python
import functools
import math

import numpy as np
import jax
import jax.numpy as jnp
from jax import lax
from jax.experimental import pallas as pl
from jax.experimental.pallas import tpu as pltpu

F32 = jnp.float32
BF16 = jnp.bfloat16

D_MODEL = 2048
MEM_LEN = 256
ROPE_THETA = 500000.0
ROPE_FRAC = 4
NORM_EPS = 1e-6

GLA_HEADS = 4
GLA_DK = 64
GLA_DV = 128
GLA_GATE_RANK = 16
GLA_TAU = 16.0
GLA_CHUNK = 64

S5_GROUPS = 32
S5_CH = 16
S5_STATE = 64
S5_WIDTH = S5_GROUPS * S5_CH
S5_BLOCK = 16

DSA_HEADS = 4
DSA_HD = 128
DSA_IDX_HEADS = 16
DSA_IDX_DIM = 64
DSA_TOPK = 256
DSA_QBLOCK = 256
DSA_KCHUNK = 256

MEM_HEADS = 4
MEM_HD = 128
HEAD = 128
MIX_WIDTH = 2048

VMEM_LIMIT_BYTES = 52 * 1024 * 1024

_ORIG = dict(gq=(0, 256), gk=(256, 512), gv=(512, 1024), glr=(1024, 1040), gz=(1040, 1552),
             su=(1552, 2064), sz=(2064, 2576), dq=(2576, 3088), dk=(3088, 3216), dv=(3216, 3344),
             diq=(3344, 4368), dik=(4368, 4432), diw=(4432, 4448), dz=(4448, 4960),
             mq=(4960, 5472), mz=(5472, 5984))
_ORDER = ("gz", "sz", "dz", "mz", "diq", "gv", "su", "dq", "mq", "gq", "gk", "dk", "dv",
          "dik", "glr", "diw")
_COL = {}
_c = 0
for _n in _ORDER:
    _COL[_n] = _c
    _c += _ORIG[_n][1] - _ORIG[_n][0]
IN_PAD = 6144
MISC_COL = _COL["dik"]
GLR_LANE = 64
DIW_LANE = 80
INT_MIN = -(2 ** 31)
I16_MIN = -(2 ** 15)


def _cparams(*sem):
    return pltpu.CompilerParams(dimension_semantics=sem, vmem_limit_bytes=VMEM_LIMIT_BYTES)


def _dot(a, b):
    return jnp.dot(a, b, preferred_element_type=F32)


def _dot_nt(a, b):
    return lax.dot_general(a, b, (((1,), (1,)), ((), ())), preferred_element_type=F32)


def _rms(x, gain):
    return x * lax.rsqrt(jnp.mean(x * x, axis=-1, keepdims=True) + NORM_EPS) * gain


def _in_proj_kernel(x_ref, g_ref, w_ref, o_ref, su_ref, h_ref):
    @pl.when(pl.program_id(1) == 0)
    def _():
        rows = 128

        def body(r, c):
            sl = pl.ds(pl.multiple_of(r * rows, rows), rows)
            h_ref[sl, :] = _rms(x_ref[sl, :], g_ref[...]).astype(BF16)
            return c

        lax.fori_loop(0, x_ref.shape[0] // rows, body, 0)

    acc = _dot(h_ref[...], w_ref[...])
    o_ref[...] = acc.astype(o_ref.dtype)

    @pl.when(pl.program_id(1) == _COL["su"] // S5_WIDTH)
    def _():
        for q in range(S5_WIDTH // 128):
            su_ref[q] = acc[:, q * 128:(q + 1) * 128]


def _in_proj(x2d, gain, w):
    T = x2d.shape[0]
    tm = min(1024, T)
    tn = S5_WIDTH
    return pl.pallas_call(
        _in_proj_kernel,
        grid=(T // tm, IN_PAD // tn),
        in_specs=[pl.BlockSpec((tm, D_MODEL), lambda i, j: (i, 0)),
                  pl.BlockSpec((1, D_MODEL), lambda i, j: (0, 0)),
                  pl.BlockSpec((D_MODEL, tn), lambda i, j: (0, j))],
        out_specs=[pl.BlockSpec((tm, tn), lambda i, j: (i, j)),
                   pl.BlockSpec((S5_WIDTH // 128, tm, 128), lambda i, j: (0, i, 0))],
        out_shape=[jax.ShapeDtypeStruct((T, IN_PAD), BF16),
                   jax.ShapeDtypeStruct((S5_WIDTH // 128, T, 128), F32)],
        scratch_shapes=[pltpu.VMEM((tm, D_MODEL), BF16)],
        compiler_params=_cparams("arbitrary", "arbitrary"),
        name="in_proj",
    )(x2d, gain, w)


def _gla_kernel(q_ref, k_ref, v_ref, misc_ref, wup_ref, bup_ref, o_ref, st_ref):
    @pl.when(pl.program_id(1) == 0)
    def _():
        st_ref[...] = jnp.zeros_like(st_ref)

    C = GLA_CHUNK
    x = _dot(misc_ref[...], wup_ref[...]) + bup_ref[...]
    log_a = (jnp.minimum(x, 0.0) - jnp.log(1.0 + jnp.exp(-jnp.abs(x)))) * (1.0 / GLA_TAU)
    row = lax.broadcasted_iota(jnp.int32, (C, C), 0)
    col = lax.broadcasted_iota(jnp.int32, (C, C), 1)
    tril = row >= col
    tri = tril.astype(F32)
    lane = lax.broadcasted_iota(jnp.int32, (1, 2 * GLA_DK), 1)
    low = lane < GLA_DK
    for c in range(q_ref.shape[0] // C):
        rs = slice(c * C, (c + 1) * C)
        b = jnp.dot(tri, log_a[rs], preferred_element_type=F32, precision=lax.Precision.HIGHEST)
        b_last = b[C - 1:C]
        mid = b[C // 2:C // 2 + 1]
        qf = q_ref[rs, :].astype(F32) * (GLA_DK ** -0.5)
        kf = k_ref[rs, :].astype(F32)
        qe = qf * jnp.exp(b - mid)
        ke = kf * jnp.exp(mid - b)
        kd = kf * jnp.exp(b_last - b)
        qi = qf * jnp.exp(b)
        dec = jnp.exp(b_last)
        for j in range(GLA_HEADS // 2):
            ls = slice(j * 128, (j + 1) * 128)
            ke_p = ke[:, ls].astype(BF16)
            kd_p = kd[:, ls].astype(BF16)
            for hh in range(2):
                h = 2 * j + hh
                sel = low if hh == 0 else jnp.logical_not(low)
                qe_h = jnp.where(sel, qe[:, ls], 0.0).astype(BF16)
                qi_h = jnp.where(sel, qi[:, ls], 0.0).astype(BF16)
                attn = jnp.where(tril, _dot_nt(qe_h, ke_p), 0.0)
                v_h = v_ref[rs, h * GLA_DV:(h + 1) * GLA_DV]
                st = st_ref[h]
                o = _dot(attn.astype(BF16), v_h) + _dot_nt(qi_h, st.astype(BF16))
                v_t = v_h.astype(F32).T.astype(BF16)
                st_ref[h] = st * dec[:, ls] + _dot(v_t, kd_p)
                o_ref[rs, h * GLA_DV:(h + 1) * GLA_DV] = o.astype(o_ref.dtype)


def _gla(p, wup_pad, bup, B, S):
    tc = min(256, S)
    nc = S // tc
    row = lambda b, c: b * nc + c
    return pl.pallas_call(
        _gla_kernel,
        grid=(B, nc),
        in_specs=[pl.BlockSpec((tc, 256), lambda b, c: (row(b, c), _COL["gq"] // 256)),
                  pl.BlockSpec((tc, 256), lambda b, c: (row(b, c), _COL["gk"] // 256)),
                  pl.BlockSpec((tc, 512), lambda b, c: (row(b, c), _COL["gv"] // 512)),
                  pl.BlockSpec((tc, 128), lambda b, c: (row(b, c), MISC_COL // 128)),
                  pl.BlockSpec((128, 256), lambda b, c: (0, 0)),
                  pl.BlockSpec((1, 256), lambda b, c: (0, 0))],
        out_specs=pl.BlockSpec((tc, 512), lambda b, c: (row(b, c), 0)),
        out_shape=jax.ShapeDtypeStruct((B * S, GLA_HEADS * GLA_DV), BF16),
        scratch_shapes=[pltpu.VMEM((GLA_HEADS, GLA_DV, 2 * GLA_DK), F32)],
        compiler_params=_cparams("arbitrary", "arbitrary"),
        name="gla",
    )(p, p, p, p, wup_pad, bup)


def _s5_mats(a_re, a_im, log_step, b_re, b_im, c_re, c_im, d_skip):
    L, P, CH = S5_BLOCK, S5_STATE, S5_CH
    step = jnp.exp(log_step.astype(F32))[:, None]
    are, aim = a_re.astype(F32), a_im.astype(F32)
    j = jnp.arange(L + 1, dtype=F32)[None, :, None]
    mag = jnp.exp(are[:, None, :] * step[:, None, :] * j)
    ang = aim[:, None, :] * step[:, None, :] * j
    pw_re, pw_im = mag * jnp.cos(ang), mag * jnp.sin(ang)
    lb_re, lb_im = pw_re[:, 1], pw_im[:, 1]
    den = are * are + aim * aim
    f_re = ((lb_re - 1.0) * are + lb_im * aim) / den
    f_im = (lb_im * are - (lb_re - 1.0) * aim) / den
    bb_re = f_re[..., None] * b_re - f_im[..., None] * b_im
    bb_im = f_re[..., None] * b_im + f_im[..., None] * b_re
    lb_b_re = pw_re[..., None] * bb_re[:, None] - pw_im[..., None] * bb_im[:, None]
    lb_b_im = pw_re[..., None] * bb_im[:, None] + pw_im[..., None] * bb_re[:, None]
    kj = jnp.einsum("gcp,gjpd->gjcd", c_re, lb_b_re) - jnp.einsum("gcp,gjpd->gjcd", c_im, lb_b_im)
    t = np.arange(L)
    lag = t[None, :] - t[:, None]
    m = kj[:, np.clip(lag, 0, L)]
    m = jnp.where((lag >= 0)[None, :, :, None, None], m, 0.0)
    m = jnp.transpose(m, (0, 1, 4, 2, 3))
    eye = (lag == 0)[None, :, None, :, None] * jnp.eye(CH, dtype=F32)[None, None, :, None, :]
    m = m + eye * d_skip.reshape(S5_GROUPS, 1, 1, 1, CH)
    m = m.reshape(S5_GROUPS, L * CH, L * CH)
    p_re = jnp.transpose(lb_b_re[:, L - 1 - t], (0, 1, 3, 2)).reshape(S5_GROUPS, L * CH, P)
    p_im = jnp.transpose(lb_b_im[:, L - 1 - t], (0, 1, 3, 2)).reshape(S5_GROUPS, L * CH, P)
    pm = jnp.concatenate([p_re, p_im, p_im, p_re], axis=-1)
    cl_re = c_re[:, None] * pw_re[:, 1:, None, :] - c_im[:, None] * pw_im[:, 1:, None, :]
    cl_im = c_re[:, None] * pw_im[:, 1:, None, :] + c_im[:, None] * pw_re[:, 1:, None, :]
    q_re = jnp.transpose(cl_re, (0, 3, 1, 2)).reshape(S5_GROUPS, P, L * CH)
    q_im = -jnp.transpose(cl_im, (0, 3, 1, 2)).reshape(S5_GROUPS, P, L * CH)
    qm = jnp.concatenate([q_re, q_im], axis=1)
    al_re, al_im = pw_re[:, L], pw_im[:, L]
    ar = jnp.concatenate([al_re, al_re], axis=-1).reshape(1, S5_GROUPS * 2 * P)
    ai_a = jnp.concatenate([-al_im, al_im], axis=-1).reshape(1, S5_GROUPS * 2 * P)
    ai_b = jnp.concatenate([al_im, -al_im], axis=-1).reshape(1, S5_GROUPS * 2 * P)
    return m.astype(BF16), pm.astype(BF16), qm.astype(BF16), ar, ai_a, ai_b


def _s5_perm():
    src = np.arange(1024).reshape(8, 8, S5_CH).transpose(1, 0, 2).reshape(-1)
    m = np.zeros((1024, 1024), np.float32)
    m[np.arange(1024), src] = 1.0
    return jnp.asarray(m, BF16)


def _s5_kernel(su_ref, perm_ref, m_ref, p_ref, q_ref, ar_ref, aia_ref, aib_ref, y_ref,
               xcat_ref, u_ref, va_ref, vb_ref, x_ref, ycat_ref):
    L = S5_BLOCK
    NB = su_ref.shape[1] // L
    GH = 8
    W = 2 * S5_STATE
    for s in range(L):
        xcat_ref[:, s * 128:(s + 1) * 128] = su_ref[0, pl.ds(s, NB, stride=L), :].astype(BF16)
    for half in range(2):
        u_ref[half] = _dot(xcat_ref[:, half * 1024:(half + 1) * 1024], perm_ref[...]).astype(BF16)

    def u_of(g):
        return jnp.concatenate([u_ref[0, :, g * 128:(g + 1) * 128], u_ref[1, :, g * 128:(g + 1) * 128]], axis=1)

    for g in range(GH):
        v2 = _dot(u_of(g), p_ref[g])
        va_ref[:, g * W:(g + 1) * W] = v2[:, :W]
        vb_ref[:, g * W:(g + 1) * W] = v2[:, W:]
    ar, aia, aib = ar_ref[...], aia_ref[...], aib_ref[...]

    def step(n, carry):
        xa, xb = carry
        x_ref[pl.ds(n, 1), :] = xa
        na = xa * ar + xb * aia + va_ref[pl.ds(n, 1), :]
        nb = xb * ar + xa * aib + vb_ref[pl.ds(n, 1), :]
        return na, nb

    z = jnp.zeros((1, GH * W), F32)
    lax.fori_loop(0, NB, step, (z, z), unroll=8)
    for g in range(GH):
        y = _dot(u_of(g), m_ref[g]) + _dot(x_ref[:, g * W:(g + 1) * W].astype(BF16), q_ref[g])
        for half in range(2):
            ycat_ref[half, :, g * 128:(g + 1) * 128] = y[:, half * 128:(half + 1) * 128].astype(BF16)
    for half in range(2):
        out = _dot(ycat_ref[half], perm_ref[...])
        for t in range(8):
            y_ref[0, pl.ds(half * 8 + t, NB, stride=L), :] = out[:, t * 128:(t + 1) * 128]


def _s5(su, mats, B, S):
    m, pm, qm, ar, aia, aib = mats
    L, G, CH = S5_BLOCK, S5_GROUPS, S5_CH
    NB = S // L
    GH = 8
    W = 2 * S5_STATE
    slab = pl.BlockSpec((1, S, 128), lambda b, h: (h, b, 0))
    coef = pl.BlockSpec((1, GH * W), lambda b, h: (0, h))
    return pl.pallas_call(
        _s5_kernel,
        grid=(B, G // GH),
        in_specs=[slab,
                  pl.BlockSpec((1024, 1024), lambda b, h: (0, 0)),
                  pl.BlockSpec((GH, L * CH, L * CH), lambda b, h: (h, 0, 0)),
                  pl.BlockSpec((GH, L * CH, 2 * W), lambda b, h: (h, 0, 0)),
                  pl.BlockSpec((GH, W, L * CH), lambda b, h: (h, 0, 0)),
                  coef, coef, coef],
        out_specs=slab,
        out_shape=jax.ShapeDtypeStruct((G // GH, B * S, 128), F32),
        scratch_shapes=[pltpu.VMEM((NB, L * 128), BF16), pltpu.VMEM((2, NB, 1024), BF16),
                        pltpu.VMEM((NB, GH * W), F32), pltpu.VMEM((NB, GH * W), F32),
                        pltpu.VMEM((NB, GH * W), F32), pltpu.VMEM((2, NB, 1024), BF16)],
        compiler_params=_cparams("arbitrary", "arbitrary"),
        name="s5_scan",
    )(su, _s5_perm(), m, pm, qm, ar, aia, aib)


def _rope_kernel(pos_ref, f128_ref, f64_ref, c128_ref, s128_ref, c64_ref, s64_ref):
    pos = pos_ref[...]
    a = pos * f128_ref[...]
    c128_ref[...] = jnp.cos(a)
    s128_ref[...] = jnp.sin(a)
    a = pos * f64_ref[...]
    c64_ref[...] = jnp.cos(a)
    s64_ref[...] = jnp.sin(a)


def _rope_freq(d, width):
    rd = d // ROPE_FRAC
    half = rd // 2
    inv = ROPE_THETA ** (-jnp.arange(half, dtype=F32) * 2.0 / rd)
    per_head = jnp.concatenate([inv, inv, jnp.zeros((d - rd,), F32)])
    return jnp.tile(per_head, width // d).reshape(1, width)


def _rope_tables(positions):
    T = positions.size
    pos = positions.astype(F32).reshape(T, 1)
    tq = min(512, T)
    out = jax.ShapeDtypeStruct((T, 128), F32)
    return pl.pallas_call(
        _rope_kernel,
        grid=(T // tq,),
        in_specs=[pl.BlockSpec((tq, 1), lambda i: (i, 0)),
                  pl.BlockSpec((1, 128), lambda i: (0, 0)),
                  pl.BlockSpec((1, 128), lambda i: (0, 0))],
        out_specs=[pl.BlockSpec((tq, 128), lambda i: (i, 0))] * 4,
        out_shape=[out] * 4,
        compiler_params=_cparams("arbitrary"),
        name="rope_tables",
    )(pos, _rope_freq(DSA_HD, 128), _rope_freq(DSA_IDX_DIM, 128))


def _rope(x, cos, sin, head_dim):
    width = x.shape[-1]
    half = head_dim // ROPE_FRAC // 2
    lane = lax.broadcasted_iota(jnp.int32, (1, width), 1)
    per = lane % head_dim
    sgn_lo = jnp.where(per < half, -1.0, 0.0)
    sgn_hi = jnp.where((per >= half) & (per < 2 * half), 1.0, 0.0)
    up = pltpu.roll(x, width - half, axis=1)
    dn = pltpu.roll(x, half, axis=1)
    return x * cos + sin * (up * sgn_lo + dn * sgn_hi)


def _dsa_prep_kernel(dq_ref, dk_ref, diq_ref, misc_ref, c128_ref, s128_ref, c64_ref, s64_ref,
                     qg_ref, kg_ref, qt_out, k_out, iqt_out, ik_out, wt_out):
    c128, s128 = c128_ref[...], s128_ref[...]
    c64, s64 = c64_ref[...], s64_ref[...]
    for h in range(DSA_HEADS):
        sl = slice(h * DSA_HD, (h + 1) * DSA_HD)
        q = _rope(_rms(dq_ref[:, sl].astype(F32), qg_ref[...]), c128, s128, DSA_HD)
        qt_out[h] = (q * (DSA_HD ** -0.5)).T.astype(qt_out.dtype)
    k = _rope(_rms(dk_ref[...].astype(F32), kg_ref[...]), c128, s128, DSA_HD)
    k_out[...] = k.astype(k_out.dtype)
    for j in range(DSA_IDX_HEADS // 2):
        sl = slice(j * 128, (j + 1) * 128)
        xt = _rope(diq_ref[:, sl].astype(F32), c64, s64, DSA_IDX_DIM).T
        iqt_out[2 * j] = xt[:DSA_IDX_DIM].astype(iqt_out.dtype)
        iqt_out[2 * j + 1] = xt[DSA_IDX_DIM:].astype(iqt_out.dtype)
    misc = misc_ref[...].astype(F32)
    ik_out[...] = _rope(misc, c64, s64, DSA_IDX_DIM)[:, :DSA_IDX_DIM].astype(ik_out.dtype)
    wt_out[...] = misc.T[DIW_LANE:DIW_LANE + DSA_IDX_HEADS]


def _dsa_prep(p, tables, q_gain, k_gain):
    T = p.shape[0]
    tq = min(512, T)
    tab = pl.BlockSpec((tq, 128), lambda i: (i, 0))
    gain = pl.BlockSpec((1, 128), lambda i: (0, 0))
    return pl.pallas_call(
        _dsa_prep_kernel,
        grid=(T // tq,),
        in_specs=[pl.BlockSpec((tq, 512), lambda i: (i, _COL["dq"] // 512)),
                  pl.BlockSpec((tq, 128), lambda i: (i, _COL["dk"] // 128)),
                  pl.BlockSpec((tq, 1024), lambda i: (i, _COL["diq"] // 1024)),
                  pl.BlockSpec((tq, 128), lambda i: (i, MISC_COL // 128)),
                  tab, tab, tab, tab, gain, gain],
        out_specs=[pl.BlockSpec((DSA_HEADS, DSA_HD, tq), lambda i: (0, 0, i)),
                   pl.BlockSpec((tq, DSA_HD), lambda i: (i, 0)),
                   pl.BlockSpec((DSA_IDX_HEADS, DSA_IDX_DIM, tq), lambda i: (0, 0, i)),
                   pl.BlockSpec((tq, DSA_IDX_DIM), lambda i: (i, 0)),
                   pl.BlockSpec((DSA_IDX_HEADS, tq), lambda i: (0, i))],
        out_shape=[jax.ShapeDtypeStruct((DSA_HEADS, DSA_HD, T), BF16),
                   jax.ShapeDtypeStruct((T, DSA_HD), BF16),
                   jax.ShapeDtypeStruct((DSA_IDX_HEADS, DSA_IDX_DIM, T), BF16),
                   jax.ShapeDtypeStruct((T, DSA_IDX_DIM), BF16),
                   jax.ShapeDtypeStruct((DSA_IDX_HEADS, T), F32)],
        compiler_params=_cparams("arbitrary"),
        name="dsa_prep",
    )(p, p, p, p, *tables, q_gain, k_gain)


def _dsa_kernel(iqt_ref, wt_ref, qt_ref, ik_ref, k_ref, vt_ref, o_ref,
                key_ref, half_ref, bias_ref, p_ref, acc_ref, *, ksel):
    TQ, CK = DSA_QBLOCK, DSA_KCHUNK
    HK = CK // 2
    i = pl.program_id(1)
    nk = (i * TQ + TQ + CK - 1) // CK
    t_idx = i * TQ + lax.broadcasted_iota(jnp.int32, (1, TQ), 1)
    w_all = wt_ref[...] * (DSA_IDX_HEADS ** -0.5 * DSA_IDX_DIM ** -0.5)

    def score_chunk(kc, carry):
        for half in range(2):
            lhs = ik_ref[0, kc, half * HK:(half + 1) * HK, :]
            acc = None
            for h in range(DSA_IDX_HEADS):
                term = w_all[h:h + 1] * jnp.maximum(_dot(lhs, iqt_ref[h]), 0.0)
                acc = term if acc is None else acc + term
            s_idx = kc * CK + half * HK + lax.broadcasted_iota(jnp.int32, (HK, 1), 0)
            bits = lax.bitcast_convert_type(acc, jnp.int32)
            key = bits ^ ((bits >> 31) & 0x7FFFFFFF)
            key = jnp.where(s_idx <= t_idx, key, INT_MIN)
            key_ref[kc, half * HK:(half + 1) * HK, :] = key
            half_ref[kc, half * HK:(half + 1) * HK, :] = (key >> 16).astype(jnp.int16)
        return carry

    lax.fori_loop(0, nk, score_chunk, 0)
    npair = (nk + 1) // 2

    @pl.when(nk % 2 == 1)
    def _():
        key_ref[nk] = jnp.full((CK, TQ), INT_MIN, jnp.int32)
        half_ref[nk] = jnp.full((CK, TQ), I16_MIN, jnp.int16)

    chains = 4
    groups = 2 * CK // (16 * chains)

    def count_ge(cand):
        cb = jnp.broadcast_to(cand.astype(jnp.int16), (16, TQ))[None, None]

        def count(j, acc):
            keys = half_ref[pl.ds(2 * j, 2)].reshape(groups, chains, 16, TQ)
            for g in range(groups):
                acc = acc + jnp.where(keys[g:g + 1] >= cb, jnp.int16(1), jnp.int16(0))[0]
            return acc

        acc = lax.fori_loop(0, npair, count, jnp.zeros((chains, 16, TQ), jnp.int16))
        return jnp.sum(jnp.sum(acc.astype(jnp.int32), axis=0), axis=0, keepdims=True)

    def select16(need):
        def bit_step(bi, thr):
            cand = thr + jnp.left_shift(jnp.int32(1), 15 - bi)
            return jnp.where(count_ge(cand) >= need, cand, thr)

        return lax.fori_loop(0, 16, bit_step, jnp.full((1, TQ), I16_MIN, jnp.int32))

    thr_hi = select16(ksel)
    above = jnp.where(thr_hi >= -I16_MIN - 1, 0, count_ge(jnp.minimum(thr_hi + 1, -I16_MIN - 1)))

    def low_half(j, carry):
        keys = key_ref[pl.ds(2 * j, 2)]
        low = (keys & 0xFFFF) + I16_MIN
        half_ref[pl.ds(2 * j, 2)] = jnp.where((keys >> 16) == thr_hi, low, I16_MIN).astype(jnp.int16)
        return carry

    lax.fori_loop(0, npair, low_half, 0)
    thr_lo = select16(ksel - above)
    thr = thr_hi * 65536 + (thr_lo - I16_MIN)
    thr = jnp.maximum(thr, INT_MIN + 1)

    kg = CK // 16

    def max_chunk(kc, mxs):
        bias_ref[...] = jnp.where(key_ref[kc] >= thr, 0.0, -1e30)
        kk = k_ref[0, kc]
        new = []
        for h in range(DSA_HEADS):
            s = _dot(kk, qt_ref[h]) + bias_ref[...]
            new.append(jnp.maximum(mxs[h], jnp.max(s.reshape(kg, 2, 8, TQ), axis=0)))
        return tuple(new)

    mxs = lax.fori_loop(0, nk, max_chunk, tuple(jnp.full((2, 8, TQ), -1e30, F32) for _ in range(DSA_HEADS)))
    mb = [jnp.broadcast_to(jnp.max(jnp.max(m, axis=0), axis=0, keepdims=True), (8, TQ)) for m in mxs]
    acc_ref[...] = jnp.zeros_like(acc_ref)

    unit = 64

    def pv_chunk(kc, ls):
        vt = vt_ref[0, kc]
        new = list(ls)
        for u in range(CK // unit):
            rows = slice(u * unit, (u + 1) * unit)
            bias = jnp.where(key_ref[kc, rows, :] >= thr, 0.0, -1e30)
            kk = k_ref[0, kc, rows, :]
            for h in range(DSA_HEADS):
                s = _dot(kk, qt_ref[h]) + bias
                pr = jnp.exp(s.reshape(unit // 8, 8, TQ) - mb[h][None])
                new[h] = new[h] + jnp.sum(pr.reshape(unit // 16, 2, 8, TQ), axis=0)
                p_ref[h, rows, :] = pr.reshape(unit, TQ).astype(BF16)
        for h in range(DSA_HEADS):
            acc_ref[h] += _dot(vt, p_ref[h])
        return tuple(new)

    ls = lax.fori_loop(0, nk, pv_chunk, tuple(jnp.zeros((2, 8, TQ), F32) for _ in range(DSA_HEADS)))
    for h in range(DSA_HEADS):
        denom = jnp.sum(jnp.sum(ls[h], axis=0), axis=0, keepdims=True)
        o_ref[:, h * DSA_HD:(h + 1) * DSA_HD] = (acc_ref[h] / denom).T.astype(o_ref.dtype)


def _dsa(p, qt, k_rot, iqt, ik_rot, wt, B, S):
    TQ, CK = DSA_QBLOCK, DSA_KCHUNK
    NC = S // CK
    nq = S // TQ
    assert NC % 2 == 0 and TQ % CK == 0
    ksel = min(DSA_TOPK, S // 4)
    ik = ik_rot.reshape(B, NC, CK, DSA_IDX_DIM)
    k = k_rot.reshape(B, NC, CK, DSA_HD)
    vt = p[:, _COL["dv"]:_COL["dv"] + DSA_HD].reshape(B, NC, CK, DSA_HD).transpose(0, 1, 3, 2)
    return pl.pallas_call(
        functools.partial(_dsa_kernel, ksel=ksel),
        grid=(B, nq),
        in_specs=[pl.BlockSpec((DSA_IDX_HEADS, DSA_IDX_DIM, TQ), lambda b, i: (0, 0, b * nq + i)),
                  pl.BlockSpec((DSA_IDX_HEADS, TQ), lambda b, i: (0, b * nq + i)),
                  pl.BlockSpec((DSA_HEADS, DSA_HD, TQ), lambda b, i: (0, 0, b * nq + i)),
                  pl.BlockSpec((1, NC, CK, DSA_IDX_DIM), lambda b, i: (b, 0, 0, 0)),
                  pl.BlockSpec((1, NC, CK, DSA_HD), lambda b, i: (b, 0, 0, 0)),
                  pl.BlockSpec((1, NC, DSA_HD, CK), lambda b, i: (b, 0, 0, 0))],
        out_specs=pl.BlockSpec((TQ, DSA_HEADS * DSA_HD), lambda b, i: (b * nq + i, 0)),
        out_shape=jax.ShapeDtypeStruct((B * S, DSA_HEADS * DSA_HD), BF16),
        scratch_shapes=[pltpu.VMEM((NC, CK, TQ), jnp.int32),
                        pltpu.VMEM((NC, CK, TQ), jnp.int16),
                        pltpu.VMEM((CK, TQ), F32),
                        pltpu.VMEM((DSA_HEADS, CK, TQ), BF16),
                        pltpu.VMEM((DSA_HEADS, DSA_HD, TQ), F32)],
        compiler_params=_cparams("arbitrary", "arbitrary"),
        name="dsa_attn",
    )(iqt, wt, qt, ik, k, vt)


def _mem_kv_kernel(mem_ref, g_ref, w_ref, kg_ref, k_out, v_out):
    h = _rms(mem_ref[...], g_ref[...]).astype(BF16)
    kv = _dot(h, w_ref[...])
    width = MEM_HEADS * MEM_HD
    for hd in range(MEM_HEADS):
        sl = slice(hd * MEM_HD, (hd + 1) * MEM_HD)
        k_out[:, sl] = _rms(kv[:, sl], kg_ref[...]).astype(k_out.dtype)
    v_out[...] = kv[:, width:].astype(v_out.dtype)


def _mem_kv(mem2d, gain, w_kv, k_gain, B):
    width = MEM_HEADS * MEM_HD
    out = jax.ShapeDtypeStruct((B * MEM_LEN, width), BF16)
    return pl.pallas_call(
        _mem_kv_kernel,
        grid=(B,),
        in_specs=[pl.BlockSpec((MEM_LEN, D_MODEL), lambda b: (b, 0)),
                  pl.BlockSpec((1, D_MODEL), lambda b: (0, 0)),
                  pl.BlockSpec((D_MODEL, 2 * width), lambda b: (0, 0)),
                  pl.BlockSpec((1, MEM_HD), lambda b: (0, 0))],
        out_specs=[pl.BlockSpec((MEM_LEN, width), lambda b: (b, 0))] * 2,
        out_shape=[out, out],
        compiler_params=_cparams("arbitrary"),
        name="mem_kv",
    )(mem2d, gain, w_kv, k_gain)


def _mem_attn_kernel(q_ref, k_ref, v_ref, qg_ref, o_ref):
    for h in range(MEM_HEADS):
        sl = slice(h * MEM_HD, (h + 1) * MEM_HD)
        q = _rms(q_ref[:, sl].astype(F32), qg_ref[...]) * (MEM_HD ** -0.5)
        s = _dot_nt(q.astype(BF16), k_ref[:, sl])
        pr = jnp.exp(s - jnp.max(s, axis=1, keepdims=True))
        o = _dot(pr.astype(BF16), v_ref[:, sl]) / jnp.sum(pr, axis=1, keepdims=True)
        o_ref[:, sl] = o.astype(o_ref.dtype)


def _mem_attn(p, mk, mv, q_gain, B, S):
    tq = min(512, S)
    nq = S // tq
    width = MEM_HEADS * MEM_HD
    return pl.pallas_call(
        _mem_attn_kernel,
        grid=(B, nq),
        in_specs=[pl.BlockSpec((tq, width), lambda b, i: (b * nq + i, _COL["mq"] // width)),
                  pl.BlockSpec((MEM_LEN, width), lambda b, i: (b, 0)),
                  pl.BlockSpec((MEM_LEN, width), lambda b, i: (b, 0)),
                  pl.BlockSpec((1, MEM_HD), lambda b, i: (0, 0))],
        out_specs=pl.BlockSpec((tq, width), lambda b, i: (b * nq + i, 0)),
        out_shape=jax.ShapeDtypeStruct((B * S, width), BF16),
        compiler_params=_cparams("arbitrary", "arbitrary"),
        name="mem_attn",
    )(p, mk, mv, q_gain)


def _out_proj_kernel(og_ref, ys_ref, od_ref, om_ref, z_ref, x_ref, gn_ref, wglu_ref, w_ref, o_ref, lhs_ref):
    y = jax.nn.gelu(jnp.concatenate([ys_ref[q] for q in range(ys_ref.shape[0])], axis=1))
    o_s5 = y * jax.nn.sigmoid(_dot(y.astype(BF16), wglu_ref[...]))
    branches = (og_ref[...].astype(F32), o_s5, od_ref[...].astype(F32), om_ref[...].astype(F32))
    for bi, o in enumerate(branches):
        for h in range(o.shape[1] // HEAD):
            c0 = bi * 512 + h * HEAD
            z = z_ref[:, c0:c0 + HEAD].astype(F32)
            n = _rms(o[:, h * HEAD:(h + 1) * HEAD], gn_ref[:, c0:c0 + HEAD])
            lhs_ref[:, c0:c0 + HEAD] = (n * (z * jax.nn.sigmoid(z))).astype(BF16)
    o_ref[...] = x_ref[...] + _dot(lhs_ref[...], w_ref[...])


def _out_proj(o_gla, y_s5, o_dsa, o_mem, p, x2d, out_gain, w_glu, w_out):
    T = x2d.shape[0]
    tm = min(256, T)
    branch = pl.BlockSpec((tm, 512), lambda i: (i, 0))
    return pl.pallas_call(
        _out_proj_kernel,
        grid=(T // tm,),
        in_specs=[branch, pl.BlockSpec((S5_WIDTH // 128, tm, 128), lambda i: (0, i, 0)), branch, branch,
                  pl.BlockSpec((tm, MIX_WIDTH), lambda i: (i, 0)),
                  pl.BlockSpec((tm, D_MODEL), lambda i: (i, 0)),
                  pl.BlockSpec((1, MIX_WIDTH), lambda i: (0, 0)),
                  pl.BlockSpec((S5_WIDTH, S5_WIDTH), lambda i: (0, 0)),
                  pl.BlockSpec((MIX_WIDTH, D_MODEL), lambda i: (0, 0))],
        out_specs=pl.BlockSpec((tm, D_MODEL), lambda i: (i, 0)),
        out_shape=jax.ShapeDtypeStruct((T, D_MODEL), F32),
        scratch_shapes=[pltpu.VMEM((tm, MIX_WIDTH), BF16)],
        compiler_params=_cparams("arbitrary"),
        name="out_proj",
    )(o_gla, y_s5, o_dsa, o_mem, p, x2d, out_gain, w_glu, w_out)


def _permute_w_in(w_in):
    pieces = [w_in[..., _ORIG[n][0]:_ORIG[n][1]] for n in _ORDER]
    used = sum(_ORIG[n][1] - _ORIG[n][0] for n in _ORDER)
    pad = jnp.zeros(w_in.shape[:-1] + (IN_PAD - used,), w_in.dtype)
    return jnp.concatenate(pieces + [pad], axis=-1).astype(BF16)


def kernel(x, mem, positions, ln_gain, w_in, gla_w_up, gla_b_up, s5_a_re, s5_a_im, s5_log_step, s5_b_re, s5_b_im, s5_c_re, s5_c_im, s5_d, s5_w_glu, dsa_q_norm, dsa_k_norm, mem_norm, w_mem_kv, mem_q_norm, mem_k_norm, out_norm, w_out):
    B, S, _ = x.shape
    depth = w_in.shape[0]
    T = B * S
    x2d = x.reshape(T, D_MODEL)
    mem2d = mem.reshape(B * mem.shape[1], D_MODEL)
    w_in_p = _permute_w_in(w_in)
    wup_pad = jnp.zeros((depth, 128, GLA_HEADS * GLA_DK), F32)
    wup_pad = wup_pad.at[:, GLR_LANE:GLR_LANE + GLA_GATE_RANK].set(gla_w_up).astype(BF16)
    w_kv = w_mem_kv.astype(BF16)
    w_glu = s5_w_glu.astype(BF16)
    w_o = w_out.astype(BF16)
    tables = _rope_tables(positions)
    for l in range(depth):
        p, su = _in_proj(x2d, ln_gain[l][None], w_in_p[l])
        o_gla = _gla(p, wup_pad[l], gla_b_up[l][None], B, S)
        mats = _s5_mats(s5_a_re[l], s5_a_im[l], s5_log_step[l], s5_b_re[l], s5_b_im[l],
                        s5_c_re[l], s5_c_im[l], s5_d[l])
        y_s5 = _s5(su, mats, B, S)
        qt, k_rot, iqt, ik_rot, wt = _dsa_prep(p, tables, dsa_q_norm[l][None], dsa_k_norm[l][None])
        o_dsa = _dsa(p, qt, k_rot, iqt, ik_rot, wt, B, S)
        mk, mv = _mem_kv(mem2d, mem_norm[l][None], w_kv[l], mem_k_norm[l][None], B)
        o_mem = _mem_attn(p, mk, mv, mem_q_norm[l][None], B, S)
        x2d = _out_proj(o_gla, y_s5, o_dsa, o_mem, p, x2d, out_norm[l][None], w_glu[l], w_o[l])
    return x2d.reshape(B, S, D_MODEL)
```

```python
import functools
import math

import numpy as np
import jax
import jax.numpy as jnp
from jax import lax
from jax.experimental import pallas as pl
from jax.experimental.pallas import tpu as pltpu

F32 = jnp.float32
BF16 = jnp.bfloat16

D_MODEL = 2048
MEM_LEN = 256
ROPE_THETA = 500000.0
ROPE_FRAC = 4
NORM_EPS = 1e-6

GLA_HEADS = 4
GLA_DK = 64
GLA_DV = 128
GLA_GATE_RANK = 16
GLA_TAU = 16.0
GLA_CHUNK = 64

S5_GROUPS = 32
S5_CH = 16
S5_STATE = 64
S5_WIDTH = S5_GROUPS * S5_CH
S5_BLOCK = 16

DSA_HEADS = 4
DSA_HD = 128
DSA_IDX_HEADS = 16
DSA_IDX_DIM = 64
DSA_TOPK = 256
DSA_QBLOCK = 256
DSA_KCHUNK = 256

MEM_HEADS = 4
MEM_HD = 128
HEAD = 128
MIX_WIDTH = 2048

VMEM_LIMIT_BYTES = 52 * 1024 * 1024

_ORIG = dict(gq=(0, 256), gk=(256, 512), gv=(512, 1024), glr=(1024, 1040), gz=(1040, 1552),
             su=(1552, 2064), sz=(2064, 2576), dq=(2576, 3088), dk=(3088, 3216), dv=(3216, 3344),
             diq=(3344, 4368), dik=(4368, 4432), diw=(4432, 4448), dz=(4448, 4960),
             mq=(4960, 5472), mz=(5472, 5984))
_ORDER = ("gz", "sz", "dz", "mz", "diq", "gv", "su", "dq", "mq", "gq", "gk", "dk", "dv",
          "dik", "glr", "diw")
_COL = {}
_c = 0
for _n in _ORDER:
    _COL[_n] = _c
    _c += _ORIG[_n][1] - _ORIG[_n][0]
IN_PAD = 6144
MISC_COL = _COL["dik"]
GLR_LANE = 64
DIW_LANE = 80
INT_MIN = -(2 ** 31)
I16_MIN = -(2 ** 15)


def _cparams(*sem):
    return pltpu.CompilerParams(dimension_semantics=sem, vmem_limit_bytes=VMEM_LIMIT_BYTES)


def _dot(a, b):
    return jnp.dot(a, b, preferred_element_type=F32)


def _dot_nt(a, b):
    return lax.dot_general(a, b, (((1,), (1,)), ((), ())), preferred_element_type=F32)


def _rms(x, gain):
    return x * lax.rsqrt(jnp.mean(x * x, axis=-1, keepdims=True) + NORM_EPS) * gain


def _in_proj_kernel(x_ref, g_ref, w_ref, o_ref, su_ref, h_ref):
    h_ref[...] = _rms(x_ref[...], g_ref[...]).astype(BF16)
    tn = S5_WIDTH
    for j in range(IN_PAD // tn):
        acc = _dot(h_ref[...], w_ref[:, j * tn:(j + 1) * tn])
        o_ref[:, j * tn:(j + 1) * tn] = acc.astype(o_ref.dtype)
        if j == _COL["su"] // tn:
            for q in range(tn // 128):
                su_ref[q] = acc[:, q * 128:(q + 1) * 128]


def _resident(shape):
    return pl.BlockSpec(shape, lambda *_: (0,) * len(shape), pipeline_mode=pl.Buffered(1))


def _in_proj(x2d, gain, w):
    T = x2d.shape[0]
    tm = min(256, T)
    return pl.pallas_call(
        _in_proj_kernel,
        grid=(T // tm,),
        in_specs=[pl.BlockSpec((tm, D_MODEL), lambda i: (i, 0)),
                  _resident((1, D_MODEL)),
                  _resident((D_MODEL, IN_PAD))],
        out_specs=[pl.BlockSpec((tm, IN_PAD), lambda i: (i, 0)),
                   pl.BlockSpec((S5_WIDTH // 128, tm, 128), lambda i: (0, i, 0))],
        out_shape=[jax.ShapeDtypeStruct((T, IN_PAD), BF16),
                   jax.ShapeDtypeStruct((S5_WIDTH // 128, T, 128), F32)],
        scratch_shapes=[pltpu.VMEM((tm, D_MODEL), BF16)],
        compiler_params=_cparams("arbitrary"),
        name="in_proj",
    )(x2d, gain, w)


def _gla_kernel(q_ref, k_ref, v_ref, misc_ref, wup_ref, bup_ref, o_ref, st_ref):
    @pl.when(pl.program_id(1) == 0)
    def _():
        st_ref[...] = jnp.zeros_like(st_ref)

    C = GLA_CHUNK
    x = _dot(misc_ref[...], wup_ref[...]) + bup_ref[...]
    log_a = (jnp.minimum(x, 0.0) - jnp.log(1.0 + jnp.exp(-jnp.abs(x)))) * (1.0 / GLA_TAU)
    row = lax.broadcasted_iota(jnp.int32, (C, C), 0)
    col = lax.broadcasted_iota(jnp.int32, (C, C), 1)
    tril = row >= col
    tri = tril.astype(F32)
    lane = lax.broadcasted_iota(jnp.int32, (1, 2 * GLA_DK), 1)
    low = lane < GLA_DK
    for c in range(q_ref.shape[0] // C):
        rs = slice(c * C, (c + 1) * C)
        b = jnp.dot(tri, log_a[rs], preferred_element_type=F32, precision=lax.Precision.HIGHEST)
        b_last = b[C - 1:C]
        mid = b[C // 2:C // 2 + 1]
        qf = q_ref[rs, :].astype(F32) * (GLA_DK ** -0.5)
        kf = k_ref[rs, :].astype(F32)
        qe = qf * jnp.exp(b - mid)
        ke = kf * jnp.exp(mid - b)
        kd = kf * jnp.exp(b_last - b)
        qi = qf * jnp.exp(b)
        dec = jnp.exp(b_last)
        for j in range(GLA_HEADS // 2):
            ls = slice(j * 128, (j + 1) * 128)
            ke_p = ke[:, ls].astype(BF16)
            kd_p = kd[:, ls].astype(BF16)
            for hh in range(2):
                h = 2 * j + hh
                sel = low if hh == 0 else jnp.logical_not(low)
                qe_h = jnp.where(sel, qe[:, ls], 0.0).astype(BF16)
                qi_h = jnp.where(sel, qi[:, ls], 0.0).astype(BF16)
                attn = jnp.where(tril, _dot_nt(qe_h, ke_p), 0.0)
                v_h = v_ref[rs, h * GLA_DV:(h + 1) * GLA_DV]
                st = st_ref[h]
                o = _dot(attn.astype(BF16), v_h) + _dot_nt(qi_h, st.astype(BF16))
                v_t = v_h.astype(F32).T.astype(BF16)
                st_ref[h] = st * dec[:, ls] + _dot(v_t, kd_p)
                o_ref[rs, h * GLA_DV:(h + 1) * GLA_DV] = o.astype(o_ref.dtype)


def _gla(p, wup_pad, bup, B, S):
    tc = min(256, S)
    nc = S // tc
    row = lambda b, c: b * nc + c
    return pl.pallas_call(
        _gla_kernel,
        grid=(B, nc),
        in_specs=[pl.BlockSpec((tc, 256), lambda b, c: (row(b, c), _COL["gq"] // 256)),
                  pl.BlockSpec((tc, 256), lambda b, c: (row(b, c), _COL["gk"] // 256)),
                  pl.BlockSpec((tc, 512), lambda b, c: (row(b, c), _COL["gv"] // 512)),
                  pl.BlockSpec((tc, 128), lambda b, c: (row(b, c), MISC_COL // 128)),
                  pl.BlockSpec((128, 256), lambda b, c: (0, 0)),
                  pl.BlockSpec((1, 256), lambda b, c: (0, 0))],
        out_specs=pl.BlockSpec((tc, 512), lambda b, c: (row(b, c), 0)),
        out_shape=jax.ShapeDtypeStruct((B * S, GLA_HEADS * GLA_DV), BF16),
        scratch_shapes=[pltpu.VMEM((GLA_HEADS, GLA_DV, 2 * GLA_DK), F32)],
        compiler_params=_cparams("arbitrary", "arbitrary"),
        name="gla",
    )(p, p, p, p, wup_pad, bup)


def _s5_mats(a_re, a_im, log_step, b_re, b_im, c_re, c_im, d_skip):
    L, P, CH = S5_BLOCK, S5_STATE, S5_CH
    step = jnp.exp(log_step.astype(F32))[:, None]
    are, aim = a_re.astype(F32), a_im.astype(F32)
    j = jnp.arange(L + 1, dtype=F32)[None, :, None]
    mag = jnp.exp(are[:, None, :] * step[:, None, :] * j)
    ang = aim[:, None, :] * step[:, None, :] * j
    pw_re, pw_im = mag * jnp.cos(ang), mag * jnp.sin(ang)
    lb_re, lb_im = pw_re[:, 1], pw_im[:, 1]
    den = are * are + aim * aim
    f_re = ((lb_re - 1.0) * are + lb_im * aim) / den
    f_im = (lb_im * are - (lb_re - 1.0) * aim) / den
    bb_re = f_re[..., None] * b_re - f_im[..., None] * b_im
    bb_im = f_re[..., None] * b_im + f_im[..., None] * b_re
    lb_b_re = pw_re[..., None] * bb_re[:, None] - pw_im[..., None] * bb_im[:, None]
    lb_b_im = pw_re[..., None] * bb_im[:, None] + pw_im[..., None] * bb_re[:, None]
    kj = jnp.einsum("gcp,gjpd->gjcd", c_re, lb_b_re) - jnp.einsum("gcp,gjpd->gjcd", c_im, lb_b_im)
    t = np.arange(L)
    lag = t[None, :] - t[:, None]
    m = kj[:, np.clip(lag, 0, L)]
    m = jnp.where((lag >= 0)[None, :, :, None, None], m, 0.0)
    m = jnp.transpose(m, (0, 1, 4, 2, 3))
    eye = (lag == 0)[None, :, None, :, None] * jnp.eye(CH, dtype=F32)[None, None, :, None, :]
    m = m + eye * d_skip.reshape(S5_GROUPS, 1, 1, 1, CH)
    m = m.reshape(S5_GROUPS, L * CH, L * CH)
    p_re = jnp.transpose(lb_b_re[:, L - 1 - t], (0, 1, 3, 2)).reshape(S5_GROUPS, L * CH, P)
    p_im = jnp.transpose(lb_b_im[:, L - 1 - t], (0, 1, 3, 2)).reshape(S5_GROUPS, L * CH, P)
    pm = jnp.concatenate([p_re, p_im, p_im, p_re], axis=-1)
    cl_re = c_re[:, None] * pw_re[:, 1:, None, :] - c_im[:, None] * pw_im[:, 1:, None, :]
    cl_im = c_re[:, None] * pw_im[:, 1:, None, :] + c_im[:, None] * pw_re[:, 1:, None, :]
    q_re = jnp.transpose(cl_re, (0, 3, 1, 2)).reshape(S5_GROUPS, P, L * CH)
    q_im = -jnp.transpose(cl_im, (0, 3, 1, 2)).reshape(S5_GROUPS, P, L * CH)
    qm = jnp.concatenate([q_re, q_im], axis=1)
    al_re, al_im = pw_re[:, L], pw_im[:, L]
    ar = jnp.concatenate([al_re, al_re], axis=-1).reshape(1, S5_GROUPS * 2 * P)
    ai_a = jnp.concatenate([-al_im, al_im], axis=-1).reshape(1, S5_GROUPS * 2 * P)
    ai_b = jnp.concatenate([al_im, -al_im], axis=-1).reshape(1, S5_GROUPS * 2 * P)
    return m.astype(BF16), pm.astype(BF16), qm.astype(BF16), ar, ai_a, ai_b


def _s5_perm():
    src = np.arange(1024).reshape(8, 8, S5_CH).transpose(1, 0, 2).reshape(-1)
    m = np.zeros((1024, 1024), np.float32)
    m[np.arange(1024), src] = 1.0
    return jnp.asarray(m, BF16)


def _s5_kernel(su_ref, perm_ref, m_ref, p_ref, q_ref, ar_ref, aia_ref, aib_ref, y_ref,
               xcat_ref, u_ref, va_ref, vb_ref, x_ref, ycat_ref):
    L = S5_BLOCK
    NB = su_ref.shape[1] // L
    GH = 8
    W = 2 * S5_STATE
    for s in range(L):
        xcat_ref[:, s * 128:(s + 1) * 128] = su_ref[0, pl.ds(s, NB, stride=L), :].astype(BF16)
    for half in range(2):
        u_ref[half] = _dot(xcat_ref[:, half * 1024:(half + 1) * 1024], perm_ref[...]).astype(BF16)

    def u_of(g):
        return jnp.concatenate([u_ref[0, :, g * 128:(g + 1) * 128], u_ref[1, :, g * 128:(g + 1) * 128]], axis=1)

    for g in range(GH):
        v2 = _dot(u_of(g), p_ref[g])
        va_ref[:, g * W:(g + 1) * W] = v2[:, :W]
        vb_ref[:, g * W:(g + 1) * W] = v2[:, W:]
    ar, aia, aib = ar_ref[...], aia_ref[...], aib_ref[...]

    def step(n, carry):
        xa, xb = carry
        x_ref[pl.ds(n, 1), :] = xa
        na = xa * ar + xb * aia + va_ref[pl.ds(n, 1), :]
        nb = xb * ar + xa * aib + vb_ref[pl.ds(n, 1), :]
        return na, nb

    z = jnp.zeros((1, GH * W), F32)
    lax.fori_loop(0, NB, step, (z, z), unroll=8)
    for g in range(GH):
        y = _dot(u_of(g), m_ref[g]) + _dot(x_ref[:, g * W:(g + 1) * W].astype(BF16), q_ref[g])
        for half in range(2):
            ycat_ref[half, :, g * 128:(g + 1) * 128] = y[:, half * 128:(half + 1) * 128].astype(BF16)
    for half in range(2):
        out = _dot(ycat_ref[half], perm_ref[...])
        for t in range(8):
            y_ref[0, pl.ds(half * 8 + t, NB, stride=L), :] = out[:, t * 128:(t + 1) * 128]


def _s5(su, mats, B, S):
    m, pm, qm, ar, aia, aib = mats
    L, G, CH = S5_BLOCK, S5_GROUPS, S5_CH
    NB = S // L
    GH = 8
    W = 2 * S5_STATE
    slab = pl.BlockSpec((1, S, 128), lambda b, h: (h, b, 0))
    coef = pl.BlockSpec((1, GH * W), lambda b, h: (0, h))
    return pl.pallas_call(
        _s5_kernel,
        grid=(B, G // GH),
        in_specs=[slab,
                  pl.BlockSpec((1024, 1024), lambda b, h: (0, 0)),
                  pl.BlockSpec((GH, L * CH, L * CH), lambda b, h: (h, 0, 0)),
                  pl.BlockSpec((GH, L * CH, 2 * W), lambda b, h: (h, 0, 0)),
                  pl.BlockSpec((GH, W, L * CH), lambda b, h: (h, 0, 0)),
                  coef, coef, coef],
        out_specs=slab,
        out_shape=jax.ShapeDtypeStruct((G // GH, B * S, 128), F32),
        scratch_shapes=[pltpu.VMEM((NB, L * 128), BF16), pltpu.VMEM((2, NB, 1024), BF16),
                        pltpu.VMEM((NB, GH * W), F32), pltpu.VMEM((NB, GH * W), F32),
                        pltpu.VMEM((NB, GH * W), F32), pltpu.VMEM((2, NB, 1024), BF16)],
        compiler_params=_cparams("arbitrary", "arbitrary"),
        name="s5_scan",
    )(su, _s5_perm(), m, pm, qm, ar, aia, aib)


def _rope_kernel(pos_ref, f128_ref, f64_ref, c128_ref, s128_ref, c64_ref, s64_ref):
    pos = pos_ref[...]
    a = pos * f128_ref[...]
    c128_ref[...] = jnp.cos(a)
    s128_ref[...] = jnp.sin(a)
    a = pos * f64_ref[...]
    c64_ref[...] = jnp.cos(a)
    s64_ref[...] = jnp.sin(a)


def _rope_freq(d, width):
    rd = d // ROPE_FRAC
    half = rd // 2
    inv = ROPE_THETA ** (-jnp.arange(half, dtype=F32) * 2.0 / rd)
    per_head = jnp.concatenate([inv, inv, jnp.zeros((d - rd,), F32)])
    return jnp.tile(per_head, width // d).reshape(1, width)


def _rope_tables(positions):
    T = positions.size
    pos = positions.astype(F32).reshape(T, 1)
    tq = min(512, T)
    out = jax.ShapeDtypeStruct((T, 128), F32)
    return pl.pallas_call(
        _rope_kernel,
        grid=(T // tq,),
        in_specs=[pl.BlockSpec((tq, 1), lambda i: (i, 0)),
                  pl.BlockSpec((1, 128), lambda i: (0, 0)),
                  pl.BlockSpec((1, 128), lambda i: (0, 0))],
        out_specs=[pl.BlockSpec((tq, 128), lambda i: (i, 0))] * 4,
        out_shape=[out] * 4,
        compiler_params=_cparams("arbitrary"),
        name="rope_tables",
    )(pos, _rope_freq(DSA_HD, 128), _rope_freq(DSA_IDX_DIM, 128))


def _rope(x, cos, sin, head_dim):
    width = x.shape[-1]
    half = head_dim // ROPE_FRAC // 2
    lane = lax.broadcasted_iota(jnp.int32, (1, width), 1)
    per = lane % head_dim
    sgn_lo = jnp.where(per < half, -1.0, 0.0)
    sgn_hi = jnp.where((per >= half) & (per < 2 * half), 1.0, 0.0)
    up = pltpu.roll(x, width - half, axis=1)
    dn = pltpu.roll(x, half, axis=1)
    return x * cos + sin * (up * sgn_lo + dn * sgn_hi)


def _dsa_prep_kernel(dq_ref, dk_ref, diq_ref, misc_ref, c128_ref, s128_ref, c64_ref, s64_ref,
                     qg_ref, kg_ref, qt_out, k_out, iqt_out, ik_out, wt_out):
    c128, s128 = c128_ref[...], s128_ref[...]
    c64, s64 = c64_ref[...], s64_ref[...]
    for h in range(DSA_HEADS):
        sl = slice(h * DSA_HD, (h + 1) * DSA_HD)
        q = _rope(_rms(dq_ref[:, sl].astype(F32), qg_ref[...]), c128, s128, DSA_HD)
        qt_out[h] = (q * (DSA_HD ** -0.5)).T.astype(qt_out.dtype)
    k = _rope(_rms(dk_ref[...].astype(F32), kg_ref[...]), c128, s128, DSA_HD)
    k_out[...] = k.astype(k_out.dtype)
    for j in range(DSA_IDX_HEADS // 2):
        sl = slice(j * 128, (j + 1) * 128)
        xt = _rope(diq_ref[:, sl].astype(F32), c64, s64, DSA_IDX_DIM).T
        iqt_out[2 * j] = xt[:DSA_IDX_DIM].astype(iqt_out.dtype)
        iqt_out[2 * j + 1] = xt[DSA_IDX_DIM:].astype(iqt_out.dtype)
    misc = misc_ref[...].astype(F32)
    ik_out[...] = _rope(misc, c64, s64, DSA_IDX_DIM)[:, :DSA_IDX_DIM].astype(ik_out.dtype)
    wt_out[...] = misc.T[DIW_LANE:DIW_LANE + DSA_IDX_HEADS]


def _dsa_prep(p, tables, q_gain, k_gain):
    T = p.shape[0]
    tq = min(512, T)
    tab = pl.BlockSpec((tq, 128), lambda i: (i, 0))
    gain = pl.BlockSpec((1, 128), lambda i: (0, 0))
    return pl.pallas_call(
        _dsa_prep_kernel,
        grid=(T // tq,),
        in_specs=[pl.BlockSpec((tq, 512), lambda i: (i, _COL["dq"] // 512)),
                  pl.BlockSpec((tq, 128), lambda i: (i, _COL["dk"] // 128)),
                  pl.BlockSpec((tq, 1024), lambda i: (i, _COL["diq"] // 1024)),
                  pl.BlockSpec((tq, 128), lambda i: (i, MISC_COL // 128)),
                  tab, tab, tab, tab, gain, gain],
        out_specs=[pl.BlockSpec((DSA_HEADS, DSA_HD, tq), lambda i: (0, 0, i)),
                   pl.BlockSpec((tq, DSA_HD), lambda i: (i, 0)),
                   pl.BlockSpec((DSA_IDX_HEADS, DSA_IDX_DIM, tq), lambda i: (0, 0, i)),
                   pl.BlockSpec((tq, DSA_IDX_DIM), lambda i: (i, 0)),
                   pl.BlockSpec((DSA_IDX_HEADS, tq), lambda i: (0, i))],
        out_shape=[jax.ShapeDtypeStruct((DSA_HEADS, DSA_HD, T), BF16),
                   jax.ShapeDtypeStruct((T, DSA_HD), BF16),
                   jax.ShapeDtypeStruct((DSA_IDX_HEADS, DSA_IDX_DIM, T), BF16),
                   jax.ShapeDtypeStruct((T, DSA_IDX_DIM), BF16),
                   jax.ShapeDtypeStruct((DSA_IDX_HEADS, T), F32)],
        compiler_params=_cparams("arbitrary"),
        name="dsa_prep",
    )(p, p, p, p, *tables, q_gain, k_gain)


def _dsa_kernel(iqt_ref, wt_ref, qt_ref, ik_ref, k_ref, vt_ref, o_ref,
                key_ref, half_ref, bias_ref, p_ref, acc_ref, *, ksel):
    TQ, CK = DSA_QBLOCK, DSA_KCHUNK
    HK = CK // 2
    i = pl.program_id(1)
    nk = (i * TQ + TQ + CK - 1) // CK
    t_idx = i * TQ + lax.broadcasted_iota(jnp.int32, (1, TQ), 1)
    w_all = wt_ref[...] * (DSA_IDX_HEADS ** -0.5 * DSA_IDX_DIM ** -0.5)

    def score_chunk(kc, carry):
        for half in range(2):
            lhs = ik_ref[0, kc, half * HK:(half + 1) * HK, :]
            acc = None
            for h in range(DSA_IDX_HEADS):
                term = w_all[h:h + 1] * jnp.maximum(_dot(lhs, iqt_ref[h]), 0.0)
                acc = term if acc is None else acc + term
            s_idx = kc * CK + half * HK + lax.broadcasted_iota(jnp.int32, (HK, 1), 0)
            bits = lax.bitcast_convert_type(acc, jnp.int32)
            key = bits ^ ((bits >> 31) & 0x7FFFFFFF)
            key = jnp.where(s_idx <= t_idx, key, INT_MIN)
            key_ref[kc, half * HK:(half + 1) * HK, :] = key
            half_ref[kc, half * HK:(half + 1) * HK, :] = (key >> 16).astype(jnp.int16)
        return carry

    lax.fori_loop(0, nk, score_chunk, 0)
    npair = (nk + 1) // 2

    @pl.when(nk % 2 == 1)
    def _():
        key_ref[nk] = jnp.full((CK, TQ), INT_MIN, jnp.int32)
        half_ref[nk] = jnp.full((CK, TQ), I16_MIN, jnp.int16)

    chains = 4
    groups = 2 * CK // (16 * chains)

    def count_ge(cand):
        cb = jnp.broadcast_to(cand.astype(jnp.int16), (16, TQ))[None, None]

        def count(j, acc):
            keys = half_ref[pl.ds(2 * j, 2)].reshape(groups, chains, 16, TQ)
            for g in range(groups):
                acc = acc + jnp.where(keys[g:g + 1] >= cb, jnp.int16(1), jnp.int16(0))[0]
            return acc

        acc = lax.fori_loop(0, npair, count, jnp.zeros((chains, 16, TQ), jnp.int16))
        return jnp.sum(jnp.sum(acc.astype(jnp.int32), axis=0), axis=0, keepdims=True)

    def select16(need):
        def bit_step(bi, thr):
            cand = thr + jnp.left_shift(jnp.int32(1), 15 - bi)
            return jnp.where(count_ge(cand) >= need, cand, thr)

        return lax.fori_loop(0, 16, bit_step, jnp.full((1, TQ), I16_MIN, jnp.int32))

    thr_hi = select16(ksel)
    above = jnp.where(thr_hi >= -I16_MIN - 1, 0, count_ge(jnp.minimum(thr_hi + 1, -I16_MIN - 1)))

    def low_half(j, carry):
        keys = key_ref[pl.ds(2 * j, 2)]
        low = (keys & 0xFFFF) + I16_MIN
        half_ref[pl.ds(2 * j, 2)] = jnp.where((keys >> 16) == thr_hi, low, I16_MIN).astype(jnp.int16)
        return carry

    lax.fori_loop(0, npair, low_half, 0)
    thr_lo = select16(ksel - above)
    thr = thr_hi * 65536 + (thr_lo - I16_MIN)
    thr = jnp.maximum(thr, INT_MIN + 1)

    kg = CK // 16

    def max_chunk(kc, mxs):
        bias_ref[...] = jnp.where(key_ref[kc] >= thr, 0.0, -1e30)
        kk = k_ref[0, kc]
        new = []
        for h in range(DSA_HEADS):
            s = _dot(kk, qt_ref[h]) + bias_ref[...]
            new.append(jnp.maximum(mxs[h], jnp.max(s.reshape(kg, 2, 8, TQ), axis=0)))
        return tuple(new)

    mxs = lax.fori_loop(0, nk, max_chunk, tuple(jnp.full((2, 8, TQ), -1e30, F32) for _ in range(DSA_HEADS)))
    mb = [jnp.broadcast_to(jnp.max(jnp.max(m, axis=0), axis=0, keepdims=True), (8, TQ)) for m in mxs]
    acc_ref[...] = jnp.zeros_like(acc_ref)

    unit = 64

    def pv_chunk(kc, ls):
        vt = vt_ref[0, kc]
        new = list(ls)
        for u in range(CK // unit):
            rows = slice(u * unit, (u + 1) * unit)
            bias = jnp.where(key_ref[kc, rows, :] >= thr, 0.0, -1e30)
            kk = k_ref[0, kc, rows, :]
            for h in range(DSA_HEADS):
                s = _dot(kk, qt_ref[h]) + bias
                pr = jnp.exp(s.reshape(unit // 8, 8, TQ) - mb[h][None])
                new[h] = new[h] + jnp.sum(pr.reshape(unit // 16, 2, 8, TQ), axis=0)
                p_ref[h, rows, :] = pr.reshape(unit, TQ).astype(BF16)
        for h in range(DSA_HEADS):
            acc_ref[h] += _dot(vt, p_ref[h])
        return tuple(new)

    ls = lax.fori_loop(0, nk, pv_chunk, tuple(jnp.zeros((2, 8, TQ), F32) for _ in range(DSA_HEADS)))
    for h in range(DSA_HEADS):
        denom = jnp.sum(jnp.sum(ls[h], axis=0), axis=0, keepdims=True)
        o_ref[:, h * DSA_HD:(h + 1) * DSA_HD] = (acc_ref[h] / denom).T.astype(o_ref.dtype)


def _dsa(p, qt, k_rot, iqt, ik_rot, wt, B, S):
    TQ, CK = DSA_QBLOCK, DSA_KCHUNK
    NC = S // CK
    nq = S // TQ
    assert NC % 2 == 0 and TQ % CK == 0
    ksel = min(DSA_TOPK, S // 4)
    ik = ik_rot.reshape(B, NC, CK, DSA_IDX_DIM)
    k = k_rot.reshape(B, NC, CK, DSA_HD)
    vt = p[:, _COL["dv"]:_COL["dv"] + DSA_HD].reshape(B, NC, CK, DSA_HD).transpose(0, 1, 3, 2)
    return pl.pallas_call(
        functools.partial(_dsa_kernel, ksel=ksel),
        grid=(B, nq),
        in_specs=[pl.BlockSpec((DSA_IDX_HEADS, DSA_IDX_DIM, TQ), lambda b, i: (0, 0, b * nq + i)),
                  pl.BlockSpec((DSA_IDX_HEADS, TQ), lambda b, i: (0, b * nq + i)),
                  pl.BlockSpec((DSA_HEADS, DSA_HD, TQ), lambda b, i: (0, 0, b * nq + i)),
                  pl.BlockSpec((1, NC, CK, DSA_IDX_DIM), lambda b, i: (b, 0, 0, 0)),
                  pl.BlockSpec((1, NC, CK, DSA_HD), lambda b, i: (b, 0, 0, 0)),
                  pl.BlockSpec((1, NC, DSA_HD, CK), lambda b, i: (b, 0, 0, 0))],
        out_specs=pl.BlockSpec((TQ, DSA_HEADS * DSA_HD), lambda b, i: (b * nq + i, 0)),
        out_shape=jax.ShapeDtypeStruct((B * S, DSA_HEADS * DSA_HD), BF16),
        scratch_shapes=[pltpu.VMEM((NC, CK, TQ), jnp.int32),
                        pltpu.VMEM((NC, CK, TQ), jnp.int16),
                        pltpu.VMEM((CK, TQ), F32),
                        pltpu.VMEM((DSA_HEADS, CK, TQ), BF16),
                        pltpu.VMEM((DSA_HEADS, DSA_HD, TQ), F32)],
        compiler_params=_cparams("arbitrary", "arbitrary"),
        name="dsa_attn",
    )(iqt, wt, qt, ik, k, vt)


def _mem_kv_kernel(mem_ref, g_ref, w_ref, kg_ref, k_out, v_out):
    h = _rms(mem_ref[...], g_ref[...]).astype(BF16)
    kv = _dot(h, w_ref[...])
    width = MEM_HEADS * MEM_HD
    for hd in range(MEM_HEADS):
        sl = slice(hd * MEM_HD, (hd + 1) * MEM_HD)
        k_out[:, sl] = _rms(kv[:, sl], kg_ref[...]).astype(k_out.dtype)
    v_out[...] = kv[:, width:].astype(v_out.dtype)


def _mem_kv(mem2d, gain, w_kv, k_gain, B):
    width = MEM_HEADS * MEM_HD
    out = jax.ShapeDtypeStruct((B * MEM_LEN, width), BF16)
    return pl.pallas_call(
        _mem_kv_kernel,
        grid=(B,),
        in_specs=[pl.BlockSpec((MEM_LEN, D_MODEL), lambda b: (b, 0)),
                  pl.BlockSpec((1, D_MODEL), lambda b: (0, 0)),
                  pl.BlockSpec((D_MODEL, 2 * width), lambda b: (0, 0)),
                  pl.BlockSpec((1, MEM_HD), lambda b: (0, 0))],
        out_specs=[pl.BlockSpec((MEM_LEN, width), lambda b: (b, 0))] * 2,
        out_shape=[out, out],
        compiler_params=_cparams("arbitrary"),
        name="mem_kv",
    )(mem2d, gain, w_kv, k_gain)


def _mem_attn_kernel(q_ref, k_ref, v_ref, qg_ref, o_ref):
    for h in range(MEM_HEADS):
        sl = slice(h * MEM_HD, (h + 1) * MEM_HD)
        q = _rms(q_ref[:, sl].astype(F32), qg_ref[...]) * (MEM_HD ** -0.5)
        s = _dot_nt(q.astype(BF16), k_ref[:, sl])
        pr = jnp.exp(s - jnp.max(s, axis=1, keepdims=True))
        o = _dot(pr.astype(BF16), v_ref[:, sl]) / jnp.sum(pr, axis=1, keepdims=True)
        o_ref[:, sl] = o.astype(o_ref.dtype)


def _mem_attn(p, mk, mv, q_gain, B, S):
    tq = min(512, S)
    nq = S // tq
    width = MEM_HEADS * MEM_HD
    return pl.pallas_call(
        _mem_attn_kernel,
        grid=(B, nq),
        in_specs=[pl.BlockSpec((tq, width), lambda b, i: (b * nq + i, _COL["mq"] // width)),
                  pl.BlockSpec((MEM_LEN, width), lambda b, i: (b, 0)),
                  pl.BlockSpec((MEM_LEN, width), lambda b, i: (b, 0)),
                  pl.BlockSpec((1, MEM_HD), lambda b, i: (0, 0))],
        out_specs=pl.BlockSpec((tq, width), lambda b, i: (b * nq + i, 0)),
        out_shape=jax.ShapeDtypeStruct((B * S, width), BF16),
        compiler_params=_cparams("arbitrary", "arbitrary"),
        name="mem_attn",
    )(p, mk, mv, q_gain)


def _out_proj_kernel(og_ref, ys_ref, od_ref, om_ref, z_ref, x_ref, gn_ref, wglu_ref, w_ref, o_ref, lhs_ref):
    y = jax.nn.gelu(jnp.concatenate([ys_ref[q] for q in range(ys_ref.shape[0])], axis=1))
    o_s5 = y * jax.nn.sigmoid(_dot(y.astype(BF16), wglu_ref[...]))
    branches = (og_ref[...].astype(F32), o_s5, od_ref[...].astype(F32), om_ref[...].astype(F32))
    for bi, o in enumerate(branches):
        for h in range(o.shape[1] // HEAD):
            c0 = bi * 512 + h * HEAD
            z = z_ref[:, c0:c0 + HEAD].astype(F32)
            n = _rms(o[:, h * HEAD:(h + 1) * HEAD], gn_ref[:, c0:c0 + HEAD])
            lhs_ref[:, c0:c0 + HEAD] = (n * (z * jax.nn.sigmoid(z))).astype(BF16)
    o_ref[...] = x_ref[...] + _dot(lhs_ref[...], w_ref[...])


def _out_proj(o_gla, y_s5, o_dsa, o_mem, p, x2d, out_gain, w_glu, w_out):
    T = x2d.shape[0]
    tm = min(512, T)
    branch = pl.BlockSpec((tm, 512), lambda i: (i, 0))
    return pl.pallas_call(
        _out_proj_kernel,
        grid=(T // tm,),
        in_specs=[branch, pl.BlockSpec((S5_WIDTH // 128, tm, 128), lambda i: (0, i, 0)), branch, branch,
                  pl.BlockSpec((tm, MIX_WIDTH), lambda i: (i, 0)),
                  pl.BlockSpec((tm, D_MODEL), lambda i: (i, 0)),
                  _resident((1, MIX_WIDTH)),
                  _resident((S5_WIDTH, S5_WIDTH)),
                  _resident((MIX_WIDTH, D_MODEL))],
        out_specs=pl.BlockSpec((tm, D_MODEL), lambda i: (i, 0)),
        out_shape=jax.ShapeDtypeStruct((T, D_MODEL), F32),
        scratch_shapes=[pltpu.VMEM((tm, MIX_WIDTH), BF16)],
        compiler_params=_cparams("arbitrary"),
        name="out_proj",
    )(o_gla, y_s5, o_dsa, o_mem, p, x2d, out_gain, w_glu, w_out)


def _permute_w_in(w_in):
    pieces = [w_in[..., _ORIG[n][0]:_ORIG[n][1]] for n in _ORDER]
    used = sum(_ORIG[n][1] - _ORIG[n][0] for n in _ORDER)
    pad = jnp.zeros(w_in.shape[:-1] + (IN_PAD - used,), w_in.dtype)
    return jnp.concatenate(pieces + [pad], axis=-1).astype(BF16)


def kernel(x, mem, positions, ln_gain, w_in, gla_w_up, gla_b_up, s5_a_re, s5_a_im, s5_log_step, s5_b_re, s5_b_im, s5_c_re, s5_c_im, s5_d, s5_w_glu, dsa_q_norm, dsa_k_norm, mem_norm, w_mem_kv, mem_q_norm, mem_k_norm, out_norm, w_out):
    B, S, _ = x.shape
    depth = w_in.shape[0]
    T = B * S
    x2d = x.reshape(T, D_MODEL)
    mem2d = mem.reshape(B * mem.shape[1], D_MODEL)
    w_in_p = _permute_w_in(w_in)
    wup_pad = jnp.zeros((depth, 128, GLA_HEADS * GLA_DK), F32)
    wup_pad = wup_pad.at[:, GLR_LANE:GLR_LANE + GLA_GATE_RANK].set(gla_w_up).astype(BF16)
    w_kv = w_mem_kv.astype(BF16)
    w_glu = s5_w_glu.astype(BF16)
    w_o = w_out.astype(BF16)
    tables = _rope_tables(positions)
    for l in range(depth):
        p, su = _in_proj(x2d, ln_gain[l][None], w_in_p[l])
        o_gla = _gla(p, wup_pad[l], gla_b_up[l][None], B, S)
        mats = _s5_mats(s5_a_re[l], s5_a_im[l], s5_log_step[l], s5_b_re[l], s5_b_im[l],
                        s5_c_re[l], s5_c_im[l], s5_d[l])
        y_s5 = _s5(su, mats, B, S)
        qt, k_rot, iqt, ik_rot, wt = _dsa_prep(p, tables, dsa_q_norm[l][None], dsa_k_norm[l][None])
        o_dsa = _dsa(p, qt, k_rot, iqt, ik_rot, wt, B, S)
        mk, mv = _mem_kv(mem2d, mem_norm[l][None], w_kv[l], mem_k_norm[l][None], B)
        o_mem = _mem_attn(p, mk, mv, mem_q_norm[l][None], B, S)
        x2d = _out_proj(o_gla, y_s5, o_dsa, o_mem, p, x2d, out_norm[l][None], w_glu[l], w_o[l])
    return x2d.reshape(B, S, D_MODEL)
```

```python
import functools
import math

import numpy as np
import jax
import jax.numpy as jnp
from jax import lax
from jax.experimental import pallas as pl
from jax.experimental.pallas import tpu as pltpu

F32 = jnp.float32
BF16 = jnp.bfloat16

D_MODEL = 2048
MEM_LEN = 256
ROPE_THETA = 500000.0
ROPE_FRAC = 4
NORM_EPS = 1e-6

GLA_HEADS = 4
GLA_DK = 64
GLA_DV = 128
GLA_GATE_RANK = 16
GLA_TAU = 16.0
GLA_CHUNK = 64

S5_GROUPS = 32
S5_CH = 16
S5_STATE = 64
S5_WIDTH = S5_GROUPS * S5_CH
S5_BLOCK = 16

DSA_HEADS = 4
DSA_HD = 128
DSA_IDX_HEADS = 16
DSA_IDX_DIM = 64
DSA_TOPK = 256
DSA_QBLOCK = 256
DSA_KCHUNK = 256

MEM_HEADS = 4
MEM_HD = 128
HEAD = 128
MIX_WIDTH = 2048

VMEM_LIMIT_BYTES = 52 * 1024 * 1024

_ORIG = dict(gq=(0, 256), gk=(256, 512), gv=(512, 1024), glr=(1024, 1040), gz=(1040, 1552),
             su=(1552, 2064), sz=(2064, 2576), dq=(2576, 3088), dk=(3088, 3216), dv=(3216, 3344),
             diq=(3344, 4368), dik=(4368, 4432), diw=(4432, 4448), dz=(4448, 4960),
             mq=(4960, 5472), mz=(5472, 5984))
_ORDER = ("gz", "sz", "dz", "mz", "diq", "gv", "su", "dq", "mq", "gq", "gk", "dk", "dv",
          "dik", "glr", "diw")
_COL = {}
_c = 0
for _n in _ORDER:
    _COL[_n] = _c
    _c += _ORIG[_n][1] - _ORIG[_n][0]
IN_PAD = 6144
MISC_COL = _COL["dik"]
GLR_LANE = 64
DIW_LANE = 80
INT_MIN = -(2 ** 31)
I16_MIN = -(2 ** 15)


def _cparams(*sem):
    return pltpu.CompilerParams(dimension_semantics=sem, vmem_limit_bytes=VMEM_LIMIT_BYTES)


def _dot(a, b):
    return jnp.dot(a, b, preferred_element_type=F32)


def _dot_nt(a, b):
    return lax.dot_general(a, b, (((1,), (1,)), ((), ())), preferred_element_type=F32)


def _rms(x, gain):
    return x * lax.rsqrt(jnp.mean(x * x, axis=-1, keepdims=True) + NORM_EPS) * gain


def _in_proj_kernel(x_ref, g_ref, w_ref, o_ref, su_ref, h_ref):
    h_ref[...] = _rms(x_ref[...], g_ref[...]).astype(BF16)
    tn = S5_WIDTH
    for j in range(IN_PAD // tn):
        acc = _dot(h_ref[...], w_ref[:, j * tn:(j + 1) * tn])
        o_ref[:, j * tn:(j + 1) * tn] = acc.astype(o_ref.dtype)
        if j == _COL["su"] // tn:
            for q in range(tn // 128):
                su_ref[q] = acc[:, q * 128:(q + 1) * 128]


def _resident(shape):
    return pl.BlockSpec(shape, lambda *_: (0,) * len(shape), pipeline_mode=pl.Buffered(1))


def _in_proj(x2d, gain, w):
    T = x2d.shape[0]
    tm = min(256, T)
    return pl.pallas_call(
        _in_proj_kernel,
        grid=(T // tm,),
        in_specs=[pl.BlockSpec((tm, D_MODEL), lambda i: (i, 0)),
                  _resident((1, D_MODEL)),
                  _resident((D_MODEL, IN_PAD))],
        out_specs=[pl.BlockSpec((tm, IN_PAD), lambda i: (i, 0)),
                   pl.BlockSpec((S5_WIDTH // 128, tm, 128), lambda i: (0, i, 0))],
        out_shape=[jax.ShapeDtypeStruct((T, IN_PAD), BF16),
                   jax.ShapeDtypeStruct((S5_WIDTH // 128, T, 128), F32)],
        scratch_shapes=[pltpu.VMEM((tm, D_MODEL), BF16)],
        compiler_params=_cparams("arbitrary"),
        name="in_proj",
    )(x2d, gain, w)


def _gla_kernel(q_ref, k_ref, v_ref, misc_ref, wup_ref, bup_ref, o_ref, st_ref):
    @pl.when(pl.program_id(1) == 0)
    def _():
        st_ref[...] = jnp.zeros_like(st_ref)

    C = GLA_CHUNK
    G = q_ref.shape[0]
    row = lax.broadcasted_iota(jnp.int32, (C, C), 0)
    col = lax.broadcasted_iota(jnp.int32, (C, C), 1)
    tril = row >= col
    tri = tril.astype(F32)
    lane = lax.broadcasted_iota(jnp.int32, (1, 2 * GLA_DK), 1)
    low = lane < GLA_DK
    log_a = []
    for g in range(G):
        x = _dot(misc_ref[g], wup_ref[...]) + bup_ref[...]
        log_a.append((jnp.minimum(x, 0.0) - jnp.log(1.0 + jnp.exp(-jnp.abs(x)))) * (1.0 / GLA_TAU))
    for c in range(q_ref.shape[1] // C):
        rs = slice(c * C, (c + 1) * C)
        for g in range(G):
            b = jnp.dot(tri, log_a[g][rs], preferred_element_type=F32, precision=lax.Precision.HIGHEST)
            b_last = b[C - 1:C]
            mid = b[C // 2:C // 2 + 1]
            qf = q_ref[g, rs, :].astype(F32) * (GLA_DK ** -0.5)
            kf = k_ref[g, rs, :].astype(F32)
            qe = qf * jnp.exp(b - mid)
            ke = kf * jnp.exp(mid - b)
            kd = kf * jnp.exp(b_last - b)
            qi = qf * jnp.exp(b)
            dec = jnp.exp(b_last)
            for j in range(GLA_HEADS // 2):
                ls = slice(j * 128, (j + 1) * 128)
                ke_p = ke[:, ls].astype(BF16)
                kd_p = kd[:, ls].astype(BF16)
                for hh in range(2):
                    h = 2 * j + hh
                    sel = low if hh == 0 else jnp.logical_not(low)
                    qe_h = jnp.where(sel, qe[:, ls], 0.0).astype(BF16)
                    qi_h = jnp.where(sel, qi[:, ls], 0.0).astype(BF16)
                    attn = jnp.where(tril, _dot_nt(qe_h, ke_p), 0.0)
                    v_h = v_ref[g, rs, h * GLA_DV:(h + 1) * GLA_DV]
                    st = st_ref[g * GLA_HEADS + h]
                    o = _dot(attn.astype(BF16), v_h) + _dot_nt(qi_h, st.astype(BF16))
                    v_t = v_h.astype(F32).T.astype(BF16)
                    st_ref[g * GLA_HEADS + h] = st * dec[:, ls] + _dot(v_t, kd_p)
                    o_ref[g, rs, h * GLA_DV:(h + 1) * GLA_DV] = o.astype(o_ref.dtype)


def _gla(p, wup_pad, bup, B, S):
    tc = min(256, S)
    nc = S // tc
    G = 1
    T = B * S
    pg = p.reshape(G, T // G, IN_PAD)
    row = lambda b, c: b * nc + c
    out = pl.pallas_call(
        _gla_kernel,
        grid=(B // G, nc),
        in_specs=[pl.BlockSpec((G, tc, 256), lambda b, c: (0, row(b, c), _COL["gq"] // 256)),
                  pl.BlockSpec((G, tc, 256), lambda b, c: (0, row(b, c), _COL["gk"] // 256)),
                  pl.BlockSpec((G, tc, 512), lambda b, c: (0, row(b, c), _COL["gv"] // 512)),
                  pl.BlockSpec((G, tc, 128), lambda b, c: (0, row(b, c), MISC_COL // 128)),
                  _resident((128, 256)),
                  _resident((1, 256))],
        out_specs=pl.BlockSpec((G, tc, 512), lambda b, c: (0, row(b, c), 0)),
        out_shape=jax.ShapeDtypeStruct((G, T // G, GLA_HEADS * GLA_DV), BF16),
        scratch_shapes=[pltpu.VMEM((G * GLA_HEADS, GLA_DV, 2 * GLA_DK), F32)],
        compiler_params=_cparams("arbitrary", "arbitrary"),
        name="gla",
    )(pg, pg, pg, pg, wup_pad, bup)
    return out.reshape(T, GLA_HEADS * GLA_DV)


def _s5_mats(a_re, a_im, log_step, b_re, b_im, c_re, c_im, d_skip):
    L, P, CH = S5_BLOCK, S5_STATE, S5_CH
    step = jnp.exp(log_step.astype(F32))[:, None]
    are, aim = a_re.astype(F32), a_im.astype(F32)
    j = jnp.arange(L + 1, dtype=F32)[None, :, None]
    mag = jnp.exp(are[:, None, :] * step[:, None, :] * j)
    ang = aim[:, None, :] * step[:, None, :] * j
    pw_re, pw_im = mag * jnp.cos(ang), mag * jnp.sin(ang)
    lb_re, lb_im = pw_re[:, 1], pw_im[:, 1]
    den = are * are + aim * aim
    f_re = ((lb_re - 1.0) * are + lb_im * aim) / den
    f_im = (lb_im * are - (lb_re - 1.0) * aim) / den
    bb_re = f_re[..., None] * b_re - f_im[..., None] * b_im
    bb_im = f_re[..., None] * b_im + f_im[..., None] * b_re
    lb_b_re = pw_re[..., None] * bb_re[:, None] - pw_im[..., None] * bb_im[:, None]
    lb_b_im = pw_re[..., None] * bb_im[:, None] + pw_im[..., None] * bb_re[:, None]
    kj = jnp.einsum("gcp,gjpd->gjcd", c_re, lb_b_re) - jnp.einsum("gcp,gjpd->gjcd", c_im, lb_b_im)
    t = np.arange(L)
    lag = t[None, :] - t[:, None]
    m = kj[:, np.clip(lag, 0, L)]
    m = jnp.where((lag >= 0)[None, :, :, None, None], m, 0.0)
    m = jnp.transpose(m, (0, 1, 4, 2, 3))
    eye = (lag == 0)[None, :, None, :, None] * jnp.eye(CH, dtype=F32)[None, None, :, None, :]
    m = m + eye * d_skip.reshape(S5_GROUPS, 1, 1, 1, CH)
    m = m.reshape(S5_GROUPS, L * CH, L * CH)
    p_re = jnp.transpose(lb_b_re[:, L - 1 - t], (0, 1, 3, 2)).reshape(S5_GROUPS, L * CH, P)
    p_im = jnp.transpose(lb_b_im[:, L - 1 - t], (0, 1, 3, 2)).reshape(S5_GROUPS, L * CH, P)
    pm = jnp.concatenate([p_re, p_im, p_im, p_re], axis=-1)
    cl_re = c_re[:, None] * pw_re[:, 1:, None, :] - c_im[:, None] * pw_im[:, 1:, None, :]
    cl_im = c_re[:, None] * pw_im[:, 1:, None, :] + c_im[:, None] * pw_re[:, 1:, None, :]
    q_re = jnp.transpose(cl_re, (0, 3, 1, 2)).reshape(S5_GROUPS, P, L * CH)
    q_im = -jnp.transpose(cl_im, (0, 3, 1, 2)).reshape(S5_GROUPS, P, L * CH)
    qm = jnp.concatenate([q_re, q_im], axis=1)
    al_re, al_im = pw_re[:, L], pw_im[:, L]
    ar = jnp.concatenate([al_re, al_re], axis=-1).reshape(1, S5_GROUPS * 2 * P)
    ai_a = jnp.concatenate([-al_im, al_im], axis=-1).reshape(1, S5_GROUPS * 2 * P)
    ai_b = jnp.concatenate([al_im, -al_im], axis=-1).reshape(1, S5_GROUPS * 2 * P)
    return m.astype(BF16), pm.astype(BF16), qm.astype(BF16), ar, ai_a, ai_b


def _s5_perm():
    src = np.arange(1024).reshape(8, 8, S5_CH).transpose(1, 0, 2).reshape(-1)
    m = np.zeros((1024, 1024), np.float32)
    m[np.arange(1024), src] = 1.0
    return jnp.asarray(m, BF16)


def _s5_kernel(su_ref, perm_ref, m_ref, p_ref, q_ref, ar_ref, aia_ref, aib_ref, y_ref,
               xcat_ref, u_ref, va_ref, vb_ref, x_ref, ycat_ref):
    L = S5_BLOCK
    NB = su_ref.shape[1] // L
    GH = 8
    W = 2 * S5_STATE
    for s in range(L):
        xcat_ref[:, s * 128:(s + 1) * 128] = su_ref[0, pl.ds(s, NB, stride=L), :].astype(BF16)
    for half in range(2):
        u_ref[half] = _dot(xcat_ref[:, half * 1024:(half + 1) * 1024], perm_ref[...]).astype(BF16)

    def u_of(g):
        return jnp.concatenate([u_ref[0, :, g * 128:(g + 1) * 128], u_ref[1, :, g * 128:(g + 1) * 128]], axis=1)

    for g in range(GH):
        v2 = _dot(u_of(g), p_ref[g])
        va_ref[:, g * W:(g + 1) * W] = v2[:, :W]
        vb_ref[:, g * W:(g + 1) * W] = v2[:, W:]
    ar, aia, aib = ar_ref[...], aia_ref[...], aib_ref[...]

    def step(n, carry):
        xa, xb = carry
        x_ref[pl.ds(n, 1), :] = xa
        na = xa * ar + xb * aia + va_ref[pl.ds(n, 1), :]
        nb = xb * ar + xa * aib + vb_ref[pl.ds(n, 1), :]
        return na, nb

    z = jnp.zeros((1, GH * W), F32)
    lax.fori_loop(0, NB, step, (z, z), unroll=8)
    for g in range(GH):
        y = _dot(u_of(g), m_ref[g]) + _dot(x_ref[:, g * W:(g + 1) * W].astype(BF16), q_ref[g])
        for half in range(2):
            ycat_ref[half, :, g * 128:(g + 1) * 128] = y[:, half * 128:(half + 1) * 128].astype(BF16)
    for half in range(2):
        out = _dot(ycat_ref[half], perm_ref[...])
        for t in range(8):
            y_ref[0, pl.ds(half * 8 + t, NB, stride=L), :] = out[:, t * 128:(t + 1) * 128]


def _s5(su, mats, B, S):
    m, pm, qm, ar, aia, aib = mats
    L, G, CH = S5_BLOCK, S5_GROUPS, S5_CH
    NB = S // L
    GH = 8
    W = 2 * S5_STATE
    slab = pl.BlockSpec((1, S, 128), lambda b, h: (h, b, 0))
    coef = pl.BlockSpec((1, GH * W), lambda b, h: (0, h))
    return pl.pallas_call(
        _s5_kernel,
        grid=(B, G // GH),
        in_specs=[slab,
                  pl.BlockSpec((1024, 1024), lambda b, h: (0, 0)),
                  pl.BlockSpec((GH, L * CH, L * CH), lambda b, h: (h, 0, 0)),
                  pl.BlockSpec((GH, L * CH, 2 * W), lambda b, h: (h, 0, 0)),
                  pl.BlockSpec((GH, W, L * CH), lambda b, h: (h, 0, 0)),
                  coef, coef, coef],
        out_specs=slab,
        out_shape=jax.ShapeDtypeStruct((G // GH, B * S, 128), F32),
        scratch_shapes=[pltpu.VMEM((NB, L * 128), BF16), pltpu.VMEM((2, NB, 1024), BF16),
                        pltpu.VMEM((NB, GH * W), F32), pltpu.VMEM((NB, GH * W), F32),
                        pltpu.VMEM((NB, GH * W), F32), pltpu.VMEM((2, NB, 1024), BF16)],
        compiler_params=_cparams("arbitrary", "arbitrary"),
        name="s5_scan",
    )(su, _s5_perm(), m, pm, qm, ar, aia, aib)


def _rope_kernel(pos_ref, f128_ref, f64_ref, c128_ref, s128_ref, c64_ref, s64_ref):
    pos = pos_ref[...]
    a = pos * f128_ref[...]
    c128_ref[...] = jnp.cos(a)
    s128_ref[...] = jnp.sin(a)
    a = pos * f64_ref[...]
    c64_ref[...] = jnp.cos(a)
    s64_ref[...] = jnp.sin(a)


def _rope_freq(d, width):
    rd = d // ROPE_FRAC
    half = rd // 2
    inv = ROPE_THETA ** (-jnp.arange(half, dtype=F32) * 2.0 / rd)
    per_head = jnp.concatenate([inv, inv, jnp.zeros((d - rd,), F32)])
    return jnp.tile(per_head, width // d).reshape(1, width)


def _rope_tables(positions):
    T = positions.size
    pos = positions.astype(F32).reshape(T, 1)
    tq = min(512, T)
    out = jax.ShapeDtypeStruct((T, 128), F32)
    return pl.pallas_call(
        _rope_kernel,
        grid=(T // tq,),
        in_specs=[pl.BlockSpec((tq, 1), lambda i: (i, 0)),
                  pl.BlockSpec((1, 128), lambda i: (0, 0)),
                  pl.BlockSpec((1, 128), lambda i: (0, 0))],
        out_specs=[pl.BlockSpec((tq, 128), lambda i: (i, 0))] * 4,
        out_shape=[out] * 4,
        compiler_params=_cparams("arbitrary"),
        name="rope_tables",
    )(pos, _rope_freq(DSA_HD, 128), _rope_freq(DSA_IDX_DIM, 128))


def _rope(x, cos, sin, head_dim):
    width = x.shape[-1]
    half = head_dim // ROPE_FRAC // 2
    lane = lax.broadcasted_iota(jnp.int32, (1, width), 1)
    per = lane % head_dim
    sgn_lo = jnp.where(per < half, -1.0, 0.0)
    sgn_hi = jnp.where((per >= half) & (per < 2 * half), 1.0, 0.0)
    up = pltpu.roll(x, width - half, axis=1)
    dn = pltpu.roll(x, half, axis=1)
    return x * cos + sin * (up * sgn_lo + dn * sgn_hi)


def _dsa_prep_kernel(dq_ref, dk_ref, diq_ref, misc_ref, c128_ref, s128_ref, c64_ref, s64_ref,
                     qg_ref, kg_ref, qt_out, k_out, iqt_out, ik_out, wt_out):
    c128, s128 = c128_ref[...], s128_ref[...]
    c64, s64 = c64_ref[...], s64_ref[...]
    for h in range(DSA_HEADS):
        sl = slice(h * DSA_HD, (h + 1) * DSA_HD)
        q = _rope(_rms(dq_ref[:, sl].astype(F32), qg_ref[...]), c128, s128, DSA_HD)
        qt_out[h] = (q * (DSA_HD ** -0.5)).T.astype(qt_out.dtype)
    k = _rope(_rms(dk_ref[...].astype(F32), kg_ref[...]), c128, s128, DSA_HD)
    k_out[...] = k.astype(k_out.dtype)
    for j in range(DSA_IDX_HEADS // 2):
        sl = slice(j * 128, (j + 1) * 128)
        xt = _rope(diq_ref[:, sl].astype(F32), c64, s64, DSA_IDX_DIM).T
        iqt_out[2 * j] = xt[:DSA_IDX_DIM].astype(iqt_out.dtype)
        iqt_out[2 * j + 1] = xt[DSA_IDX_DIM:].astype(iqt_out.dtype)
    misc = misc_ref[...].astype(F32)
    ik_out[...] = _rope(misc, c64, s64, DSA_IDX_DIM)[:, :DSA_IDX_DIM].astype(ik_out.dtype)
    wt_out[...] = misc.T[DIW_LANE:DIW_LANE + DSA_IDX_HEADS]


def _dsa_prep(p, tables, q_gain, k_gain):
    T = p.shape[0]
    tq = min(512, T)
    tab = pl.BlockSpec((tq, 128), lambda i: (i, 0))
    gain = pl.BlockSpec((1, 128), lambda i: (0, 0))
    return pl.pallas_call(
        _dsa_prep_kernel,
        grid=(T // tq,),
        in_specs=[pl.BlockSpec((tq, 512), lambda i: (i, _COL["dq"] // 512)),
                  pl.BlockSpec((tq, 128), lambda i: (i, _COL["dk"] // 128)),
                  pl.BlockSpec((tq, 1024), lambda i: (i, _COL["diq"] // 1024)),
                  pl.BlockSpec((tq, 128), lambda i: (i, MISC_COL // 128)),
                  tab, tab, tab, tab, gain, gain],
        out_specs=[pl.BlockSpec((DSA_HEADS, DSA_HD, tq), lambda i: (0, 0, i)),
                   pl.BlockSpec((tq, DSA_HD), lambda i: (i, 0)),
                   pl.BlockSpec((DSA_IDX_HEADS, DSA_IDX_DIM, tq), lambda i: (0, 0, i)),
                   pl.BlockSpec((tq, DSA_IDX_DIM), lambda i: (i, 0)),
                   pl.BlockSpec((DSA_IDX_HEADS, tq), lambda i: (0, i))],
        out_shape=[jax.ShapeDtypeStruct((DSA_HEADS, DSA_HD, T), BF16),
                   jax.ShapeDtypeStruct((T, DSA_HD), BF16),
                   jax.ShapeDtypeStruct((DSA_IDX_HEADS, DSA_IDX_DIM, T), BF16),
                   jax.ShapeDtypeStruct((T, DSA_IDX_DIM), BF16),
                   jax.ShapeDtypeStruct((DSA_IDX_HEADS, T), F32)],
        compiler_params=_cparams("arbitrary"),
        name="dsa_prep",
    )(p, p, p, p, *tables, q_gain, k_gain)


def _dsa_kernel(iqt_ref, wt_ref, qt_ref, ik_ref, k_ref, vt_ref, o_ref,
                key_ref, half_ref, bias_ref, s_ref, p_ref, acc_ref, *, ksel):
    TQ, CK = DSA_QBLOCK, DSA_KCHUNK
    HK = CK // 2
    i = pl.program_id(1)
    nk = (i * TQ + TQ + CK - 1) // CK
    t_idx = i * TQ + lax.broadcasted_iota(jnp.int32, (1, TQ), 1)
    w_all = wt_ref[...] * (DSA_IDX_HEADS ** -0.5 * DSA_IDX_DIM ** -0.5)

    def score_chunk(kc, carry):
        for half in range(2):
            lhs = ik_ref[0, kc, half * HK:(half + 1) * HK, :]
            acc = None
            for h in range(DSA_IDX_HEADS):
                term = w_all[h:h + 1] * jnp.maximum(_dot(lhs, iqt_ref[h]), 0.0)
                acc = term if acc is None else acc + term
            s_idx = kc * CK + half * HK + lax.broadcasted_iota(jnp.int32, (HK, 1), 0)
            bits = lax.bitcast_convert_type(acc, jnp.int32)
            key = bits ^ ((bits >> 31) & 0x7FFFFFFF)
            key = jnp.where(s_idx <= t_idx, key, INT_MIN)
            key_ref[kc, half * HK:(half + 1) * HK, :] = key
            half_ref[kc, half * HK:(half + 1) * HK, :] = (key >> 16).astype(jnp.int16)
        return carry

    lax.fori_loop(0, nk, score_chunk, 0)
    npair = (nk + 1) // 2

    @pl.when(nk % 2 == 1)
    def _():
        key_ref[nk] = jnp.full((CK, TQ), INT_MIN, jnp.int32)
        half_ref[nk] = jnp.full((CK, TQ), I16_MIN, jnp.int16)

    chains = 4
    groups = 2 * CK // (16 * chains)

    def count_ge(cand):
        cb = jnp.broadcast_to(cand.astype(jnp.int16), (16, TQ))[None, None]

        def count(j, acc):
            keys = half_ref[pl.ds(2 * j, 2)].reshape(groups, chains, 16, TQ)
            for g in range(groups):
                acc = acc + jnp.where(keys[g:g + 1] >= cb, jnp.int16(1), jnp.int16(0))[0]
            return acc

        acc = lax.fori_loop(0, npair, count, jnp.zeros((chains, 16, TQ), jnp.int16))
        return jnp.sum(jnp.sum(acc.astype(jnp.int32), axis=0), axis=0, keepdims=True)

    def select16(need):
        def bit_step(bi, thr):
            cand = thr + jnp.left_shift(jnp.int32(1), 15 - bi)
            return jnp.where(count_ge(cand) >= need, cand, thr)

        return lax.fori_loop(0, 16, bit_step, jnp.full((1, TQ), I16_MIN, jnp.int32))

    thr_hi = select16(ksel)
    above = jnp.where(thr_hi >= -I16_MIN - 1, 0, count_ge(jnp.minimum(thr_hi + 1, -I16_MIN - 1)))

    def low_half(j, carry):
        keys = key_ref[pl.ds(2 * j, 2)]
        low = (keys & 0xFFFF) + I16_MIN
        half_ref[pl.ds(2 * j, 2)] = jnp.where((keys >> 16) == thr_hi, low, I16_MIN).astype(jnp.int16)
        return carry

    lax.fori_loop(0, npair, low_half, 0)
    thr_lo = select16(ksel - above)
    thr = thr_hi * 65536 + (thr_lo - I16_MIN)
    thr = jnp.maximum(thr, INT_MIN + 1)

    kg = CK // 16
    unit = 64

    def logits(kc, buf, m_run):
        bias_ref[...] = jnp.where(key_ref[kc] >= thr, 0.0, -1e30)
        kk = k_ref[0, kc]
        new = []
        for h in range(DSA_HEADS):
            s = _dot(kk, qt_ref[h]) + bias_ref[...]
            s_ref[buf, h] = s
            mx = jnp.max(s.reshape(kg, 2, 8, TQ), axis=0)
            new.append(jnp.maximum(m_run[h], jnp.max(jnp.max(mx, axis=0), axis=0, keepdims=True)))
        return tuple(new)

    def attn_chunk(kc, carry):
        m_old, m_run, ls = carry
        cur = kc % 2
        vt = vt_ref[0, kc]
        new_l = []
        for h in range(DSA_HEADS):
            alpha = jnp.exp(m_old[h] - m_run[h])
            mb = jnp.broadcast_to(m_run[h], (8, TQ))[None]
            part = ls[h] * alpha
            for u in range(CK // unit):
                rows = slice(u * unit, (u + 1) * unit)
                pr = jnp.exp(s_ref[cur, h, rows, :].reshape(unit // 8, 8, TQ) - mb)
                part = part + jnp.sum(pr.reshape(unit // 16, 2, 8, TQ), axis=0)
                p_ref[h, rows, :] = pr.reshape(unit, TQ).astype(BF16)
            new_l.append(part)
            acc_ref[h] = acc_ref[h] * alpha + _dot(vt, p_ref[h])
        m_next = logits(jnp.minimum(kc + 1, nk - 1), 1 - cur, m_run)
        return m_run, m_next, tuple(new_l)

    acc_ref[...] = jnp.zeros_like(acc_ref)
    floor = tuple(jnp.full((1, TQ), -1e30, F32) for _ in range(DSA_HEADS))
    init = (floor, logits(0, 0, floor), tuple(jnp.zeros((2, 8, TQ), F32) for _ in range(DSA_HEADS)))
    _, _, ls = lax.fori_loop(0, nk, attn_chunk, init)
    for h in range(DSA_HEADS):
        denom = jnp.sum(jnp.sum(ls[h], axis=0), axis=0, keepdims=True)
        o_ref[:, h * DSA_HD:(h + 1) * DSA_HD] = (acc_ref[h] / denom).T.astype(o_ref.dtype)


def _dsa(p, qt, k_rot, iqt, ik_rot, wt, B, S):
    TQ, CK = DSA_QBLOCK, DSA_KCHUNK
    NC = S // CK
    nq = S // TQ
    assert NC % 2 == 0 and TQ % CK == 0
    ksel = min(DSA_TOPK, S // 4)
    ik = ik_rot.reshape(B, NC, CK, DSA_IDX_DIM)
    k = k_rot.reshape(B, NC, CK, DSA_HD)
    vt = p[:, _COL["dv"]:_COL["dv"] + DSA_HD].reshape(B, NC, CK, DSA_HD).transpose(0, 1, 3, 2)
    return pl.pallas_call(
        functools.partial(_dsa_kernel, ksel=ksel),
        grid=(B, nq),
        in_specs=[pl.BlockSpec((DSA_IDX_HEADS, DSA_IDX_DIM, TQ), lambda b, i: (0, 0, b * nq + i)),
                  pl.BlockSpec((DSA_IDX_HEADS, TQ), lambda b, i: (0, b * nq + i)),
                  pl.BlockSpec((DSA_HEADS, DSA_HD, TQ), lambda b, i: (0, 0, b * nq + i)),
                  pl.BlockSpec((1, NC, CK, DSA_IDX_DIM), lambda b, i: (b, 0, 0, 0)),
                  pl.BlockSpec((1, NC, CK, DSA_HD), lambda b, i: (b, 0, 0, 0)),
                  pl.BlockSpec((1, NC, DSA_HD, CK), lambda b, i: (b, 0, 0, 0))],
        out_specs=pl.BlockSpec((TQ, DSA_HEADS * DSA_HD), lambda b, i: (b * nq + i, 0)),
        out_shape=jax.ShapeDtypeStruct((B * S, DSA_HEADS * DSA_HD), BF16),
        scratch_shapes=[pltpu.VMEM((NC, CK, TQ), jnp.int32),
                        pltpu.VMEM((NC, CK, TQ), jnp.int16),
                        pltpu.VMEM((CK, TQ), F32),
                        pltpu.VMEM((2, DSA_HEADS, CK, TQ), F32),
                        pltpu.VMEM((DSA_HEADS, CK, TQ), BF16),
                        pltpu.VMEM((DSA_HEADS, DSA_HD, TQ), F32)],
        compiler_params=_cparams("arbitrary", "arbitrary"),
        name="dsa_attn",
    )(iqt, wt, qt, ik, k, vt)


def _mem_kv_kernel(mem_ref, g_ref, w_ref, kg_ref, k_out, v_out):
    h = _rms(mem_ref[...], g_ref[...]).astype(BF16)
    kv = _dot(h, w_ref[...])
    width = MEM_HEADS * MEM_HD
    for hd in range(MEM_HEADS):
        sl = slice(hd * MEM_HD, (hd + 1) * MEM_HD)
        k_out[:, sl] = _rms(kv[:, sl], kg_ref[...]).astype(k_out.dtype)
    v_out[...] = kv[:, width:].astype(v_out.dtype)


def _mem_kv(mem2d, gain, w_kv, k_gain, B):
    width = MEM_HEADS * MEM_HD
    out = jax.ShapeDtypeStruct((B * MEM_LEN, width), BF16)
    return pl.pallas_call(
        _mem_kv_kernel,
        grid=(B,),
        in_specs=[pl.BlockSpec((MEM_LEN, D_MODEL), lambda b: (b, 0)),
                  pl.BlockSpec((1, D_MODEL), lambda b: (0, 0)),
                  pl.BlockSpec((D_MODEL, 2 * width), lambda b: (0, 0)),
                  pl.BlockSpec((1, MEM_HD), lambda b: (0, 0))],
        out_specs=[pl.BlockSpec((MEM_LEN, width), lambda b: (b, 0))] * 2,
        out_shape=[out, out],
        compiler_params=_cparams("arbitrary"),
        name="mem_kv",
    )(mem2d, gain, w_kv, k_gain)


def _mem_attn_kernel(q_ref, k_ref, v_ref, qg_ref, o_ref):
    for h in range(MEM_HEADS):
        sl = slice(h * MEM_HD, (h + 1) * MEM_HD)
        q = _rms(q_ref[:, sl].astype(F32), qg_ref[...]) * (MEM_HD ** -0.5)
        s = _dot_nt(q.astype(BF16), k_ref[:, sl])
        pr = jnp.exp(s - jnp.max(s, axis=1, keepdims=True))
        o = _dot(pr.astype(BF16), v_ref[:, sl]) / jnp.sum(pr, axis=1, keepdims=True)
        o_ref[:, sl] = o.astype(o_ref.dtype)


def _mem_attn(p, mk, mv, q_gain, B, S):
    tq = min(512, S)
    nq = S // tq
    width = MEM_HEADS * MEM_HD
    return pl.pallas_call(
        _mem_attn_kernel,
        grid=(B, nq),
        in_specs=[pl.BlockSpec((tq, width), lambda b, i: (b * nq + i, _COL["mq"] // width)),
                  pl.BlockSpec((MEM_LEN, width), lambda b, i: (b, 0)),
                  pl.BlockSpec((MEM_LEN, width), lambda b, i: (b, 0)),
                  pl.BlockSpec((1, MEM_HD), lambda b, i: (0, 0))],
        out_specs=pl.BlockSpec((tq, width), lambda b, i: (b * nq + i, 0)),
        out_shape=jax.ShapeDtypeStruct((B * S, width), BF16),
        compiler_params=_cparams("arbitrary", "arbitrary"),
        name="mem_attn",
    )(p, mk, mv, q_gain)


def _out_proj_kernel(og_ref, ys_ref, od_ref, om_ref, z_ref, x_ref, gn_ref, wglu_ref, w_ref, o_ref, lhs_ref):
    y = jax.nn.gelu(jnp.concatenate([ys_ref[q] for q in range(ys_ref.shape[0])], axis=1))
    o_s5 = y * jax.nn.sigmoid(_dot(y.astype(BF16), wglu_ref[...]))
    branches = (og_ref[...].astype(F32), o_s5, od_ref[...].astype(F32), om_ref[...].astype(F32))
    for bi, o in enumerate(branches):
        for h in range(o.shape[1] // HEAD):
            c0 = bi * 512 + h * HEAD
            z = z_ref[:, c0:c0 + HEAD].astype(F32)
            n = _rms(o[:, h * HEAD:(h + 1) * HEAD], gn_ref[:, c0:c0 + HEAD])
            lhs_ref[:, c0:c0 + HEAD] = (n * (z * jax.nn.sigmoid(z))).astype(BF16)
    o_ref[...] = x_ref[...] + _dot(lhs_ref[...], w_ref[...])


def _out_proj(o_gla, y_s5, o_dsa, o_mem, p, x2d, out_gain, w_glu, w_out):
    T = x2d.shape[0]
    tm = min(512, T)
    branch = pl.BlockSpec((tm, 512), lambda i: (i, 0))
    return pl.pallas_call(
        _out_proj_kernel,
        grid=(T // tm,),
        in_specs=[branch, pl.BlockSpec((S5_WIDTH // 128, tm, 128), lambda i: (0, i, 0)), branch, branch,
                  pl.BlockSpec((tm, MIX_WIDTH), lambda i: (i, 0)),
                  pl.BlockSpec((tm, D_MODEL), lambda i: (i, 0)),
                  _resident((1, MIX_WIDTH)),
                  _resident((S5_WIDTH, S5_WIDTH)),
                  _resident((MIX_WIDTH, D_MODEL))],
        out_specs=pl.BlockSpec((tm, D_MODEL), lambda i: (i, 0)),
        out_shape=jax.ShapeDtypeStruct((T, D_MODEL), F32),
        scratch_shapes=[pltpu.VMEM((tm, MIX_WIDTH), BF16)],
        compiler_params=_cparams("arbitrary"),
        name="out_proj",
    )(o_gla, y_s5, o_dsa, o_mem, p, x2d, out_gain, w_glu, w_out)


def _permute_w_in(w_in):
    pieces = [w_in[..., _ORIG[n][0]:_ORIG[n][1]] for n in _ORDER]
    used = sum(_ORIG[n][1] - _ORIG[n][0] for n in _ORDER)
    pad = jnp.zeros(w_in.shape[:-1] + (IN_PAD - used,), w_in.dtype)
    return jnp.concatenate(pieces + [pad], axis=-1).astype(BF16)


def kernel(x, mem, positions, ln_gain, w_in, gla_w_up, gla_b_up, s5_a_re, s5_a_im, s5_log_step, s5_b_re, s5_b_im, s5_c_re, s5_c_im, s5_d, s5_w_glu, dsa_q_norm, dsa_k_norm, mem_norm, w_mem_kv, mem_q_norm, mem_k_norm, out_norm, w_out):
    B, S, _ = x.shape
    depth = w_in.shape[0]
    T = B * S
    x2d = x.reshape(T, D_MODEL)
    mem2d = mem.reshape(B * mem.shape[1], D_MODEL)
    w_in_p = _permute_w_in(w_in)
    wup_pad = jnp.zeros((depth, 128, GLA_HEADS * GLA_DK), F32)
    wup_pad = wup_pad.at[:, GLR_LANE:GLR_LANE + GLA_GATE_RANK].set(gla_w_up).astype(BF16)
    w_kv = w_mem_kv.astype(BF16)
    w_glu = s5_w_glu.astype(BF16)
    w_o = w_out.astype(BF16)
    tables = _rope_tables(positions)
    for l in range(depth):
        p, su = _in_proj(x2d, ln_gain[l][None], w_in_p[l])
        o_gla = _gla(p, wup_pad[l], gla_b_up[l][None], B, S)
        mats = _s5_mats(s5_a_re[l], s5_a_im[l], s5_log_step[l], s5_b_re[l], s5_b_im[l],
                        s5_c_re[l], s5_c_im[l], s5_d[l])
        y_s5 = _s5(su, mats, B, S)
        qt, k_rot, iqt, ik_rot, wt = _dsa_prep(p, tables, dsa_q_norm[l][None], dsa_k_norm[l][None])
        o_dsa = _dsa(p, qt, k_rot, iqt, ik_rot, wt, B, S)
        mk, mv = _mem_kv(mem2d, mem_norm[l][None], w_kv[l], mem_k_norm[l][None], B)
        o_mem = _mem_attn(p, mk, mv, mem_q_norm[l][None], B, S)
        x2d = _out_proj(o_gla, y_s5, o_dsa, o_mem, p, x2d, out_norm[l][None], w_glu[l], w_o[l])
    return x2d.reshape(B, S, D_MODEL)
```

```python
import functools

import numpy as np
import jax
import jax.numpy as jnp
from jax import lax
from jax.experimental import pallas as pl
from jax.experimental.pallas import tpu as pltpu

F32 = jnp.float32
BF16 = jnp.bfloat16

D_MODEL = 2048
MEM_LEN = 256
ROPE_THETA = 500000.0
ROPE_FRAC = 4
NORM_EPS = 1e-6

GLA_HEADS = 4
GLA_DK = 64
GLA_DV = 128
GLA_GATE_RANK = 16
GLA_TAU = 16.0
GLA_CHUNK = 64

S5_GROUPS = 32
S5_CH = 16
S5_STATE = 64
S5_WIDTH = S5_GROUPS * S5_CH
S5_BLOCK = 16

DSA_HEADS = 4
DSA_HD = 128
DSA_IDX_HEADS = 16
DSA_IDX_DIM = 64
DSA_TOPK = 256
DSA_QBLOCK = 256
DSA_KCHUNK = 256

MEM_HEADS = 4
MEM_HD = 128
HEAD = 128
MIX_WIDTH = 2048

VMEM_LIMIT_BYTES = 52 * 1024 * 1024

_ORIG = dict(gq=(0, 256), gk=(256, 512), gv=(512, 1024), glr=(1024, 1040), gz=(1040, 1552),
             su=(1552, 2064), sz=(2064, 2576), dq=(2576, 3088), dk=(3088, 3216), dv=(3216, 3344),
             diq=(3344, 4368), dik=(4368, 4432), diw=(4432, 4448), dz=(4448, 4960),
             mq=(4960, 5472), mz=(5472, 5984))
_ORDER = ("gz", "sz", "dz", "mz", "diq", "gv", "su", "dq", "mq", "gq", "gk", "dk", "dv",
          "dik", "glr", "diw")
_COL = {}
_c = 0
for _n in _ORDER:
    _COL[_n] = _c
    _c += _ORIG[_n][1] - _ORIG[_n][0]
IN_PAD = 6144
MISC_COL = _COL["dik"]
GLR_LANE = 64
DIW_LANE = 80
INT_MIN = -(2 ** 31)
I16_MIN = -(2 ** 15)


def _cparams(*sem):
    return pltpu.CompilerParams(dimension_semantics=sem, vmem_limit_bytes=VMEM_LIMIT_BYTES)


def _dot(a, b):
    return jnp.dot(a, b, preferred_element_type=F32)


def _dot_nt(a, b):
    return lax.dot_general(a, b, (((1,), (1,)), ((), ())), preferred_element_type=F32)


def _rms(x, gain):
    return x * lax.rsqrt(jnp.mean(x * x, axis=-1, keepdims=True) + NORM_EPS) * gain


def _in_proj_kernel(x_ref, g_ref, w_ref, o_ref, su_ref, h_ref):
    h_ref[...] = _rms(x_ref[...], g_ref[...]).astype(BF16)
    tn = S5_WIDTH
    for j in range(IN_PAD // tn):
        acc = _dot(h_ref[...], w_ref[:, j * tn:(j + 1) * tn])
        o_ref[:, j * tn:(j + 1) * tn] = acc.astype(o_ref.dtype)
        if j == _COL["su"] // tn:
            for q in range(tn // 128):
                su_ref[q] = acc[:, q * 128:(q + 1) * 128]


def _resident(shape):
    return pl.BlockSpec(shape, lambda *_: (0,) * len(shape), pipeline_mode=pl.Buffered(1))


def _layer(shape, l):
    return pl.BlockSpec((None,) + shape, lambda *_: (l,) + (0,) * len(shape), pipeline_mode=pl.Buffered(1))


def _in_proj(x2d, gain, w, l):
    T = x2d.shape[0]
    tm = min(256, T)
    return pl.pallas_call(
        _in_proj_kernel,
        grid=(T // tm,),
        in_specs=[pl.BlockSpec((tm, D_MODEL), lambda i: (i, 0)),
                  _resident((1, D_MODEL)),
                  _layer((D_MODEL, IN_PAD), l)],
        out_specs=[pl.BlockSpec((tm, IN_PAD), lambda i: (i, 0)),
                   pl.BlockSpec((S5_WIDTH // 128, tm, 128), lambda i: (0, i, 0))],
        out_shape=[jax.ShapeDtypeStruct((T, IN_PAD), BF16),
                   jax.ShapeDtypeStruct((S5_WIDTH // 128, T, 128), F32)],
        scratch_shapes=[pltpu.VMEM((tm, D_MODEL), BF16)],
        compiler_params=_cparams("arbitrary"),
        name="in_proj",
    )(x2d, gain, w)


def _gla_kernel(q_ref, k_ref, v_ref, misc_ref, wup_ref, bup_ref, o_ref, st_ref):
    @pl.when(pl.program_id(1) == 0)
    def _():
        st_ref[...] = jnp.zeros_like(st_ref)

    C = GLA_CHUNK
    tc = q_ref.shape[0]
    nchunk = tc // C
    x = _dot(misc_ref[...], wup_ref[...]) + bup_ref[...]
    log_a = (jnp.minimum(x, 0.0) - jnp.log(1.0 + jnp.exp(-jnp.abs(x)))) * (1.0 / GLA_TAU)
    row = lax.broadcasted_iota(jnp.int32, (tc, tc), 0)
    col = lax.broadcasted_iota(jnp.int32, (tc, tc), 1)
    tril = (row >= col) & ((row // C) == (col // C))
    la_hi = log_a.astype(BF16)
    r1 = log_a - la_hi.astype(F32)
    la_mid = r1.astype(BF16)
    la_lo = (r1 - la_mid.astype(F32)).astype(BF16)
    tri3 = jnp.concatenate([tril.astype(BF16)] * 3, axis=1)
    b = _dot(tri3, jnp.concatenate([la_hi, la_mid, la_lo], axis=0))
    lane = lax.broadcasted_iota(jnp.int32, (1, 2 * GLA_DK), 1)
    low = lane < GLA_DK
    srow = lax.broadcasted_iota(jnp.int32, (2 * GLA_DV, 1), 0)
    diag = (srow < GLA_DV) == low
    for j in range(GLA_HEADS // 2):
        ls = slice(j * 128, (j + 1) * 128)
        bp = b[:, ls]
        mid = jnp.concatenate([jnp.broadcast_to(bp[c * C + C // 2:c * C + C // 2 + 1], (C, 128))
                               for c in range(nchunk)], axis=0)
        last = jnp.concatenate([jnp.broadcast_to(bp[c * C + C - 1:c * C + C], (C, 128))
                                for c in range(nchunk)], axis=0)
        qf = q_ref[:, ls].astype(F32) * (GLA_DK ** -0.5)
        kf = k_ref[:, ls].astype(F32)
        qe = qf * jnp.exp(bp - mid)
        ke_p = (kf * jnp.exp(mid - bp)).astype(BF16)
        kd_p = (kf * jnp.exp(last - bp)).astype(BF16)
        qi_p = (qf * jnp.exp(bp)).astype(BF16)
        intra = []
        for hh in range(2):
            h = 2 * j + hh
            sel = low if hh == 0 else jnp.logical_not(low)
            qe_h = jnp.where(sel, qe, 0.0).astype(BF16)
            attn = jnp.where(tril, _dot_nt(qe_h, ke_p), 0.0)
            intra.append(_dot(attn.astype(BF16), v_ref[:, h * GLA_DV:(h + 1) * GLA_DV]))
        intra = jnp.concatenate(intra, axis=1)
        vs = slice(2 * j * GLA_DV, (2 * j + 2) * GLA_DV)
        st = st_ref[j]
        for c in range(nchunk):
            rs = slice(c * C, (c + 1) * C)
            o = intra[rs] + _dot_nt(qi_p[rs], st.astype(BF16))
            o_ref[rs, vs] = o.astype(o_ref.dtype)
            v_t = v_ref[rs, vs].astype(F32).T.astype(BF16)
            dec = jnp.exp(bp[c * C + C - 1:c * C + C])
            st = jnp.where(diag, st * dec + _dot(v_t, kd_p[rs]), 0.0)
        st_ref[j] = st


def _gla(p, wup_pad, bup, B, S):
    tc = min(256, S)
    nc = S // tc
    row = lambda b, c: b * nc + c
    return pl.pallas_call(
        _gla_kernel,
        grid=(B, nc),
        in_specs=[pl.BlockSpec((tc, 256), lambda b, c: (row(b, c), _COL["gq"] // 256)),
                  pl.BlockSpec((tc, 256), lambda b, c: (row(b, c), _COL["gk"] // 256)),
                  pl.BlockSpec((tc, 512), lambda b, c: (row(b, c), _COL["gv"] // 512)),
                  pl.BlockSpec((tc, 128), lambda b, c: (row(b, c), MISC_COL // 128)),
                  _resident((128, 256)),
                  _resident((1, 256))],
        out_specs=pl.BlockSpec((tc, 512), lambda b, c: (row(b, c), 0)),
        out_shape=jax.ShapeDtypeStruct((B * S, GLA_HEADS * GLA_DV), BF16),
        scratch_shapes=[pltpu.VMEM((GLA_HEADS // 2, 2 * GLA_DV, 2 * GLA_DK), F32)],
        compiler_params=_cparams("arbitrary", "arbitrary"),
        name="gla",
    )(p, p, p, p, wup_pad, bup)


def _s5_mats(a_re, a_im, log_step, b_re, b_im, c_re, c_im, d_skip):
    L, P, CH = S5_BLOCK, S5_STATE, S5_CH
    step = jnp.exp(log_step.astype(F32))[:, None]
    are, aim = a_re.astype(F32), a_im.astype(F32)
    j = jnp.arange(L + 1, dtype=F32)[None, :, None]
    mag = jnp.exp(are[:, None, :] * step[:, None, :] * j)
    ang = aim[:, None, :] * step[:, None, :] * j
    pw_re, pw_im = mag * jnp.cos(ang), mag * jnp.sin(ang)
    lb_re, lb_im = pw_re[:, 1], pw_im[:, 1]
    den = are * are + aim * aim
    f_re = ((lb_re - 1.0) * are + lb_im * aim) / den
    f_im = (lb_im * are - (lb_re - 1.0) * aim) / den
    bb_re = f_re[..., None] * b_re - f_im[..., None] * b_im
    bb_im = f_re[..., None] * b_im + f_im[..., None] * b_re
    lb_b_re = pw_re[..., None] * bb_re[:, None] - pw_im[..., None] * bb_im[:, None]
    lb_b_im = pw_re[..., None] * bb_im[:, None] + pw_im[..., None] * bb_re[:, None]
    kj = jnp.einsum("gcp,gjpd->gjcd", c_re, lb_b_re) - jnp.einsum("gcp,gjpd->gjcd", c_im, lb_b_im)
    t = np.arange(L)
    lag = t[None, :] - t[:, None]
    m = kj[:, np.clip(lag, 0, L)]
    m = jnp.where((lag >= 0)[None, :, :, None, None], m, 0.0)
    m = jnp.transpose(m, (0, 1, 4, 2, 3))
    eye = (lag == 0)[None, :, None, :, None] * jnp.eye(CH, dtype=F32)[None, None, :, None, :]
    m = m + eye * d_skip.reshape(S5_GROUPS, 1, 1, 1, CH)
    m = m.reshape(S5_GROUPS, L * CH, L * CH)
    p_re = jnp.transpose(lb_b_re[:, L - 1 - t], (0, 1, 3, 2)).reshape(S5_GROUPS, L * CH, P)
    p_im = jnp.transpose(lb_b_im[:, L - 1 - t], (0, 1, 3, 2)).reshape(S5_GROUPS, L * CH, P)
    pm = jnp.concatenate([p_re, p_im, p_im, p_re], axis=-1)
    cl_re = c_re[:, None] * pw_re[:, 1:, None, :] - c_im[:, None] * pw_im[:, 1:, None, :]
    cl_im = c_re[:, None] * pw_im[:, 1:, None, :] + c_im[:, None] * pw_re[:, 1:, None, :]
    q_re = jnp.transpose(cl_re, (0, 3, 1, 2)).reshape(S5_GROUPS, P, L * CH)
    q_im = -jnp.transpose(cl_im, (0, 3, 1, 2)).reshape(S5_GROUPS, P, L * CH)
    qm = jnp.concatenate([q_re, q_im], axis=1)
    al_re, al_im = pw_re[:, L], pw_im[:, L]
    ar = jnp.concatenate([al_re, al_re], axis=-1).reshape(1, S5_GROUPS * 2 * P)
    ai_a = jnp.concatenate([-al_im, al_im], axis=-1).reshape(1, S5_GROUPS * 2 * P)
    ai_b = jnp.concatenate([al_im, -al_im], axis=-1).reshape(1, S5_GROUPS * 2 * P)
    return m.astype(BF16), pm.astype(BF16), qm.astype(BF16), ar, ai_a, ai_b


def _s5_perm():
    src = np.arange(1024).reshape(8, 8, S5_CH).transpose(1, 0, 2).reshape(-1)
    m = np.zeros((1024, 1024), np.float32)
    m[np.arange(1024), src] = 1.0
    return jnp.asarray(m, BF16)


def _s5_kernel(su_ref, perm_ref, m_ref, p_ref, q_ref, ar_ref, aia_ref, aib_ref, y_ref,
               xcat_ref, u_ref, va_ref, vb_ref, x_ref, ycat_ref):
    L = S5_BLOCK
    NB = su_ref.shape[1] // L
    GH = 8
    W = 2 * S5_STATE
    for s in range(L):
        xcat_ref[:, s * 128:(s + 1) * 128] = su_ref[0, pl.ds(s, NB, stride=L), :].astype(BF16)
    for half in range(2):
        u_ref[half] = _dot(xcat_ref[:, half * 1024:(half + 1) * 1024], perm_ref[...]).astype(BF16)

    def u_of(g):
        return jnp.concatenate([u_ref[0, :, g * 128:(g + 1) * 128], u_ref[1, :, g * 128:(g + 1) * 128]], axis=1)

    for g in range(GH):
        v2 = _dot(u_of(g), p_ref[g])
        va_ref[:, g * W:(g + 1) * W] = v2[:, :W]
        vb_ref[:, g * W:(g + 1) * W] = v2[:, W:]
    ar, aia, aib = ar_ref[...], aia_ref[...], aib_ref[...]

    def step(n, carry):
        xa, xb = carry
        x_ref[pl.ds(n, 1), :] = xa
        na = xa * ar + xb * aia + va_ref[pl.ds(n, 1), :]
        nb = xb * ar + xa * aib + vb_ref[pl.ds(n, 1), :]
        return na, nb

    z = jnp.zeros((1, GH * W), F32)
    lax.fori_loop(0, NB, step, (z, z), unroll=8)
    for g in range(GH):
        y = _dot(u_of(g), m_ref[g]) + _dot(x_ref[:, g * W:(g + 1) * W].astype(BF16), q_ref[g])
        for half in range(2):
            ycat_ref[half, :, g * 128:(g + 1) * 128] = y[:, half * 128:(half + 1) * 128].astype(BF16)
    for half in range(2):
        out = _dot(ycat_ref[half], perm_ref[...])
        for t in range(8):
            y_ref[0, pl.ds(half * 8 + t, NB, stride=L), :] = out[:, t * 128:(t + 1) * 128]


def _s5(su, mats, B, S):
    m, pm, qm, ar, aia, aib = mats
    L, G, CH = S5_BLOCK, S5_GROUPS, S5_CH
    NB = S // L
    GH = 8
    W = 2 * S5_STATE
    slab = pl.BlockSpec((1, S, 128), lambda b, h: (h, b, 0))
    coef = pl.BlockSpec((1, GH * W), lambda b, h: (0, h))
    return pl.pallas_call(
        _s5_kernel,
        grid=(B, G // GH),
        in_specs=[slab,
                  pl.BlockSpec((1024, 1024), lambda b, h: (0, 0)),
                  pl.BlockSpec((GH, L * CH, L * CH), lambda b, h: (h, 0, 0)),
                  pl.BlockSpec((GH, L * CH, 2 * W), lambda b, h: (h, 0, 0)),
                  pl.BlockSpec((GH, W, L * CH), lambda b, h: (h, 0, 0)),
                  coef, coef, coef],
        out_specs=slab,
        out_shape=jax.ShapeDtypeStruct((G // GH, B * S, 128), F32),
        scratch_shapes=[pltpu.VMEM((NB, L * 128), BF16), pltpu.VMEM((2, NB, 1024), BF16),
                        pltpu.VMEM((NB, GH * W), F32), pltpu.VMEM((NB, GH * W), F32),
                        pltpu.VMEM((NB, GH * W), F32), pltpu.VMEM((2, NB, 1024), BF16)],
        compiler_params=_cparams("arbitrary", "arbitrary"),
        name="s5_scan",
    )(su, _s5_perm(), m, pm, qm, ar, aia, aib)


def _rope_kernel(pos_ref, f128_ref, f64_ref, c128_ref, s128_ref, c64_ref, s64_ref):
    pos = pos_ref[...]
    a = pos * f128_ref[...]
    c128_ref[...] = jnp.cos(a)
    s128_ref[...] = jnp.sin(a)
    a = pos * f64_ref[...]
    c64_ref[...] = jnp.cos(a)
    s64_ref[...] = jnp.sin(a)


def _rope_freq(d, width):
    rd = d // ROPE_FRAC
    half = rd // 2
    inv = ROPE_THETA ** (-jnp.arange(half, dtype=F32) * 2.0 / rd)
    per_head = jnp.concatenate([inv, inv, jnp.zeros((d - rd,), F32)])
    return jnp.tile(per_head, width // d).reshape(1, width)


def _rope_tables(positions):
    T = positions.size
    pos = positions.astype(F32).reshape(T, 1)
    tq = min(512, T)
    out = jax.ShapeDtypeStruct((T, 128), F32)
    return pl.pallas_call(
        _rope_kernel,
        grid=(T // tq,),
        in_specs=[pl.BlockSpec((tq, 1), lambda i: (i, 0)),
                  pl.BlockSpec((1, 128), lambda i: (0, 0)),
                  pl.BlockSpec((1, 128), lambda i: (0, 0))],
        out_specs=[pl.BlockSpec((tq, 128), lambda i: (i, 0))] * 4,
        out_shape=[out] * 4,
        compiler_params=_cparams("arbitrary"),
        name="rope_tables",
    )(pos, _rope_freq(DSA_HD, 128), _rope_freq(DSA_IDX_DIM, 128))


def _rope(x, cos, sin, head_dim):
    width = x.shape[-1]
    half = head_dim // ROPE_FRAC // 2
    lane = lax.broadcasted_iota(jnp.int32, (1, width), 1)
    per = lane % head_dim
    sgn_lo = jnp.where(per < half, -1.0, 0.0)
    sgn_hi = jnp.where((per >= half) & (per < 2 * half), 1.0, 0.0)
    up = pltpu.roll(x, width - half, axis=1)
    dn = pltpu.roll(x, half, axis=1)
    return x * cos + sin * (up * sgn_lo + dn * sgn_hi)


def _dsa_prep_kernel(dq_ref, dk_ref, dv_ref, diq_ref, misc_ref, c128_ref, s128_ref, c64_ref, s64_ref,
                     qg_ref, kg_ref, qt_out, k_out, vt_out, iqt_out, ik_out, wt_out):
    c128, s128 = c128_ref[...], s128_ref[...]
    c64, s64 = c64_ref[...], s64_ref[...]
    for h in range(DSA_HEADS):
        sl = slice(h * DSA_HD, (h + 1) * DSA_HD)
        q = _rope(_rms(dq_ref[:, sl].astype(F32), qg_ref[...]), c128, s128, DSA_HD)
        qt_out[h] = (q * (DSA_HD ** -0.5)).T.astype(qt_out.dtype)
    k = _rope(_rms(dk_ref[...].astype(F32), kg_ref[...]), c128, s128, DSA_HD)
    k_out[...] = k.astype(k_out.dtype)
    for c in range(vt_out.shape[0]):
        vt_out[c] = dv_ref[c * DSA_KCHUNK:(c + 1) * DSA_KCHUNK, :].astype(F32).T.astype(vt_out.dtype)
    for j in range(DSA_IDX_HEADS // 2):
        sl = slice(j * 128, (j + 1) * 128)
        xt = _rope(diq_ref[:, sl].astype(F32), c64, s64, DSA_IDX_DIM).T
        iqt_out[2 * j] = xt[:DSA_IDX_DIM].astype(iqt_out.dtype)
        iqt_out[2 * j + 1] = xt[DSA_IDX_DIM:].astype(iqt_out.dtype)
    misc = misc_ref[...].astype(F32)
    ik_out[...] = _rope(misc, c64, s64, DSA_IDX_DIM)[:, :DSA_IDX_DIM].astype(ik_out.dtype)
    wt_out[...] = misc.T[DIW_LANE:DIW_LANE + DSA_IDX_HEADS]


def _dsa_prep(p, tables, q_gain, k_gain):
    T = p.shape[0]
    tq = min(512, T)
    assert tq % DSA_KCHUNK == 0
    tab = pl.BlockSpec((tq, 128), lambda i: (i, 0))
    gain = pl.BlockSpec((1, 128), lambda i: (0, 0))
    return pl.pallas_call(
        _dsa_prep_kernel,
        grid=(T // tq,),
        in_specs=[pl.BlockSpec((tq, 512), lambda i: (i, _COL["dq"] // 512)),
                  pl.BlockSpec((tq, 128), lambda i: (i, _COL["dk"] // 128)),
                  pl.BlockSpec((tq, 128), lambda i: (i, _COL["dv"] // 128)),
                  pl.BlockSpec((tq, 1024), lambda i: (i, _COL["diq"] // 1024)),
                  pl.BlockSpec((tq, 128), lambda i: (i, MISC_COL // 128)),
                  tab, tab, tab, tab, gain, gain],
        out_specs=[pl.BlockSpec((DSA_HEADS, DSA_HD, tq), lambda i: (0, 0, i)),
                   pl.BlockSpec((tq, DSA_HD), lambda i: (i, 0)),
                   pl.BlockSpec((tq // DSA_KCHUNK, DSA_HD, DSA_KCHUNK), lambda i: (i, 0, 0)),
                   pl.BlockSpec((DSA_IDX_HEADS, DSA_IDX_DIM, tq), lambda i: (0, 0, i)),
                   pl.BlockSpec((tq, DSA_IDX_DIM), lambda i: (i, 0)),
                   pl.BlockSpec((DSA_IDX_HEADS, tq), lambda i: (0, i))],
        out_shape=[jax.ShapeDtypeStruct((DSA_HEADS, DSA_HD, T), BF16),
                   jax.ShapeDtypeStruct((T, DSA_HD), BF16),
                   jax.ShapeDtypeStruct((T // DSA_KCHUNK, DSA_HD, DSA_KCHUNK), BF16),
                   jax.ShapeDtypeStruct((DSA_IDX_HEADS, DSA_IDX_DIM, T), BF16),
                   jax.ShapeDtypeStruct((T, DSA_IDX_DIM), BF16),
                   jax.ShapeDtypeStruct((DSA_IDX_HEADS, T), F32)],
        compiler_params=_cparams("arbitrary"),
        name="dsa_prep",
    )(p, p, p, p, p, *tables, q_gain, k_gain)


def _dsa_kernel(iqt_ref, wt_ref, qt_ref, ik_ref, k_ref, vt_ref, o_ref,
                key_ref, half_ref, bias_ref, s_ref, p_ref, acc_ref, *, ksel):
    TQ, CK = DSA_QBLOCK, DSA_KCHUNK
    HK = CK // 2
    i = pl.program_id(1)
    nk = (i * TQ + TQ + CK - 1) // CK
    t_idx = i * TQ + lax.broadcasted_iota(jnp.int32, (1, TQ), 1)
    w_all = wt_ref[...] * (DSA_IDX_HEADS ** -0.5 * DSA_IDX_DIM ** -0.5)

    def score_chunk(kc, carry):
        for half in range(2):
            lhs = ik_ref[0, kc, half * HK:(half + 1) * HK, :]
            acc = None
            for h in range(DSA_IDX_HEADS):
                term = w_all[h:h + 1] * jnp.maximum(_dot(lhs, iqt_ref[h]), 0.0)
                acc = term if acc is None else acc + term
            s_idx = kc * CK + half * HK + lax.broadcasted_iota(jnp.int32, (HK, 1), 0)
            bits = lax.bitcast_convert_type(acc, jnp.int32)
            key = bits ^ ((bits >> 31) & 0x7FFFFFFF)
            key = jnp.where(s_idx <= t_idx, key, INT_MIN)
            key_ref[kc, half * HK:(half + 1) * HK, :] = key
            half_ref[kc, half * HK:(half + 1) * HK, :] = (key >> 16).astype(jnp.int16)
        return carry

    lax.fori_loop(0, nk, score_chunk, 0)
    npair = (nk + 1) // 2

    @pl.when(nk % 2 == 1)
    def _():
        key_ref[nk] = jnp.full((CK, TQ), INT_MIN, jnp.int32)
        half_ref[nk] = jnp.full((CK, TQ), I16_MIN, jnp.int16)

    chains = 4
    groups = 2 * CK // (16 * chains)

    def count_ge(cand):
        cb = jnp.broadcast_to(cand.astype(jnp.int16), (16, TQ))[None, None]

        def count(j, acc):
            keys = half_ref[pl.ds(2 * j, 2)].reshape(groups, chains, 16, TQ)
            for g in range(groups):
                acc = acc + jnp.where(keys[g:g + 1] >= cb, jnp.int16(1), jnp.int16(0))[0]
            return acc

        acc = lax.fori_loop(0, npair, count, jnp.zeros((chains, 16, TQ), jnp.int16))
        return jnp.sum(jnp.sum(acc.astype(jnp.int32), axis=0), axis=0, keepdims=True)

    def select16(need):
        def bit_step(bi, thr):
            cand = thr + jnp.left_shift(jnp.int32(1), 15 - bi)
            return jnp.where(count_ge(cand) >= need, cand, thr)

        return lax.fori_loop(0, 16, bit_step, jnp.full((1, TQ), I16_MIN, jnp.int32))

    thr_hi = select16(ksel)
    above = jnp.where(thr_hi >= -I16_MIN - 1, 0, count_ge(jnp.minimum(thr_hi + 1, -I16_MIN - 1)))

    def low_half(j, carry):
        keys = key_ref[pl.ds(2 * j, 2)]
        low = (keys & 0xFFFF) + I16_MIN
        half_ref[pl.ds(2 * j, 2)] = jnp.where((keys >> 16) == thr_hi, low, I16_MIN).astype(jnp.int16)
        return carry

    lax.fori_loop(0, npair, low_half, 0)
    thr_lo = select16(ksel - above)
    thr = thr_hi * 65536 + (thr_lo - I16_MIN)
    thr = jnp.maximum(thr, INT_MIN + 1)

    kg = CK // 16
    unit = 64

    def logits(kc, buf, m_run):
        bias_ref[...] = jnp.where(key_ref[kc] >= thr, 0.0, -1e30)
        kk = k_ref[0, kc]
        new = []
        for h in range(DSA_HEADS):
            s = _dot(kk, qt_ref[h]) + bias_ref[...]
            s_ref[buf, h] = s
            mx = jnp.max(s.reshape(kg, 2, 8, TQ), axis=0)
            new.append(jnp.maximum(m_run[h], jnp.max(jnp.max(mx, axis=0), axis=0, keepdims=True)))
        return tuple(new)

    def attn_chunk(kc, carry):
        m_old, m_run, ls = carry
        cur = kc % 2
        vt = vt_ref[0, kc]
        new_l = []
        for h in range(DSA_HEADS):
            alpha = jnp.exp(m_old[h] - m_run[h])
            mb = jnp.broadcast_to(m_run[h], (8, TQ))[None]
            part = ls[h] * alpha
            for u in range(CK // unit):
                rows = slice(u * unit, (u + 1) * unit)
                pr = jnp.exp(s_ref[cur, h, rows, :].reshape(unit // 8, 8, TQ) - mb)
                part = part + jnp.sum(pr.reshape(unit // 16, 2, 8, TQ), axis=0)
                p_ref[h, rows, :] = pr.reshape(unit, TQ).astype(BF16)
            new_l.append(part)
            acc_ref[h] = acc_ref[h] * alpha + _dot(vt, p_ref[h])
        m_next = logits(jnp.minimum(kc + 1, nk - 1), 1 - cur, m_run)
        return m_run, m_next, tuple(new_l)

    acc_ref[...] = jnp.zeros_like(acc_ref)
    floor = tuple(jnp.full((1, TQ), -1e30, F32) for _ in range(DSA_HEADS))
    init = (floor, logits(0, 0, floor), tuple(jnp.zeros((2, 8, TQ), F32) for _ in range(DSA_HEADS)))
    _, _, ls = lax.fori_loop(0, nk, attn_chunk, init)
    for h in range(DSA_HEADS):
        denom = jnp.sum(jnp.sum(ls[h], axis=0), axis=0, keepdims=True)
        o_ref[:, h * DSA_HD:(h + 1) * DSA_HD] = (acc_ref[h] / denom).T.astype(o_ref.dtype)


def _dsa(qt, k_rot, vt, iqt, ik_rot, wt, B, S):
    TQ, CK = DSA_QBLOCK, DSA_KCHUNK
    NC = S // CK
    nq = S // TQ
    assert NC % 2 == 0 and TQ % CK == 0
    ksel = min(DSA_TOPK, S // 4)
    ik = ik_rot.reshape(B, NC, CK, DSA_IDX_DIM)
    k = k_rot.reshape(B, NC, CK, DSA_HD)
    vt = vt.reshape(B, NC, DSA_HD, CK)
    return pl.pallas_call(
        functools.partial(_dsa_kernel, ksel=ksel),
        grid=(B, nq),
        in_specs=[pl.BlockSpec((DSA_IDX_HEADS, DSA_IDX_DIM, TQ), lambda b, i: (0, 0, b * nq + i)),
                  pl.BlockSpec((DSA_IDX_HEADS, TQ), lambda b, i: (0, b * nq + i)),
                  pl.BlockSpec((DSA_HEADS, DSA_HD, TQ), lambda b, i: (0, 0, b * nq + i)),
                  pl.BlockSpec((1, NC, CK, DSA_IDX_DIM), lambda b, i: (b, 0, 0, 0)),
                  pl.BlockSpec((1, NC, CK, DSA_HD), lambda b, i: (b, 0, 0, 0)),
                  pl.BlockSpec((1, NC, DSA_HD, CK), lambda b, i: (b, 0, 0, 0))],
        out_specs=pl.BlockSpec((TQ, DSA_HEADS * DSA_HD), lambda b, i: (b * nq + i, 0)),
        out_shape=jax.ShapeDtypeStruct((B * S, DSA_HEADS * DSA_HD), BF16),
        scratch_shapes=[pltpu.VMEM((NC, CK, TQ), jnp.int32),
                        pltpu.VMEM((NC, CK, TQ), jnp.int16),
                        pltpu.VMEM((CK, TQ), F32),
                        pltpu.VMEM((2, DSA_HEADS, CK, TQ), F32),
                        pltpu.VMEM((DSA_HEADS, CK, TQ), BF16),
                        pltpu.VMEM((DSA_HEADS, DSA_HD, TQ), F32)],
        compiler_params=_cparams("arbitrary", "arbitrary"),
        name="dsa_attn",
    )(iqt, wt, qt, ik, k, vt)


def _mem_kv_kernel(mem_ref, g_ref, w_ref, kg_ref, k_out, v_out):
    h = _rms(mem_ref[...], g_ref[...]).astype(BF16)
    kv = _dot(h, w_ref[...])
    width = MEM_HEADS * MEM_HD
    for hd in range(MEM_HEADS):
        sl = slice(hd * MEM_HD, (hd + 1) * MEM_HD)
        k_out[:, sl] = _rms(kv[:, sl], kg_ref[...]).astype(k_out.dtype)
    v_out[...] = kv[:, width:].astype(v_out.dtype)


def _mem_kv(mem2d, gain, w_kv, k_gain, B, l):
    width = MEM_HEADS * MEM_HD
    out = jax.ShapeDtypeStruct((B * MEM_LEN, width), BF16)
    return pl.pallas_call(
        _mem_kv_kernel,
        grid=(B,),
        in_specs=[pl.BlockSpec((MEM_LEN, D_MODEL), lambda b: (b, 0)),
                  pl.BlockSpec((1, D_MODEL), lambda b: (0, 0)),
                  _layer((D_MODEL, 2 * width), l),
                  pl.BlockSpec((1, MEM_HD), lambda b: (0, 0))],
        out_specs=[pl.BlockSpec((MEM_LEN, width), lambda b: (b, 0))] * 2,
        out_shape=[out, out],
        compiler_params=_cparams("arbitrary"),
        name="mem_kv",
    )(mem2d, gain, w_kv, k_gain)


def _mem_attn_kernel(q_ref, k_ref, v_ref, qg_ref, o_ref):
    for h in range(MEM_HEADS):
        sl = slice(h * MEM_HD, (h + 1) * MEM_HD)
        q = _rms(q_ref[:, sl].astype(F32), qg_ref[...]) * (MEM_HD ** -0.5)
        s = _dot_nt(q.astype(BF16), k_ref[:, sl])
        pr = jnp.exp(s - jnp.max(s, axis=1, keepdims=True))
        o = _dot(pr.astype(BF16), v_ref[:, sl]) / jnp.sum(pr, axis=1, keepdims=True)
        o_ref[:, sl] = o.astype(o_ref.dtype)


def _mem_attn(p, mk, mv, q_gain, B, S):
    tq = min(512, S)
    nq = S // tq
    width = MEM_HEADS * MEM_HD
    return pl.pallas_call(
        _mem_attn_kernel,
        grid=(B, nq),
        in_specs=[pl.BlockSpec((tq, width), lambda b, i: (b * nq + i, _COL["mq"] // width)),
                  pl.BlockSpec((MEM_LEN, width), lambda b, i: (b, 0)),
                  pl.BlockSpec((MEM_LEN, width), lambda b, i: (b, 0)),
                  pl.BlockSpec((1, MEM_HD), lambda b, i: (0, 0))],
        out_specs=pl.BlockSpec((tq, width), lambda b, i: (b * nq + i, 0)),
        out_shape=jax.ShapeDtypeStruct((B * S, width), BF16),
        compiler_params=_cparams("arbitrary", "arbitrary"),
        name="mem_attn",
    )(p, mk, mv, q_gain)


def _out_proj_kernel(og_ref, ys_ref, od_ref, om_ref, z_ref, x_ref, gn_ref, wglu_ref, w_ref, o_ref, lhs_ref):
    y = jax.nn.gelu(jnp.concatenate([ys_ref[q] for q in range(ys_ref.shape[0])], axis=1))
    o_s5 = y * jax.nn.sigmoid(_dot(y.astype(BF16), wglu_ref[...]))
    branches = (og_ref[...].astype(F32), o_s5, od_ref[...].astype(F32), om_ref[...].astype(F32))
    for bi, o in enumerate(branches):
        for h in range(o.shape[1] // HEAD):
            c0 = bi * 512 + h * HEAD
            z = z_ref[:, c0:c0 + HEAD].astype(F32)
            n = _rms(o[:, h * HEAD:(h + 1) * HEAD], gn_ref[:, c0:c0 + HEAD])
            lhs_ref[:, c0:c0 + HEAD] = (n * (z * jax.nn.sigmoid(z))).astype(BF16)
    o_ref[...] = x_ref[...] + _dot(lhs_ref[...], w_ref[...])


def _out_proj(o_gla, y_s5, o_dsa, o_mem, p, x2d, out_gain, w_glu, w_out, l):
    T = x2d.shape[0]
    tm = min(512, T)
    branch = pl.BlockSpec((tm, 512), lambda i: (i, 0))
    return pl.pallas_call(
        _out_proj_kernel,
        grid=(T // tm,),
        in_specs=[branch, pl.BlockSpec((S5_WIDTH // 128, tm, 128), lambda i: (0, i, 0)), branch, branch,
                  pl.BlockSpec((tm, MIX_WIDTH), lambda i: (i, 0)),
                  pl.BlockSpec((tm, D_MODEL), lambda i: (i, 0)),
                  _resident((1, MIX_WIDTH)),
                  _layer((S5_WIDTH, S5_WIDTH), l),
                  _layer((MIX_WIDTH, D_MODEL), l)],
        out_specs=pl.BlockSpec((tm, D_MODEL), lambda i: (i, 0)),
        out_shape=jax.ShapeDtypeStruct((T, D_MODEL), F32),
        scratch_shapes=[pltpu.VMEM((tm, MIX_WIDTH), BF16)],
        compiler_params=_cparams("arbitrary"),
        name="out_proj",
    )(o_gla, y_s5, o_dsa, o_mem, p, x2d, out_gain, w_glu, w_out)


def _permute_w_in(w_in):
    pieces = [w_in[..., _ORIG[n][0]:_ORIG[n][1]].astype(BF16) for n in _ORDER]
    used = sum(_ORIG[n][1] - _ORIG[n][0] for n in _ORDER)
    pad = jnp.zeros(w_in.shape[:-1] + (IN_PAD - used,), BF16)
    return jnp.concatenate(pieces + [pad], axis=-1)


def kernel(x, mem, positions, ln_gain, w_in, gla_w_up, gla_b_up, s5_a_re, s5_a_im, s5_log_step, s5_b_re, s5_b_im, s5_c_re, s5_c_im, s5_d, s5_w_glu, dsa_q_norm, dsa_k_norm, mem_norm, w_mem_kv, mem_q_norm, mem_k_norm, out_norm, w_out):
    B, S, _ = x.shape
    depth = w_in.shape[0]
    T = B * S
    x2d = x.reshape(T, D_MODEL)
    mem2d = mem.reshape(B * mem.shape[1], D_MODEL)
    w_in_p = _permute_w_in(w_in)
    wup_pad = jnp.zeros((depth, 128, GLA_HEADS * GLA_DK), F32)
    wup_pad = wup_pad.at[:, GLR_LANE:GLR_LANE + GLA_GATE_RANK].set(gla_w_up).astype(BF16)
    w_kv = w_mem_kv.astype(BF16)
    w_glu = s5_w_glu.astype(BF16)
    w_o = w_out.astype(BF16)
    tables = _rope_tables(positions)
    for l in range(depth):
        p, su = _in_proj(x2d, ln_gain[l][None], w_in_p, l)
        o_gla = _gla(p, wup_pad[l], gla_b_up[l][None], B, S)
        mats = _s5_mats(s5_a_re[l], s5_a_im[l], s5_log_step[l], s5_b_re[l], s5_b_im[l],
                        s5_c_re[l], s5_c_im[l], s5_d[l])
        y_s5 = _s5(su, mats, B, S)
        qt, k_rot, vt, iqt, ik_rot, wt = _dsa_prep(p, tables, dsa_q_norm[l][None], dsa_k_norm[l][None])
        o_dsa = _dsa(qt, k_rot, vt, iqt, ik_rot, wt, B, S)
        mk, mv = _mem_kv(mem2d, mem_norm[l][None], w_kv, mem_k_norm[l][None], B, l)
        o_mem = _mem_attn(p, mk, mv, mem_q_norm[l][None], B, S)
        x2d = _out_proj(o_gla, y_s5, o_dsa, o_mem, p, x2d, out_norm[l][None], w_glu, w_o, l)
    return x2d.reshape(B, S, D_MODEL)
```

```python
import functools

import numpy as np
import jax
import jax.numpy as jnp
from jax import lax
from jax.experimental import pallas as pl
from jax.experimental.pallas import tpu as pltpu

F32 = jnp.float32
BF16 = jnp.bfloat16

D_MODEL = 2048
MEM_LEN = 256
ROPE_THETA = 500000.0
ROPE_FRAC = 4
NORM_EPS = 1e-6

GLA_HEADS = 4
GLA_DK = 64
GLA_DV = 128
GLA_GATE_RANK = 16
GLA_TAU = 16.0
GLA_CHUNK = 64

S5_GROUPS = 32
S5_CH = 16
S5_STATE = 64
S5_WIDTH = S5_GROUPS * S5_CH
S5_BLOCK = 16

DSA_HEADS = 4
DSA_HD = 128
DSA_IDX_HEADS = 16
DSA_IDX_DIM = 64
DSA_TOPK = 256
DSA_QBLOCK = 256
DSA_KCHUNK = 256

MEM_HEADS = 4
MEM_HD = 128
HEAD = 128
MIX_WIDTH = 2048

VMEM_LIMIT_BYTES = 52 * 1024 * 1024

_ORIG = dict(gq=(0, 256), gk=(256, 512), gv=(512, 1024), glr=(1024, 1040), gz=(1040, 1552),
             su=(1552, 2064), sz=(2064, 2576), dq=(2576, 3088), dk=(3088, 3216), dv=(3216, 3344),
             diq=(3344, 4368), dik=(4368, 4432), diw=(4432, 4448), dz=(4448, 4960),
             mq=(4960, 5472), mz=(5472, 5984))
_ORDER = ("gz", "sz", "dz", "mz", "diq", "gv", "su", "dq", "mq", "gq", "gk", "dk", "dv",
          "dik", "glr", "diw")
_COL = {}
_c = 0
for _n in _ORDER:
    _COL[_n] = _c
    _c += _ORIG[_n][1] - _ORIG[_n][0]
IN_PAD = 6144
MISC_COL = _COL["dik"]
GLR_LANE = 64
DIW_LANE = 80
INT_MIN = -(2 ** 31)
I16_MIN = -(2 ** 15)


def _cparams(*sem):
    return pltpu.CompilerParams(dimension_semantics=sem, vmem_limit_bytes=VMEM_LIMIT_BYTES)


def _dot(a, b):
    return jnp.dot(a, b, preferred_element_type=F32)


def _dot_nt(a, b):
    return lax.dot_general(a, b, (((1,), (1,)), ((), ())), preferred_element_type=F32)


def _rms(x, gain):
    return x * lax.rsqrt(jnp.mean(x * x, axis=-1, keepdims=True) + NORM_EPS) * gain


def _in_proj_kernel(x_ref, g_ref, w_ref, o_ref, su_ref, h_ref):
    h_ref[...] = _rms(x_ref[...], g_ref[...]).astype(BF16)
    tn = S5_WIDTH
    for j in range(IN_PAD // tn):
        acc = _dot(h_ref[...], w_ref[:, j * tn:(j + 1) * tn])
        o_ref[:, j * tn:(j + 1) * tn] = acc.astype(o_ref.dtype)
        if j == _COL["su"] // tn:
            for q in range(tn // 128):
                su_ref[q] = acc[:, q * 128:(q + 1) * 128]


def _resident(shape):
    return pl.BlockSpec(shape, lambda *_: (0,) * len(shape), pipeline_mode=pl.Buffered(1))


def _layer(shape, l):
    return pl.BlockSpec((None,) + shape, lambda *_: (l,) + (0,) * len(shape), pipeline_mode=pl.Buffered(1))


def _in_proj(x2d, gain, w, l):
    T = x2d.shape[0]
    tm = min(256, T)
    return pl.pallas_call(
        _in_proj_kernel,
        grid=(T // tm,),
        in_specs=[pl.BlockSpec((tm, D_MODEL), lambda i: (i, 0)),
                  _resident((1, D_MODEL)),
                  _layer((D_MODEL, IN_PAD), l)],
        out_specs=[pl.BlockSpec((tm, IN_PAD), lambda i: (i, 0)),
                   pl.BlockSpec((S5_WIDTH // 128, tm, 128), lambda i: (0, i, 0))],
        out_shape=[jax.ShapeDtypeStruct((T, IN_PAD), BF16),
                   jax.ShapeDtypeStruct((S5_WIDTH // 128, T, 128), F32)],
        scratch_shapes=[pltpu.VMEM((tm, D_MODEL), BF16)],
        compiler_params=_cparams("arbitrary"),
        name="in_proj",
    )(x2d, gain, w)


def _gla_kernel(q_ref, k_ref, v_ref, misc_ref, wup_ref, bup_ref, o_ref, st_ref):
    @pl.when(pl.program_id(1) == 0)
    def _():
        st_ref[...] = jnp.zeros_like(st_ref)

    C = GLA_CHUNK
    tc = q_ref.shape[0]
    nchunk = tc // C
    x = _dot(misc_ref[...], wup_ref[...]) + bup_ref[...]
    log_a = (jnp.minimum(x, 0.0) - jnp.log(1.0 + jnp.exp(-jnp.abs(x)))) * (1.0 / GLA_TAU)
    row = lax.broadcasted_iota(jnp.int32, (tc, tc), 0)
    col = lax.broadcasted_iota(jnp.int32, (tc, tc), 1)
    tril = (row >= col) & ((row // C) == (col // C))
    la_hi = log_a.astype(BF16)
    r1 = log_a - la_hi.astype(F32)
    la_mid = r1.astype(BF16)
    la_lo = (r1 - la_mid.astype(F32)).astype(BF16)
    tri3 = jnp.concatenate([tril.astype(BF16)] * 3, axis=1)
    b = _dot(tri3, jnp.concatenate([la_hi, la_mid, la_lo], axis=0))
    lane = lax.broadcasted_iota(jnp.int32, (1, 2 * GLA_DK), 1)
    low = lane < GLA_DK
    srow = lax.broadcasted_iota(jnp.int32, (2 * GLA_DV, 1), 0)
    diag = (srow < GLA_DV) == low
    for j in range(GLA_HEADS // 2):
        ls = slice(j * 128, (j + 1) * 128)
        bp = b[:, ls]
        mid = jnp.concatenate([jnp.broadcast_to(bp[c * C + C // 2:c * C + C // 2 + 1], (C, 128))
                               for c in range(nchunk)], axis=0)
        last = jnp.concatenate([jnp.broadcast_to(bp[c * C + C - 1:c * C + C], (C, 128))
                                for c in range(nchunk)], axis=0)
        qf = q_ref[:, ls].astype(F32) * (GLA_DK ** -0.5)
        kf = k_ref[:, ls].astype(F32)
        qe = qf * jnp.exp(bp - mid)
        ke_p = (kf * jnp.exp(mid - bp)).astype(BF16)
        kd_p = (kf * jnp.exp(last - bp)).astype(BF16)
        qi_p = (qf * jnp.exp(bp)).astype(BF16)
        intra = []
        for hh in range(2):
            h = 2 * j + hh
            sel = low if hh == 0 else jnp.logical_not(low)
            qe_h = jnp.where(sel, qe, 0.0).astype(BF16)
            attn = jnp.where(tril, _dot_nt(qe_h, ke_p), 0.0)
            intra.append(_dot(attn.astype(BF16), v_ref[:, h * GLA_DV:(h + 1) * GLA_DV]))
        intra = jnp.concatenate(intra, axis=1)
        vs = slice(2 * j * GLA_DV, (2 * j + 2) * GLA_DV)
        st = st_ref[j]
        for c in range(nchunk):
            rs = slice(c * C, (c + 1) * C)
            o = intra[rs] + _dot_nt(qi_p[rs], st.astype(BF16))
            o_ref[rs, vs] = o.astype(o_ref.dtype)
            v_t = v_ref[rs, vs].astype(F32).T.astype(BF16)
            dec = jnp.exp(bp[c * C + C - 1:c * C + C])
            st = jnp.where(diag, st * dec + _dot(v_t, kd_p[rs]), 0.0)
        st_ref[j] = st


def _gla(p, wup_pad, bup, B, S):
    tc = min(256, S)
    nc = S // tc
    row = lambda b, c: b * nc + c
    return pl.pallas_call(
        _gla_kernel,
        grid=(B, nc),
        in_specs=[pl.BlockSpec((tc, 256), lambda b, c: (row(b, c), _COL["gq"] // 256)),
                  pl.BlockSpec((tc, 256), lambda b, c: (row(b, c), _COL["gk"] // 256)),
                  pl.BlockSpec((tc, 512), lambda b, c: (row(b, c), _COL["gv"] // 512)),
                  pl.BlockSpec((tc, 128), lambda b, c: (row(b, c), MISC_COL // 128)),
                  _resident((128, 256)),
                  _resident((1, 256))],
        out_specs=pl.BlockSpec((tc, 512), lambda b, c: (row(b, c), 0)),
        out_shape=jax.ShapeDtypeStruct((B * S, GLA_HEADS * GLA_DV), BF16),
        scratch_shapes=[pltpu.VMEM((GLA_HEADS // 2, 2 * GLA_DV, 2 * GLA_DK), F32)],
        compiler_params=_cparams("arbitrary", "arbitrary"),
        name="gla",
    )(p, p, p, p, wup_pad, bup)


def _s5_mats(a_re, a_im, log_step, b_re, b_im, c_re, c_im, d_skip):
    L, P, CH, G = S5_BLOCK, S5_STATE, S5_CH, S5_GROUPS
    hi = lax.Precision.HIGHEST
    step = jnp.exp(log_step.astype(F32))[:, None]
    are, aim = a_re.astype(F32), a_im.astype(F32)

    def power(j):
        mag = jnp.exp(are[:, None, :] * step[:, None, :] * j[None, :, None])
        ang = aim[:, None, :] * step[:, None, :] * j[None, :, None]
        return mag * jnp.cos(ang), mag * jnp.sin(ang)

    t = jnp.arange(L, dtype=F32)
    lb_re, lb_im = (v[:, 0] for v in power(jnp.ones((1,), F32)))
    den = are * are + aim * aim
    f_re = ((lb_re - 1.0) * are + lb_im * aim) / den
    f_im = (lb_im * are - (lb_re - 1.0) * aim) / den
    bt_re, bt_im = jnp.swapaxes(b_re, 1, 2), jnp.swapaxes(b_im, 1, 2)
    bb_re = f_re[:, None] * bt_re - f_im[:, None] * bt_im
    bb_im = f_re[:, None] * bt_im + f_im[:, None] * bt_re
    n_re, n_im = power(-t)
    a_r = (n_re[:, :, None] * bb_re[:, None] - n_im[:, :, None] * bb_im[:, None]).reshape(G, L * CH, P)
    a_i = (n_re[:, :, None] * bb_im[:, None] + n_im[:, :, None] * bb_re[:, None]).reshape(G, L * CH, P)
    p_re, p_im = power(t)
    ct_re, ct_im = jnp.swapaxes(c_re, 1, 2), jnp.swapaxes(c_im, 1, 2)
    pt_re, pt_im = jnp.swapaxes(p_re, 1, 2), jnp.swapaxes(p_im, 1, 2)
    b_r = (pt_re[..., None] * ct_re[:, :, None] - pt_im[..., None] * ct_im[:, :, None]).reshape(G, P, L * CH)
    b_i = (pt_re[..., None] * ct_im[:, :, None] + pt_im[..., None] * ct_re[:, :, None]).reshape(G, P, L * CH)
    m = (jnp.einsum("gxp,gpy->gxy", a_r, b_r, precision=hi) - jnp.einsum("gxp,gpy->gxy", a_i, b_i, precision=hi))
    blk = np.arange(L * CH) // CH
    m = jnp.where((blk[None, :] >= blk[:, None])[None], m, 0.0)
    m = m + jnp.eye(L * CH, dtype=F32)[None] * jnp.tile(d_skip.reshape(G, 1, CH), (1, 1, L))
    e_re, e_im = (v[:, 0][:, None] for v in power(jnp.full((1,), L - 1.0, F32)))
    pr, pi = a_r * e_re - a_i * e_im, a_r * e_im + a_i * e_re
    pm = jnp.concatenate([pr, pi, pi, pr], axis=-1)
    l_re, l_im = lb_re[..., None], lb_im[..., None]
    qm = jnp.concatenate([b_r * l_re - b_i * l_im, -(b_r * l_im + b_i * l_re)], axis=1)
    al_re, al_im = (v[:, 0] for v in power(jnp.full((1,), float(L), F32)))
    ar = jnp.concatenate([al_re, al_re], axis=-1).reshape(1, G * 2 * P)
    ai_a = jnp.concatenate([-al_im, al_im], axis=-1).reshape(1, G * 2 * P)
    ai_b = jnp.concatenate([al_im, -al_im], axis=-1).reshape(1, G * 2 * P)
    return m.astype(BF16), pm.astype(BF16), qm.astype(BF16), ar, ai_a, ai_b


def _s5_perm():
    src = np.arange(1024).reshape(8, 8, S5_CH).transpose(1, 0, 2).reshape(-1)
    m = np.zeros((1024, 1024), np.float32)
    m[np.arange(1024), src] = 1.0
    return jnp.asarray(m, BF16)


def _s5_kernel(su_ref, perm_ref, m_ref, p_ref, q_ref, ar_ref, aia_ref, aib_ref, y_ref,
               xcat_ref, u_ref, va_ref, vb_ref, x_ref, ycat_ref):
    L = S5_BLOCK
    NB = su_ref.shape[1] // L
    GH = 8
    W = 2 * S5_STATE
    for s in range(L):
        xcat_ref[:, s * 128:(s + 1) * 128] = su_ref[0, pl.ds(s, NB, stride=L), :].astype(BF16)
    for half in range(2):
        u_ref[half] = _dot(xcat_ref[:, half * 1024:(half + 1) * 1024], perm_ref[...]).astype(BF16)

    def u_of(g):
        return jnp.concatenate([u_ref[0, :, g * 128:(g + 1) * 128], u_ref[1, :, g * 128:(g + 1) * 128]], axis=1)

    for g in range(GH):
        v2 = _dot(u_of(g), p_ref[g])
        va_ref[:, g * W:(g + 1) * W] = v2[:, :W]
        vb_ref[:, g * W:(g + 1) * W] = v2[:, W:]
    ar, aia, aib = ar_ref[...], aia_ref[...], aib_ref[...]

    def step(n, carry):
        xa, xb = carry
        x_ref[pl.ds(n, 1), :] = xa
        na = xa * ar + xb * aia + va_ref[pl.ds(n, 1), :]
        nb = xb * ar + xa * aib + vb_ref[pl.ds(n, 1), :]
        return na, nb

    z = jnp.zeros((1, GH * W), F32)
    lax.fori_loop(0, NB, step, (z, z), unroll=8)
    for g in range(GH):
        y = _dot(u_of(g), m_ref[g]) + _dot(x_ref[:, g * W:(g + 1) * W].astype(BF16), q_ref[g])
        for half in range(2):
            ycat_ref[half, :, g * 128:(g + 1) * 128] = y[:, half * 128:(half + 1) * 128].astype(BF16)
    for half in range(2):
        out = _dot(ycat_ref[half], perm_ref[...])
        for t in range(8):
            y_ref[0, pl.ds(half * 8 + t, NB, stride=L), :] = out[:, t * 128:(t + 1) * 128]


def _s5(su, mats, B, S):
    m, pm, qm, ar, aia, aib = mats
    L, G, CH = S5_BLOCK, S5_GROUPS, S5_CH
    NB = S // L
    GH = 8
    W = 2 * S5_STATE
    slab = pl.BlockSpec((1, S, 128), lambda b, h: (h, b, 0))
    coef = pl.BlockSpec((1, GH * W), lambda b, h: (0, h))
    return pl.pallas_call(
        _s5_kernel,
        grid=(B, G // GH),
        in_specs=[slab,
                  pl.BlockSpec((1024, 1024), lambda b, h: (0, 0)),
                  pl.BlockSpec((GH, L * CH, L * CH), lambda b, h: (h, 0, 0)),
                  pl.BlockSpec((GH, L * CH, 2 * W), lambda b, h: (h, 0, 0)),
                  pl.BlockSpec((GH, W, L * CH), lambda b, h: (h, 0, 0)),
                  coef, coef, coef],
        out_specs=slab,
        out_shape=jax.ShapeDtypeStruct((G // GH, B * S, 128), F32),
        scratch_shapes=[pltpu.VMEM((NB, L * 128), BF16), pltpu.VMEM((2, NB, 1024), BF16),
                        pltpu.VMEM((NB, GH * W), F32), pltpu.VMEM((NB, GH * W), F32),
                        pltpu.VMEM((NB, GH * W), F32), pltpu.VMEM((2, NB, 1024), BF16)],
        compiler_params=_cparams("arbitrary", "arbitrary"),
        name="s5_scan",
    )(su, _s5_perm(), m, pm, qm, ar, aia, aib)


def _rope_kernel(pos_ref, f128_ref, f64_ref, c128_ref, s128_ref, c64_ref, s64_ref):
    pos = pos_ref[...]
    a = pos * f128_ref[...]
    c128_ref[...] = jnp.cos(a)
    s128_ref[...] = jnp.sin(a)
    a = pos * f64_ref[...]
    c64_ref[...] = jnp.cos(a)
    s64_ref[...] = jnp.sin(a)


def _rope_freq(d, width):
    rd = d // ROPE_FRAC
    half = rd // 2
    inv = ROPE_THETA ** (-jnp.arange(half, dtype=F32) * 2.0 / rd)
    per_head = jnp.concatenate([inv, inv, jnp.zeros((d - rd,), F32)])
    return jnp.tile(per_head, width // d).reshape(1, width)


def _rope_tables(positions):
    T = positions.size
    pos = positions.astype(F32).reshape(T, 1)
    tq = min(512, T)
    out = jax.ShapeDtypeStruct((T, 128), F32)
    return pl.pallas_call(
        _rope_kernel,
        grid=(T // tq,),
        in_specs=[pl.BlockSpec((tq, 1), lambda i: (i, 0)),
                  pl.BlockSpec((1, 128), lambda i: (0, 0)),
                  pl.BlockSpec((1, 128), lambda i: (0, 0))],
        out_specs=[pl.BlockSpec((tq, 128), lambda i: (i, 0))] * 4,
        out_shape=[out] * 4,
        compiler_params=_cparams("arbitrary"),
        name="rope_tables",
    )(pos, _rope_freq(DSA_HD, 128), _rope_freq(DSA_IDX_DIM, 128))


def _rope(x, cos, sin, head_dim):
    width = x.shape[-1]
    half = head_dim // ROPE_FRAC // 2
    lane = lax.broadcasted_iota(jnp.int32, (1, width), 1)
    per = lane % head_dim
    sgn_lo = jnp.where(per < half, -1.0, 0.0)
    sgn_hi = jnp.where((per >= half) & (per < 2 * half), 1.0, 0.0)
    up = pltpu.roll(x, width - half, axis=1)
    dn = pltpu.roll(x, half, axis=1)
    return x * cos + sin * (up * sgn_lo + dn * sgn_hi)


def _dsa_prep_kernel(dq_ref, dk_ref, dv_ref, diq_ref, misc_ref, c128_ref, s128_ref, c64_ref, s64_ref,
                     qg_ref, kg_ref, qt_out, k_out, vt_out, iqt_out, ik_out, wt_out):
    c128, s128 = c128_ref[...], s128_ref[...]
    c64, s64 = c64_ref[...], s64_ref[...]
    for h in range(DSA_HEADS):
        sl = slice(h * DSA_HD, (h + 1) * DSA_HD)
        q = _rope(_rms(dq_ref[:, sl].astype(F32), qg_ref[...]), c128, s128, DSA_HD)
        qt_out[h] = (q * (DSA_HD ** -0.5)).T.astype(qt_out.dtype)
    k = _rope(_rms(dk_ref[...].astype(F32), kg_ref[...]), c128, s128, DSA_HD)
    k_out[...] = k.astype(k_out.dtype)
    for c in range(vt_out.shape[0]):
        vt_out[c] = dv_ref[c * DSA_KCHUNK:(c + 1) * DSA_KCHUNK, :].astype(F32).T.astype(vt_out.dtype)
    for j in range(DSA_IDX_HEADS // 2):
        sl = slice(j * 128, (j + 1) * 128)
        xt = _rope(diq_ref[:, sl].astype(F32), c64, s64, DSA_IDX_DIM).T
        iqt_out[2 * j] = xt[:DSA_IDX_DIM].astype(iqt_out.dtype)
        iqt_out[2 * j + 1] = xt[DSA_IDX_DIM:].astype(iqt_out.dtype)
    misc = misc_ref[...].astype(F32)
    ik_out[...] = _rope(misc, c64, s64, DSA_IDX_DIM)[:, :DSA_IDX_DIM].astype(ik_out.dtype)
    wt_out[...] = misc.T[DIW_LANE:DIW_LANE + DSA_IDX_HEADS]


def _dsa_prep(p, tables, q_gain, k_gain):
    T = p.shape[0]
    tq = min(512, T)
    assert tq % DSA_KCHUNK == 0
    tab = pl.BlockSpec((tq, 128), lambda i: (i, 0))
    gain = pl.BlockSpec((1, 128), lambda i: (0, 0))
    return pl.pallas_call(
        _dsa_prep_kernel,
        grid=(T // tq,),
        in_specs=[pl.BlockSpec((tq, 512), lambda i: (i, _COL["dq"] // 512)),
                  pl.BlockSpec((tq, 128), lambda i: (i, _COL["dk"] // 128)),
                  pl.BlockSpec((tq, 128), lambda i: (i, _COL["dv"] // 128)),
                  pl.BlockSpec((tq, 1024), lambda i: (i, _COL["diq"] // 1024)),
                  pl.BlockSpec((tq, 128), lambda i: (i, MISC_COL // 128)),
                  tab, tab, tab, tab, gain, gain],
        out_specs=[pl.BlockSpec((DSA_HEADS, DSA_HD, tq), lambda i: (0, 0, i)),
                   pl.BlockSpec((tq, DSA_HD), lambda i: (i, 0)),
                   pl.BlockSpec((tq // DSA_KCHUNK, DSA_HD, DSA_KCHUNK), lambda i: (i, 0, 0)),
                   pl.BlockSpec((DSA_IDX_HEADS, DSA_IDX_DIM, tq), lambda i: (0, 0, i)),
                   pl.BlockSpec((tq, DSA_IDX_DIM), lambda i: (i, 0)),
                   pl.BlockSpec((DSA_IDX_HEADS, tq), lambda i: (0, i))],
        out_shape=[jax.ShapeDtypeStruct((DSA_HEADS, DSA_HD, T), BF16),
                   jax.ShapeDtypeStruct((T, DSA_HD), BF16),
                   jax.ShapeDtypeStruct((T // DSA_KCHUNK, DSA_HD, DSA_KCHUNK), BF16),
                   jax.ShapeDtypeStruct((DSA_IDX_HEADS, DSA_IDX_DIM, T), BF16),
                   jax.ShapeDtypeStruct((T, DSA_IDX_DIM), BF16),
                   jax.ShapeDtypeStruct((DSA_IDX_HEADS, T), F32)],
        compiler_params=_cparams("arbitrary"),
        name="dsa_prep",
    )(p, p, p, p, p, *tables, q_gain, k_gain)


def _dsa_kernel(iqt_ref, wt_ref, qt_ref, ik_ref, k_ref, vt_ref, o_ref,
                key_ref, half_ref, bias_ref, s_ref, p_ref, acc_ref, *, ksel):
    TQ, CK = DSA_QBLOCK, DSA_KCHUNK
    HK = CK // 2
    i = pl.program_id(1)
    nk = (i * TQ + TQ + CK - 1) // CK
    t_idx = i * TQ + lax.broadcasted_iota(jnp.int32, (1, TQ), 1)
    w_all = wt_ref[...] * (DSA_IDX_HEADS ** -0.5 * DSA_IDX_DIM ** -0.5)

    def score_chunk(kc, carry):
        for half in range(2):
            lhs = ik_ref[0, kc, half * HK:(half + 1) * HK, :]
            acc = None
            for h in range(DSA_IDX_HEADS):
                term = w_all[h:h + 1] * jnp.maximum(_dot(lhs, iqt_ref[h]), 0.0)
                acc = term if acc is None else acc + term
            s_idx = kc * CK + half * HK + lax.broadcasted_iota(jnp.int32, (HK, 1), 0)
            bits = lax.bitcast_convert_type(acc, jnp.int32)
            key = bits ^ ((bits >> 31) & 0x7FFFFFFF)
            key = jnp.where(s_idx <= t_idx, key, INT_MIN)
            key_ref[kc, half * HK:(half + 1) * HK, :] = key
            half_ref[kc, half * HK:(half + 1) * HK, :] = (key >> 16).astype(jnp.int16)
        return carry

    lax.fori_loop(0, nk, score_chunk, 0)
    npair = (nk + 1) // 2

    @pl.when(nk % 2 == 1)
    def _():
        key_ref[nk] = jnp.full((CK, TQ), INT_MIN, jnp.int32)
        half_ref[nk] = jnp.full((CK, TQ), I16_MIN, jnp.int16)

    chains = 4
    groups = 2 * CK // (16 * chains)

    def count_ge(cand):
        cb = jnp.broadcast_to(cand.astype(jnp.int16), (16, TQ))[None, None]

        def count(j, acc):
            keys = half_ref[pl.ds(2 * j, 2)].reshape(groups, chains, 16, TQ)
            for g in range(groups):
                acc = acc + jnp.where(keys[g:g + 1] >= cb, jnp.int16(1), jnp.int16(0))[0]
            return acc

        acc = lax.fori_loop(0, npair, count, jnp.zeros((chains, 16, TQ), jnp.int16))
        return jnp.sum(jnp.sum(acc.astype(jnp.int32), axis=0), axis=0, keepdims=True)

    def select16(need):
        def bit_step(bi, thr):
            cand = thr + jnp.left_shift(jnp.int32(1), 15 - bi)
            return jnp.where(count_ge(cand) >= need, cand, thr)

        return lax.fori_loop(0, 16, bit_step, jnp.full((1, TQ), I16_MIN, jnp.int32))

    thr_hi = select16(ksel)
    above = jnp.where(thr_hi >= -I16_MIN - 1, 0, count_ge(jnp.minimum(thr_hi + 1, -I16_MIN - 1)))

    def low_half(j, carry):
        keys = key_ref[pl.ds(2 * j, 2)]
        low = (keys & 0xFFFF) + I16_MIN
        half_ref[pl.ds(2 * j, 2)] = jnp.where((keys >> 16) == thr_hi, low, I16_MIN).astype(jnp.int16)
        return carry

    lax.fori_loop(0, npair, low_half, 0)
    thr_lo = select16(ksel - above)
    thr = thr_hi * 65536 + (thr_lo - I16_MIN)
    thr = jnp.maximum(thr, INT_MIN + 1)

    kg = CK // 16
    unit = 64

    def logits(kc, buf, m_run):
        bias_ref[...] = jnp.where(key_ref[kc] >= thr, 0.0, -1e30)
        kk = k_ref[0, kc]
        new = []
        for h in range(DSA_HEADS):
            s = _dot(kk, qt_ref[h]) + bias_ref[...]
            s_ref[buf, h] = s
            mx = jnp.max(s.reshape(kg, 2, 8, TQ), axis=0)
            new.append(jnp.maximum(m_run[h], jnp.max(jnp.max(mx, axis=0), axis=0, keepdims=True)))
        return tuple(new)

    def attn_chunk(kc, carry):
        m_old, m_run, ls = carry
        cur = kc % 2
        vt = vt_ref[0, kc]
        new_l = []
        for h in range(DSA_HEADS):
            alpha = jnp.exp(m_old[h] - m_run[h])
            mb = jnp.broadcast_to(m_run[h], (8, TQ))[None]
            part = ls[h] * alpha
            for u in range(CK // unit):
                rows = slice(u * unit, (u + 1) * unit)
                pr = jnp.exp(s_ref[cur, h, rows, :].reshape(unit // 8, 8, TQ) - mb)
                part = part + jnp.sum(pr.reshape(unit // 16, 2, 8, TQ), axis=0)
                p_ref[h, rows, :] = pr.reshape(unit, TQ).astype(BF16)
            new_l.append(part)
            acc_ref[h] = acc_ref[h] * alpha + _dot(vt, p_ref[h])
        m_next = logits(jnp.minimum(kc + 1, nk - 1), 1 - cur, m_run)
        return m_run, m_next, tuple(new_l)

    acc_ref[...] = jnp.zeros_like(acc_ref)
    floor = tuple(jnp.full((1, TQ), -1e30, F32) for _ in range(DSA_HEADS))
    init = (floor, logits(0, 0, floor), tuple(jnp.zeros((2, 8, TQ), F32) for _ in range(DSA_HEADS)))
    _, _, ls = lax.fori_loop(0, nk, attn_chunk, init)
    for h in range(DSA_HEADS):
        denom = jnp.sum(jnp.sum(ls[h], axis=0), axis=0, keepdims=True)
        o_ref[:, h * DSA_HD:(h + 1) * DSA_HD] = (acc_ref[h] / denom).T.astype(o_ref.dtype)


def _dsa(qt, k_rot, vt, iqt, ik_rot, wt, B, S):
    TQ, CK = DSA_QBLOCK, DSA_KCHUNK
    NC = S // CK
    nq = S // TQ
    assert NC % 2 == 0 and TQ % CK == 0
    ksel = min(DSA_TOPK, S // 4)
    ik = ik_rot.reshape(B, NC, CK, DSA_IDX_DIM)
    k = k_rot.reshape(B, NC, CK, DSA_HD)
    vt = vt.reshape(B, NC, DSA_HD, CK)
    return pl.pallas_call(
        functools.partial(_dsa_kernel, ksel=ksel),
        grid=(B, nq),
        in_specs=[pl.BlockSpec((DSA_IDX_HEADS, DSA_IDX_DIM, TQ), lambda b, i: (0, 0, b * nq + i)),
                  pl.BlockSpec((DSA_IDX_HEADS, TQ), lambda b, i: (0, b * nq + i)),
                  pl.BlockSpec((DSA_HEADS, DSA_HD, TQ), lambda b, i: (0, 0, b * nq + i)),
                  pl.BlockSpec((1, NC, CK, DSA_IDX_DIM), lambda b, i: (b, 0, 0, 0)),
                  pl.BlockSpec((1, NC, CK, DSA_HD), lambda b, i: (b, 0, 0, 0)),
                  pl.BlockSpec((1, NC, DSA_HD, CK), lambda b, i: (b, 0, 0, 0))],
        out_specs=pl.BlockSpec((TQ, DSA_HEADS * DSA_HD), lambda b, i: (b * nq + i, 0)),
        out_shape=jax.ShapeDtypeStruct((B * S, DSA_HEADS * DSA_HD), BF16),
        scratch_shapes=[pltpu.VMEM((NC, CK, TQ), jnp.int32),
                        pltpu.VMEM((NC, CK, TQ), jnp.int16),
                        pltpu.VMEM((CK, TQ), F32),
                        pltpu.VMEM((2, DSA_HEADS, CK, TQ), F32),
                        pltpu.VMEM((DSA_HEADS, CK, TQ), BF16),
                        pltpu.VMEM((DSA_HEADS, DSA_HD, TQ), F32)],
        compiler_params=_cparams("arbitrary", "arbitrary"),
        name="dsa_attn",
    )(iqt, wt, qt, ik, k, vt)


def _mem_kv_kernel(mem_ref, g_ref, w_ref, kg_ref, k_out, v_out):
    h = _rms(mem_ref[...], g_ref[...]).astype(BF16)
    kv = _dot(h, w_ref[...])
    width = MEM_HEADS * MEM_HD
    for hd in range(MEM_HEADS):
        sl = slice(hd * MEM_HD, (hd + 1) * MEM_HD)
        k_out[:, sl] = _rms(kv[:, sl], kg_ref[...]).astype(k_out.dtype)
    v_out[...] = kv[:, width:].astype(v_out.dtype)


def _mem_kv(mem2d, gain, w_kv, k_gain, B, l):
    width = MEM_HEADS * MEM_HD
    out = jax.ShapeDtypeStruct((B * MEM_LEN, width), BF16)
    return pl.pallas_call(
        _mem_kv_kernel,
        grid=(B,),
        in_specs=[pl.BlockSpec((MEM_LEN, D_MODEL), lambda b: (b, 0)),
                  pl.BlockSpec((1, D_MODEL), lambda b: (0, 0)),
                  _layer((D_MODEL, 2 * width), l),
                  pl.BlockSpec((1, MEM_HD), lambda b: (0, 0))],
        out_specs=[pl.BlockSpec((MEM_LEN, width), lambda b: (b, 0))] * 2,
        out_shape=[out, out],
        compiler_params=_cparams("arbitrary"),
        name="mem_kv",
    )(mem2d, gain, w_kv, k_gain)


def _mem_attn_kernel(q_ref, k_ref, v_ref, qg_ref, o_ref):
    for h in range(MEM_HEADS):
        sl = slice(h * MEM_HD, (h + 1) * MEM_HD)
        q = _rms(q_ref[:, sl].astype(F32), qg_ref[...]) * (MEM_HD ** -0.5)
        s = _dot_nt(q.astype(BF16), k_ref[:, sl])
        pr = jnp.exp(s - jnp.max(s, axis=1, keepdims=True))
        o = _dot(pr.astype(BF16), v_ref[:, sl]) / jnp.sum(pr, axis=1, keepdims=True)
        o_ref[:, sl] = o.astype(o_ref.dtype)


def _mem_attn(p, mk, mv, q_gain, B, S):
    tq = min(512, S)
    nq = S // tq
    width = MEM_HEADS * MEM_HD
    return pl.pallas_call(
        _mem_attn_kernel,
        grid=(B, nq),
        in_specs=[pl.BlockSpec((tq, width), lambda b, i: (b * nq + i, _COL["mq"] // width)),
                  pl.BlockSpec((MEM_LEN, width), lambda b, i: (b, 0)),
                  pl.BlockSpec((MEM_LEN, width), lambda b, i: (b, 0)),
                  pl.BlockSpec((1, MEM_HD), lambda b, i: (0, 0))],
        out_specs=pl.BlockSpec((tq, width), lambda b, i: (b * nq + i, 0)),
        out_shape=jax.ShapeDtypeStruct((B * S, width), BF16),
        compiler_params=_cparams("arbitrary", "arbitrary"),
        name="mem_attn",
    )(p, mk, mv, q_gain)


def _out_proj_kernel(og_ref, ys_ref, od_ref, om_ref, z_ref, x_ref, gn_ref, wglu_ref, w_ref, o_ref, lhs_ref):
    y = jax.nn.gelu(jnp.concatenate([ys_ref[q] for q in range(ys_ref.shape[0])], axis=1))
    o_s5 = y * jax.nn.sigmoid(_dot(y.astype(BF16), wglu_ref[...]))
    branches = (og_ref[...].astype(F32), o_s5, od_ref[...].astype(F32), om_ref[...].astype(F32))
    for bi, o in enumerate(branches):
        for h in range(o.shape[1] // HEAD):
            c0 = bi * 512 + h * HEAD
            z = z_ref[:, c0:c0 + HEAD].astype(F32)
            n = _rms(o[:, h * HEAD:(h + 1) * HEAD], gn_ref[:, c0:c0 + HEAD])
            lhs_ref[:, c0:c0 + HEAD] = (n * (z * jax.nn.sigmoid(z))).astype(BF16)
    o_ref[...] = x_ref[...] + _dot(lhs_ref[...], w_ref[...])


def _out_proj(o_gla, y_s5, o_dsa, o_mem, p, x2d, out_gain, w_glu, w_out, l):
    T = x2d.shape[0]
    tm = min(512, T)
    branch = pl.BlockSpec((tm, 512), lambda i: (i, 0))
    return pl.pallas_call(
        _out_proj_kernel,
        grid=(T // tm,),
        in_specs=[branch, pl.BlockSpec((S5_WIDTH // 128, tm, 128), lambda i: (0, i, 0)), branch, branch,
                  pl.BlockSpec((tm, MIX_WIDTH), lambda i: (i, 0)),
                  pl.BlockSpec((tm, D_MODEL), lambda i: (i, 0)),
                  _resident((1, MIX_WIDTH)),
                  _layer((S5_WIDTH, S5_WIDTH), l),
                  _layer((MIX_WIDTH, D_MODEL), l)],
        out_specs=pl.BlockSpec((tm, D_MODEL), lambda i: (i, 0)),
        out_shape=jax.ShapeDtypeStruct((T, D_MODEL), F32),
        scratch_shapes=[pltpu.VMEM((tm, MIX_WIDTH), BF16)],
        compiler_params=_cparams("arbitrary"),
        name="out_proj",
    )(o_gla, y_s5, o_dsa, o_mem, p, x2d, out_gain, w_glu, w_out)


def _permute_kernel(w_ref, o_ref):
    c = 0
    for n in _ORDER:
        lo, hi = _ORIG[n]
        o_ref[:, c:c + hi - lo] = w_ref[:, lo:hi].astype(o_ref.dtype)
        c += hi - lo
    o_ref[:, c:] = jnp.zeros((o_ref.shape[0], IN_PAD - c), o_ref.dtype)


def _permute_w_in(w_in):
    depth, rows, width = w_in.shape
    tr = 256
    return pl.pallas_call(
        _permute_kernel,
        grid=(depth, rows // tr),
        in_specs=[pl.BlockSpec((None, tr, width), lambda l, i: (l, i, 0))],
        out_specs=pl.BlockSpec((None, tr, IN_PAD), lambda l, i: (l, i, 0)),
        out_shape=jax.ShapeDtypeStruct((depth, rows, IN_PAD), BF16),
        compiler_params=_cparams("arbitrary", "arbitrary"),
        name="permute_w_in",
    )(w_in)


def kernel(x, mem, positions, ln_gain, w_in, gla_w_up, gla_b_up, s5_a_re, s5_a_im, s5_log_step, s5_b_re, s5_b_im, s5_c_re, s5_c_im, s5_d, s5_w_glu, dsa_q_norm, dsa_k_norm, mem_norm, w_mem_kv, mem_q_norm, mem_k_norm, out_norm, w_out):
    B, S, _ = x.shape
    depth = w_in.shape[0]
    T = B * S
    x2d = x.reshape(T, D_MODEL)
    mem2d = mem.reshape(B * mem.shape[1], D_MODEL)
    w_in_p = _permute_w_in(w_in)
    wup_pad = jnp.zeros((depth, 128, GLA_HEADS * GLA_DK), F32)
    wup_pad = wup_pad.at[:, GLR_LANE:GLR_LANE + GLA_GATE_RANK].set(gla_w_up).astype(BF16)
    w_kv = w_mem_kv.astype(BF16)
    w_glu = s5_w_glu.astype(BF16)
    w_o = w_out.astype(BF16)
    tables = _rope_tables(positions)
    for l in range(depth):
        p, su = _in_proj(x2d, ln_gain[l][None], w_in_p, l)
        o_gla = _gla(p, wup_pad[l], gla_b_up[l][None], B, S)
        mats = _s5_mats(s5_a_re[l], s5_a_im[l], s5_log_step[l], s5_b_re[l], s5_b_im[l],
                        s5_c_re[l], s5_c_im[l], s5_d[l])
        y_s5 = _s5(su, mats, B, S)
        qt, k_rot, vt, iqt, ik_rot, wt = _dsa_prep(p, tables, dsa_q_norm[l][None], dsa_k_norm[l][None])
        o_dsa = _dsa(qt, k_rot, vt, iqt, ik_rot, wt, B, S)
        mk, mv = _mem_kv(mem2d, mem_norm[l][None], w_kv, mem_k_norm[l][None], B, l)
        o_mem = _mem_attn(p, mk, mv, mem_q_norm[l][None], B, S)
        x2d = _out_proj(o_gla, y_s5, o_dsa, o_mem, p, x2d, out_norm[l][None], w_glu, w_o, l)
    return x2d.reshape(B, S, D_MODEL)
```

```python
import functools

import numpy as np
import jax
import jax.numpy as jnp
from jax import lax
from jax.experimental import pallas as pl
from jax.experimental.pallas import tpu as pltpu

F32 = jnp.float32
BF16 = jnp.bfloat16

D_MODEL = 2048
MEM_LEN = 256
ROPE_THETA = 500000.0
ROPE_FRAC = 4
NORM_EPS = 1e-6

GLA_HEADS = 4
GLA_DK = 64
GLA_DV = 128
GLA_GATE_RANK = 16
GLA_TAU = 16.0
GLA_CHUNK = 64

S5_GROUPS = 32
S5_CH = 16
S5_STATE = 64
S5_WIDTH = S5_GROUPS * S5_CH
S5_BLOCK = 16

DSA_HEADS = 4
DSA_HD = 128
DSA_IDX_HEADS = 16
DSA_IDX_DIM = 64
DSA_TOPK = 256
DSA_QBLOCK = 256
DSA_KCHUNK = 256

MEM_HEADS = 4
MEM_HD = 128
HEAD = 128
MIX_WIDTH = 2048

VMEM_LIMIT_BYTES = 52 * 1024 * 1024

_ORIG = dict(gq=(0, 256), gk=(256, 512), gv=(512, 1024), glr=(1024, 1040), gz=(1040, 1552),
             su=(1552, 2064), sz=(2064, 2576), dq=(2576, 3088), dk=(3088, 3216), dv=(3216, 3344),
             diq=(3344, 4368), dik=(4368, 4432), diw=(4432, 4448), dz=(4448, 4960),
             mq=(4960, 5472), mz=(5472, 5984))
_ORDER = ("gz", "sz", "dz", "mz", "diq", "gv", "su", "dq", "mq", "gq", "gk", "dk", "dv",
          "dik", "glr", "diw")
_COL = {}
_c = 0
for _n in _ORDER:
    _COL[_n] = _c
    _c += _ORIG[_n][1] - _ORIG[_n][0]
IN_PAD = 6144
MISC_COL = _COL["dik"]
GLR_LANE = 64
DIW_LANE = 80
INT_MIN = -(2 ** 31)
I16_MIN = -(2 ** 15)
LOG2E = 1.4426950408889634


def _cparams(*sem):
    return pltpu.CompilerParams(dimension_semantics=sem, vmem_limit_bytes=VMEM_LIMIT_BYTES)


def _dot(a, b):
    return jnp.dot(a, b, preferred_element_type=F32)


def _dot_nt(a, b):
    return lax.dot_general(a, b, (((1,), (1,)), ((), ())), preferred_element_type=F32)


def _rms(x, gain):
    return x * lax.rsqrt(jnp.mean(x * x, axis=-1, keepdims=True) + NORM_EPS) * gain


def _in_proj_kernel(x_ref, g_ref, w_ref, o_ref, su_ref, h_ref):
    h_ref[...] = _rms(x_ref[...], g_ref[...]).astype(BF16)
    tn = S5_WIDTH
    for j in range(IN_PAD // tn):
        acc = _dot_nt(h_ref[...], w_ref[j * tn:(j + 1) * tn, :])
        o_ref[:, j * tn:(j + 1) * tn] = acc.astype(o_ref.dtype)
        if j == _COL["su"] // tn:
            for q in range(tn // 128):
                su_ref[q] = acc[:, q * 128:(q + 1) * 128]


def _resident(shape):
    return pl.BlockSpec(shape, lambda *_: (0,) * len(shape), pipeline_mode=pl.Buffered(1))


def _layer(shape, l):
    return pl.BlockSpec((None,) + shape, lambda *_: (l,) + (0,) * len(shape), pipeline_mode=pl.Buffered(1))


def _in_proj(x2d, gain, w, l):
    T = x2d.shape[0]
    tm = min(256, T)
    return pl.pallas_call(
        _in_proj_kernel,
        grid=(T // tm,),
        in_specs=[pl.BlockSpec((tm, D_MODEL), lambda i: (i, 0)),
                  _resident((1, D_MODEL)),
                  _layer((IN_PAD, D_MODEL), l)],
        out_specs=[pl.BlockSpec((tm, IN_PAD), lambda i: (i, 0)),
                   pl.BlockSpec((S5_WIDTH // 128, tm, 128), lambda i: (0, i, 0))],
        out_shape=[jax.ShapeDtypeStruct((T, IN_PAD), BF16),
                   jax.ShapeDtypeStruct((S5_WIDTH // 128, T, 128), F32)],
        scratch_shapes=[pltpu.VMEM((tm, D_MODEL), BF16)],
        compiler_params=_cparams("arbitrary"),
        name="in_proj",
    )(x2d, gain, w)


def _gla_kernel(q_ref, k_ref, v_ref, misc_ref, wup_ref, bup_ref, o_ref, st_ref):
    @pl.when(pl.program_id(1) == 0)
    def _():
        st_ref[...] = jnp.zeros_like(st_ref)

    C = GLA_CHUNK
    tc = q_ref.shape[0]
    nchunk = tc // C
    x = _dot(misc_ref[...], wup_ref[...]) + bup_ref[...]
    log_a = (jnp.minimum(x, 0.0) - jnp.log(1.0 + jnp.exp(-jnp.abs(x)))) * (1.0 / GLA_TAU)
    row = lax.broadcasted_iota(jnp.int32, (tc, tc), 0)
    col = lax.broadcasted_iota(jnp.int32, (tc, tc), 1)
    tril = (row >= col) & ((row // C) == (col // C))
    la_hi = log_a.astype(BF16)
    r1 = log_a - la_hi.astype(F32)
    la_mid = r1.astype(BF16)
    la_lo = (r1 - la_mid.astype(F32)).astype(BF16)
    tri3 = jnp.concatenate([tril.astype(BF16)] * 3, axis=1)
    b = _dot(tri3, jnp.concatenate([la_hi, la_mid, la_lo], axis=0))
    lane = lax.broadcasted_iota(jnp.int32, (1, 2 * GLA_DK), 1)
    low = lane < GLA_DK
    srow = lax.broadcasted_iota(jnp.int32, (2 * GLA_DV, 1), 0)
    diag = (srow < GLA_DV) == low
    for j in range(GLA_HEADS // 2):
        ls = slice(j * 128, (j + 1) * 128)
        bp = b[:, ls]
        mid = jnp.concatenate([jnp.broadcast_to(bp[c * C + C // 2:c * C + C // 2 + 1], (C, 128))
                               for c in range(nchunk)], axis=0)
        last = jnp.concatenate([jnp.broadcast_to(bp[c * C + C - 1:c * C + C], (C, 128))
                                for c in range(nchunk)], axis=0)
        qf = q_ref[:, ls].astype(F32) * (GLA_DK ** -0.5)
        kf = k_ref[:, ls].astype(F32)
        qe = qf * jnp.exp(bp - mid)
        ke_p = (kf * jnp.exp(mid - bp)).astype(BF16)
        kd_p = (kf * jnp.exp(last - bp)).astype(BF16)
        qi_p = (qf * jnp.exp(bp)).astype(BF16)
        intra = []
        for hh in range(2):
            h = 2 * j + hh
            sel = low if hh == 0 else jnp.logical_not(low)
            qe_h = jnp.where(sel, qe, 0.0).astype(BF16)
            attn = jnp.where(tril, _dot_nt(qe_h, ke_p), 0.0)
            intra.append(_dot(attn.astype(BF16), v_ref[:, h * GLA_DV:(h + 1) * GLA_DV]))
        intra = jnp.concatenate(intra, axis=1)
        vs = slice(2 * j * GLA_DV, (2 * j + 2) * GLA_DV)
        st = st_ref[j]
        for c in range(nchunk):
            rs = slice(c * C, (c + 1) * C)
            o = intra[rs] + _dot_nt(qi_p[rs], st.astype(BF16))
            o_ref[rs, vs] = o.astype(o_ref.dtype)
            v_t = v_ref[rs, vs].astype(F32).T.astype(BF16)
            dec = jnp.exp(bp[c * C + C - 1:c * C + C])
            st = jnp.where(diag, st * dec + _dot(v_t, kd_p[rs]), 0.0)
        st_ref[j] = st


def _gla(p, wup_pad, bup, B, S):
    tc = min(256, S)
    nc = S // tc
    row = lambda b, c: b * nc + c
    return pl.pallas_call(
        _gla_kernel,
        grid=(B, nc),
        in_specs=[pl.BlockSpec((tc, 256), lambda b, c: (row(b, c), _COL["gq"] // 256)),
                  pl.BlockSpec((tc, 256), lambda b, c: (row(b, c), _COL["gk"] // 256)),
                  pl.BlockSpec((tc, 512), lambda b, c: (row(b, c), _COL["gv"] // 512)),
                  pl.BlockSpec((tc, 128), lambda b, c: (row(b, c), MISC_COL // 128)),
                  _resident((128, 256)),
                  _resident((1, 256))],
        out_specs=pl.BlockSpec((tc, 512), lambda b, c: (row(b, c), 0)),
        out_shape=jax.ShapeDtypeStruct((B * S, GLA_HEADS * GLA_DV), BF16),
        scratch_shapes=[pltpu.VMEM((GLA_HEADS // 2, 2 * GLA_DV, 2 * GLA_DK), F32)],
        compiler_params=_cparams("arbitrary", "arbitrary"),
        name="gla",
    )(p, p, p, p, wup_pad, bup)


def _s5_mats(a_re, a_im, log_step, b_re, b_im, c_re, c_im, d_skip):
    L, P, CH, G = S5_BLOCK, S5_STATE, S5_CH, S5_GROUPS
    hi = lax.Precision.HIGHEST
    step = jnp.exp(log_step.astype(F32))[:, None]
    are, aim = a_re.astype(F32), a_im.astype(F32)

    def power(j):
        mag = jnp.exp(are[:, None, :] * step[:, None, :] * j[None, :, None])
        ang = aim[:, None, :] * step[:, None, :] * j[None, :, None]
        return mag * jnp.cos(ang), mag * jnp.sin(ang)

    t = jnp.arange(L, dtype=F32)
    lb_re, lb_im = (v[:, 0] for v in power(jnp.ones((1,), F32)))
    den = are * are + aim * aim
    f_re = ((lb_re - 1.0) * are + lb_im * aim) / den
    f_im = (lb_im * are - (lb_re - 1.0) * aim) / den
    bt_re, bt_im = jnp.swapaxes(b_re, 1, 2), jnp.swapaxes(b_im, 1, 2)
    bb_re = f_re[:, None] * bt_re - f_im[:, None] * bt_im
    bb_im = f_re[:, None] * bt_im + f_im[:, None] * bt_re
    n_re, n_im = power(-t)
    a_r = (n_re[:, :, None] * bb_re[:, None] - n_im[:, :, None] * bb_im[:, None]).reshape(G, L * CH, P)
    a_i = (n_re[:, :, None] * bb_im[:, None] + n_im[:, :, None] * bb_re[:, None]).reshape(G, L * CH, P)
    p_re, p_im = power(t)
    ct_re, ct_im = jnp.swapaxes(c_re, 1, 2), jnp.swapaxes(c_im, 1, 2)
    pt_re, pt_im = jnp.swapaxes(p_re, 1, 2), jnp.swapaxes(p_im, 1, 2)
    b_r = (pt_re[..., None] * ct_re[:, :, None] - pt_im[..., None] * ct_im[:, :, None]).reshape(G, P, L * CH)
    b_i = (pt_re[..., None] * ct_im[:, :, None] + pt_im[..., None] * ct_re[:, :, None]).reshape(G, P, L * CH)
    m = (jnp.einsum("gxp,gpy->gxy", a_r, b_r, precision=hi) - jnp.einsum("gxp,gpy->gxy", a_i, b_i, precision=hi))
    blk = np.arange(L * CH) // CH
    m = jnp.where((blk[None, :] >= blk[:, None])[None], m, 0.0)
    m = m + jnp.eye(L * CH, dtype=F32)[None] * jnp.tile(d_skip.reshape(G, 1, CH), (1, 1, L))
    e_re, e_im = (v[:, 0][:, None] for v in power(jnp.full((1,), L - 1.0, F32)))
    pr, pi = a_r * e_re - a_i * e_im, a_r * e_im + a_i * e_re
    pm = jnp.concatenate([pr, pi, pi, pr], axis=-1)
    l_re, l_im = lb_re[..., None], lb_im[..., None]
    qm = jnp.concatenate([b_r * l_re - b_i * l_im, -(b_r * l_im + b_i * l_re)], axis=1)
    al_re, al_im = (v[:, 0] for v in power(jnp.full((1,), float(L), F32)))
    ar = jnp.concatenate([al_re, al_re], axis=-1).reshape(1, G * 2 * P)
    ai_a = jnp.concatenate([-al_im, al_im], axis=-1).reshape(1, G * 2 * P)
    ai_b = jnp.concatenate([al_im, -al_im], axis=-1).reshape(1, G * 2 * P)
    return m.astype(BF16), pm.astype(BF16), qm.astype(BF16), ar, ai_a, ai_b


def _s5_perm():
    src = np.arange(1024).reshape(8, 8, S5_CH).transpose(1, 0, 2).reshape(-1)
    m = np.zeros((1024, 1024), np.float32)
    m[np.arange(1024), src] = 1.0
    return jnp.asarray(m, BF16)


def _s5_kernel(su_ref, perm_ref, m_ref, p_ref, q_ref, ar_ref, aia_ref, aib_ref, y_ref,
               xcat_ref, u_ref, va_ref, vb_ref, x_ref, ycat_ref):
    L = S5_BLOCK
    NB = su_ref.shape[1] // L
    GH = 8
    W = 2 * S5_STATE
    for s in range(L):
        xcat_ref[:, s * 128:(s + 1) * 128] = su_ref[0, pl.ds(s, NB, stride=L), :].astype(BF16)
    for half in range(2):
        u_ref[half] = _dot(xcat_ref[:, half * 1024:(half + 1) * 1024], perm_ref[...]).astype(BF16)

    def u_of(g):
        return jnp.concatenate([u_ref[0, :, g * 128:(g + 1) * 128], u_ref[1, :, g * 128:(g + 1) * 128]], axis=1)

    for g in range(GH):
        v2 = _dot(u_of(g), p_ref[g])
        va_ref[:, g * W:(g + 1) * W] = v2[:, :W]
        vb_ref[:, g * W:(g + 1) * W] = v2[:, W:]
    ar, aia, aib = ar_ref[...], aia_ref[...], aib_ref[...]

    def step(n, carry):
        xa, xb = carry
        x_ref[pl.ds(n, 1), :] = xa
        na = xa * ar + xb * aia + va_ref[pl.ds(n, 1), :]
        nb = xb * ar + xa * aib + vb_ref[pl.ds(n, 1), :]
        return na, nb

    z = jnp.zeros((1, GH * W), F32)
    lax.fori_loop(0, NB, step, (z, z), unroll=8)
    for g in range(GH):
        y = _dot(u_of(g), m_ref[g]) + _dot(x_ref[:, g * W:(g + 1) * W].astype(BF16), q_ref[g])
        for half in range(2):
            ycat_ref[half, :, g * 128:(g + 1) * 128] = y[:, half * 128:(half + 1) * 128].astype(BF16)
    for half in range(2):
        out = _dot(ycat_ref[half], perm_ref[...])
        for t in range(8):
            y_ref[0, pl.ds(half * 8 + t, NB, stride=L), :] = out[:, t * 128:(t + 1) * 128]


def _s5(su, mats, B, S):
    m, pm, qm, ar, aia, aib = mats
    L, G, CH = S5_BLOCK, S5_GROUPS, S5_CH
    NB = S // L
    GH = 8
    W = 2 * S5_STATE
    slab = pl.BlockSpec((1, S, 128), lambda b, h: (h, b, 0))
    coef = pl.BlockSpec((1, GH * W), lambda b, h: (0, h))
    return pl.pallas_call(
        _s5_kernel,
        grid=(B, G // GH),
        in_specs=[slab,
                  pl.BlockSpec((1024, 1024), lambda b, h: (0, 0)),
                  pl.BlockSpec((GH, L * CH, L * CH), lambda b, h: (h, 0, 0)),
                  pl.BlockSpec((GH, L * CH, 2 * W), lambda b, h: (h, 0, 0)),
                  pl.BlockSpec((GH, W, L * CH), lambda b, h: (h, 0, 0)),
                  coef, coef, coef],
        out_specs=slab,
        out_shape=jax.ShapeDtypeStruct((G // GH, B * S, 128), F32),
        scratch_shapes=[pltpu.VMEM((NB, L * 128), BF16), pltpu.VMEM((2, NB, 1024), BF16),
                        pltpu.VMEM((NB, GH * W), F32), pltpu.VMEM((NB, GH * W), F32),
                        pltpu.VMEM((NB, GH * W), F32), pltpu.VMEM((2, NB, 1024), BF16)],
        compiler_params=_cparams("arbitrary", "arbitrary"),
        name="s5_scan",
    )(su, _s5_perm(), m, pm, qm, ar, aia, aib)


def _rope_kernel(pos_ref, f128_ref, f64_ref, c128_ref, s128_ref, c64_ref, s64_ref):
    pos = pos_ref[...]
    a = pos * f128_ref[...]
    c128_ref[...] = jnp.cos(a)
    s128_ref[...] = jnp.sin(a)
    a = pos * f64_ref[...]
    c64_ref[...] = jnp.cos(a)
    s64_ref[...] = jnp.sin(a)


def _rope_freq(d, width):
    rd = d // ROPE_FRAC
    half = rd // 2
    inv = ROPE_THETA ** (-jnp.arange(half, dtype=F32) * 2.0 / rd)
    per_head = jnp.concatenate([inv, inv, jnp.zeros((d - rd,), F32)])
    return jnp.tile(per_head, width // d).reshape(1, width)


def _rope_tables(positions):
    T = positions.size
    pos = positions.astype(F32).reshape(T, 1)
    tq = min(512, T)
    out = jax.ShapeDtypeStruct((T, 128), F32)
    return pl.pallas_call(
        _rope_kernel,
        grid=(T // tq,),
        in_specs=[pl.BlockSpec((tq, 1), lambda i: (i, 0)),
                  pl.BlockSpec((1, 128), lambda i: (0, 0)),
                  pl.BlockSpec((1, 128), lambda i: (0, 0))],
        out_specs=[pl.BlockSpec((tq, 128), lambda i: (i, 0))] * 4,
        out_shape=[out] * 4,
        compiler_params=_cparams("arbitrary"),
        name="rope_tables",
    )(pos, _rope_freq(DSA_HD, 128), _rope_freq(DSA_IDX_DIM, 128))


def _rope(x, cos, sin, head_dim):
    width = x.shape[-1]
    half = head_dim // ROPE_FRAC // 2
    lane = lax.broadcasted_iota(jnp.int32, (1, width), 1)
    per = lane % head_dim
    sgn_lo = jnp.where(per < half, -1.0, 0.0)
    sgn_hi = jnp.where((per >= half) & (per < 2 * half), 1.0, 0.0)
    up = pltpu.roll(x, width - half, axis=1)
    dn = pltpu.roll(x, half, axis=1)
    return x * cos + sin * (up * sgn_lo + dn * sgn_hi)


def _dsa_prep_kernel(dq_ref, dk_ref, dv_ref, diq_ref, misc_ref, c128_ref, s128_ref, c64_ref, s64_ref,
                     qg_ref, kg_ref, qt_out, k_out, vt_out, iqt_out, ik_out, wt_out):
    c128, s128 = c128_ref[...], s128_ref[...]
    c64, s64 = c64_ref[...], s64_ref[...]
    for h in range(DSA_HEADS):
        sl = slice(h * DSA_HD, (h + 1) * DSA_HD)
        q = _rope(_rms(dq_ref[:, sl].astype(F32), qg_ref[...]), c128, s128, DSA_HD)
        qt_out[h] = (q * (DSA_HD ** -0.5 * LOG2E)).T.astype(qt_out.dtype)
    k = _rope(_rms(dk_ref[...].astype(F32), kg_ref[...]), c128, s128, DSA_HD)
    k_out[...] = k.astype(k_out.dtype)
    for c in range(vt_out.shape[0]):
        vt_out[c] = dv_ref[c * DSA_KCHUNK:(c + 1) * DSA_KCHUNK, :].astype(F32).T.astype(vt_out.dtype)
    for j in range(DSA_IDX_HEADS // 2):
        sl = slice(j * 128, (j + 1) * 128)
        xt = _rope(diq_ref[:, sl].astype(F32), c64, s64, DSA_IDX_DIM).T
        iqt_out[2 * j] = xt[:DSA_IDX_DIM].astype(iqt_out.dtype)
        iqt_out[2 * j + 1] = xt[DSA_IDX_DIM:].astype(iqt_out.dtype)
    misc = misc_ref[...].astype(F32)
    ik_out[...] = _rope(misc, c64, s64, DSA_IDX_DIM)[:, :DSA_IDX_DIM].astype(ik_out.dtype)
    wt_out[...] = misc.T[DIW_LANE:DIW_LANE + DSA_IDX_HEADS]


def _dsa_prep(p, tables, q_gain, k_gain):
    T = p.shape[0]
    tq = min(512, T)
    assert tq % DSA_KCHUNK == 0
    tab = pl.BlockSpec((tq, 128), lambda i: (i, 0))
    gain = pl.BlockSpec((1, 128), lambda i: (0, 0))
    return pl.pallas_call(
        _dsa_prep_kernel,
        grid=(T // tq,),
        in_specs=[pl.BlockSpec((tq, 512), lambda i: (i, _COL["dq"] // 512)),
                  pl.BlockSpec((tq, 128), lambda i: (i, _COL["dk"] // 128)),
                  pl.BlockSpec((tq, 128), lambda i: (i, _COL["dv"] // 128)),
                  pl.BlockSpec((tq, 1024), lambda i: (i, _COL["diq"] // 1024)),
                  pl.BlockSpec((tq, 128), lambda i: (i, MISC_COL // 128)),
                  tab, tab, tab, tab, gain, gain],
        out_specs=[pl.BlockSpec((DSA_HEADS, DSA_HD, tq), lambda i: (0, 0, i)),
                   pl.BlockSpec((tq, DSA_HD), lambda i: (i, 0)),
                   pl.BlockSpec((tq // DSA_KCHUNK, DSA_HD, DSA_KCHUNK), lambda i: (i, 0, 0)),
                   pl.BlockSpec((DSA_IDX_HEADS, DSA_IDX_DIM, tq), lambda i: (0, 0, i)),
                   pl.BlockSpec((tq, DSA_IDX_DIM), lambda i: (i, 0)),
                   pl.BlockSpec((DSA_IDX_HEADS, tq), lambda i: (0, i))],
        out_shape=[jax.ShapeDtypeStruct((DSA_HEADS, DSA_HD, T), BF16),
                   jax.ShapeDtypeStruct((T, DSA_HD), BF16),
                   jax.ShapeDtypeStruct((T // DSA_KCHUNK, DSA_HD, DSA_KCHUNK), BF16),
                   jax.ShapeDtypeStruct((DSA_IDX_HEADS, DSA_IDX_DIM, T), BF16),
                   jax.ShapeDtypeStruct((T, DSA_IDX_DIM), BF16),
                   jax.ShapeDtypeStruct((DSA_IDX_HEADS, T), F32)],
        compiler_params=_cparams("arbitrary"),
        name="dsa_prep",
    )(p, p, p, p, p, *tables, q_gain, k_gain)


def _dsa_kernel(iqt_ref, wt_ref, qt_ref, ik_ref, k_ref, vt_ref, o_ref,
                key_ref, half_ref, bias_ref, s_ref, p_ref, acc_ref, *, ksel):
    TQ, CK = DSA_QBLOCK, DSA_KCHUNK
    HK = CK // 2
    i = pl.program_id(1)
    nk = (i * TQ + TQ + CK - 1) // CK
    t_idx = i * TQ + lax.broadcasted_iota(jnp.int32, (1, TQ), 1)
    w_all = wt_ref[...] * (DSA_IDX_HEADS ** -0.5 * DSA_IDX_DIM ** -0.5)

    def score_chunk(kc, carry):
        for half in range(2):
            lhs = ik_ref[0, kc, half * HK:(half + 1) * HK, :]
            acc = None
            for h in range(DSA_IDX_HEADS):
                term = w_all[h:h + 1] * jnp.maximum(_dot(lhs, iqt_ref[h]), 0.0)
                acc = term if acc is None else acc + term
            s_idx = kc * CK + half * HK + lax.broadcasted_iota(jnp.int32, (HK, 1), 0)
            bits = lax.bitcast_convert_type(acc, jnp.int32)
            key = bits ^ ((bits >> 31) & 0x7FFFFFFF)
            key = jnp.where(s_idx <= t_idx, key, INT_MIN)
            key_ref[kc, half * HK:(half + 1) * HK, :] = key
            half_ref[kc, half * HK:(half + 1) * HK, :] = (key >> 16).astype(jnp.int16)
        return carry

    lax.fori_loop(0, nk, score_chunk, 0)
    npair = (nk + 1) // 2

    @pl.when(nk % 2 == 1)
    def _():
        key_ref[nk] = jnp.full((CK, TQ), INT_MIN, jnp.int32)
        half_ref[nk] = jnp.full((CK, TQ), I16_MIN, jnp.int16)

    chains = 4
    groups = 2 * CK // (16 * chains)

    def count_ge(cand):
        cb = jnp.broadcast_to(cand.astype(jnp.int16), (16, TQ))[None, None]

        def count(j, acc):
            keys = half_ref[pl.ds(2 * j, 2)].reshape(groups, chains, 16, TQ)
            for g in range(groups):
                acc = acc + jnp.where(keys[g:g + 1] >= cb, jnp.int16(1), jnp.int16(0))[0]
            return acc

        acc = lax.fori_loop(0, npair, count, jnp.zeros((chains, 16, TQ), jnp.int16))
        return jnp.sum(jnp.sum(acc.astype(jnp.int32), axis=0), axis=0, keepdims=True)

    def select16(need):
        def bit_step(bi, thr):
            cand = thr + jnp.left_shift(jnp.int32(1), 15 - bi)
            return jnp.where(count_ge(cand) >= need, cand, thr)

        return lax.fori_loop(0, 16, bit_step, jnp.full((1, TQ), I16_MIN, jnp.int32))

    thr_hi = select16(ksel)
    above = jnp.where(thr_hi >= -I16_MIN - 1, 0, count_ge(jnp.minimum(thr_hi + 1, -I16_MIN - 1)))

    def low_half(j, carry):
        keys = key_ref[pl.ds(2 * j, 2)]
        low = (keys & 0xFFFF) + I16_MIN
        half_ref[pl.ds(2 * j, 2)] = jnp.where((keys >> 16) == thr_hi, low, I16_MIN).astype(jnp.int16)
        return carry

    lax.fori_loop(0, npair, low_half, 0)
    thr_lo = select16(ksel - above)
    thr = thr_hi * 65536 + (thr_lo - I16_MIN)
    thr = jnp.maximum(thr, INT_MIN + 1)

    kg = CK // 16
    unit = 64

    def logits(kc, buf, m_run):
        bias_ref[...] = jnp.where(key_ref[kc] >= thr, 0.0, -1e30)
        kk = k_ref[0, kc]
        new = []
        for h in range(DSA_HEADS):
            s = _dot(kk, qt_ref[h]) + bias_ref[...]
            s_ref[buf, h] = s
            mx = jnp.max(s.reshape(kg, 2, 8, TQ), axis=0)
            new.append(jnp.maximum(m_run[h], jnp.max(jnp.max(mx, axis=0), axis=0, keepdims=True)))
        return tuple(new)

    def attn_chunk(kc, carry):
        m_old, m_run, ls = carry
        cur = kc % 2
        vt = vt_ref[0, kc]
        new_l = []
        for h in range(DSA_HEADS):
            alpha = jnp.exp2(m_old[h] - m_run[h])
            mb = jnp.broadcast_to(m_run[h], (8, TQ))[None]
            part = ls[h] * alpha
            for u in range(CK // unit):
                rows = slice(u * unit, (u + 1) * unit)
                pr = jnp.exp2(s_ref[cur, h, rows, :].reshape(unit // 8, 8, TQ) - mb)
                part = part + jnp.sum(pr.reshape(unit // 16, 2, 8, TQ), axis=0)
                p_ref[h, rows, :] = pr.reshape(unit, TQ).astype(BF16)
            new_l.append(part)
            acc_ref[h] = acc_ref[h] * alpha + _dot(vt, p_ref[h])
        m_next = logits(jnp.minimum(kc + 1, nk - 1), 1 - cur, m_run)
        return m_run, m_next, tuple(new_l)

    acc_ref[...] = jnp.zeros_like(acc_ref)
    floor = tuple(jnp.full((1, TQ), -1e30, F32) for _ in range(DSA_HEADS))
    init = (floor, logits(0, 0, floor), tuple(jnp.zeros((2, 8, TQ), F32) for _ in range(DSA_HEADS)))
    _, _, ls = lax.fori_loop(0, nk, attn_chunk, init)
    for h in range(DSA_HEADS):
        denom = jnp.sum(jnp.sum(ls[h], axis=0), axis=0, keepdims=True)
        o_ref[:, h * DSA_HD:(h + 1) * DSA_HD] = (acc_ref[h] / denom).T.astype(o_ref.dtype)


def _dsa(qt, k_rot, vt, iqt, ik_rot, wt, B, S):
    TQ, CK = DSA_QBLOCK, DSA_KCHUNK
    NC = S // CK
    nq = S // TQ
    assert NC % 2 == 0 and TQ % CK == 0
    ksel = min(DSA_TOPK, S // 4)
    ik = ik_rot.reshape(B, NC, CK, DSA_IDX_DIM)
    k = k_rot.reshape(B, NC, CK, DSA_HD)
    vt = vt.reshape(B, NC, DSA_HD, CK)
    return pl.pallas_call(
        functools.partial(_dsa_kernel, ksel=ksel),
        grid=(B, nq),
        in_specs=[pl.BlockSpec((DSA_IDX_HEADS, DSA_IDX_DIM, TQ), lambda b, i: (0, 0, b * nq + i)),
                  pl.BlockSpec((DSA_IDX_HEADS, TQ), lambda b, i: (0, b * nq + i)),
                  pl.BlockSpec((DSA_HEADS, DSA_HD, TQ), lambda b, i: (0, 0, b * nq + i)),
                  pl.BlockSpec((1, NC, CK, DSA_IDX_DIM), lambda b, i: (b, 0, 0, 0)),
                  pl.BlockSpec((1, NC, CK, DSA_HD), lambda b, i: (b, 0, 0, 0)),
                  pl.BlockSpec((1, NC, DSA_HD, CK), lambda b, i: (b, 0, 0, 0))],
        out_specs=pl.BlockSpec((TQ, DSA_HEADS * DSA_HD), lambda b, i: (b * nq + i, 0)),
        out_shape=jax.ShapeDtypeStruct((B * S, DSA_HEADS * DSA_HD), BF16),
        scratch_shapes=[pltpu.VMEM((NC, CK, TQ), jnp.int32),
                        pltpu.VMEM((NC, CK, TQ), jnp.int16),
                        pltpu.VMEM((CK, TQ), F32),
                        pltpu.VMEM((2, DSA_HEADS, CK, TQ), F32),
                        pltpu.VMEM((DSA_HEADS, CK, TQ), BF16),
                        pltpu.VMEM((DSA_HEADS, DSA_HD, TQ), F32)],
        compiler_params=_cparams("arbitrary", "arbitrary"),
        name="dsa_attn",
    )(iqt, wt, qt, ik, k, vt)


def _mem_kv_kernel(mem_ref, g_ref, w_ref, kg_ref, k_out, v_out):
    h = _rms(mem_ref[...], g_ref[...]).astype(BF16)
    kv = _dot(h, w_ref[...])
    width = MEM_HEADS * MEM_HD
    for hd in range(MEM_HEADS):
        sl = slice(hd * MEM_HD, (hd + 1) * MEM_HD)
        k_out[:, sl] = _rms(kv[:, sl], kg_ref[...]).astype(k_out.dtype)
    v_out[...] = kv[:, width:].astype(v_out.dtype)


def _mem_kv(mem2d, gain, w_kv, k_gain, B, l):
    width = MEM_HEADS * MEM_HD
    out = jax.ShapeDtypeStruct((B * MEM_LEN, width), BF16)
    return pl.pallas_call(
        _mem_kv_kernel,
        grid=(B,),
        in_specs=[pl.BlockSpec((MEM_LEN, D_MODEL), lambda b: (b, 0)),
                  pl.BlockSpec((1, D_MODEL), lambda b: (0, 0)),
                  _layer((D_MODEL, 2 * width), l),
                  pl.BlockSpec((1, MEM_HD), lambda b: (0, 0))],
        out_specs=[pl.BlockSpec((MEM_LEN, width), lambda b: (b, 0))] * 2,
        out_shape=[out, out],
        compiler_params=_cparams("arbitrary"),
        name="mem_kv",
    )(mem2d, gain, w_kv, k_gain)


def _mem_attn_kernel(q_ref, k_ref, v_ref, qg_ref, o_ref):
    for h in range(MEM_HEADS):
        sl = slice(h * MEM_HD, (h + 1) * MEM_HD)
        q = _rms(q_ref[:, sl].astype(F32), qg_ref[...]) * (MEM_HD ** -0.5)
        s = _dot_nt(q.astype(BF16), k_ref[:, sl])
        pr = jnp.exp(s - jnp.max(s, axis=1, keepdims=True))
        o = _dot(pr.astype(BF16), v_ref[:, sl]) / jnp.sum(pr, axis=1, keepdims=True)
        o_ref[:, sl] = o.astype(o_ref.dtype)


def _mem_attn(p, mk, mv, q_gain, B, S):
    tq = min(512, S)
    nq = S // tq
    width = MEM_HEADS * MEM_HD
    return pl.pallas_call(
        _mem_attn_kernel,
        grid=(B, nq),
        in_specs=[pl.BlockSpec((tq, width), lambda b, i: (b * nq + i, _COL["mq"] // width)),
                  pl.BlockSpec((MEM_LEN, width), lambda b, i: (b, 0)),
                  pl.BlockSpec((MEM_LEN, width), lambda b, i: (b, 0)),
                  pl.BlockSpec((1, MEM_HD), lambda b, i: (0, 0))],
        out_specs=pl.BlockSpec((tq, width), lambda b, i: (b * nq + i, 0)),
        out_shape=jax.ShapeDtypeStruct((B * S, width), BF16),
        compiler_params=_cparams("arbitrary", "arbitrary"),
        name="mem_attn",
    )(p, mk, mv, q_gain)


def _out_proj_kernel(og_ref, ys_ref, od_ref, om_ref, z_ref, x_ref, gn_ref, wglu_ref, w_ref, o_ref, lhs_ref):
    y = jax.nn.gelu(jnp.concatenate([ys_ref[q] for q in range(ys_ref.shape[0])], axis=1))
    o_s5 = y * jax.nn.sigmoid(_dot(y.astype(BF16), wglu_ref[...]))
    branches = (og_ref[...].astype(F32), o_s5, od_ref[...].astype(F32), om_ref[...].astype(F32))
    for bi, o in enumerate(branches):
        for h in range(o.shape[1] // HEAD):
            c0 = bi * 512 + h * HEAD
            z = z_ref[:, c0:c0 + HEAD].astype(F32)
            n = _rms(o[:, h * HEAD:(h + 1) * HEAD], gn_ref[:, c0:c0 + HEAD])
            lhs_ref[:, c0:c0 + HEAD] = (n * (z * jax.nn.sigmoid(z))).astype(BF16)
    o_ref[...] = x_ref[...] + _dot(lhs_ref[...], w_ref[...])


def _out_proj(o_gla, y_s5, o_dsa, o_mem, p, x2d, out_gain, w_glu, w_out, l):
    T = x2d.shape[0]
    tm = min(512, T)
    branch = pl.BlockSpec((tm, 512), lambda i: (i, 0))
    return pl.pallas_call(
        _out_proj_kernel,
        grid=(T // tm,),
        in_specs=[branch, pl.BlockSpec((S5_WIDTH // 128, tm, 128), lambda i: (0, i, 0)), branch, branch,
                  pl.BlockSpec((tm, MIX_WIDTH), lambda i: (i, 0)),
                  pl.BlockSpec((tm, D_MODEL), lambda i: (i, 0)),
                  _resident((1, MIX_WIDTH)),
                  _layer((S5_WIDTH, S5_WIDTH), l),
                  _layer((MIX_WIDTH, D_MODEL), l)],
        out_specs=pl.BlockSpec((tm, D_MODEL), lambda i: (i, 0)),
        out_shape=jax.ShapeDtypeStruct((T, D_MODEL), F32),
        scratch_shapes=[pltpu.VMEM((tm, MIX_WIDTH), BF16)],
        compiler_params=_cparams("arbitrary"),
        name="out_proj",
    )(o_gla, y_s5, o_dsa, o_mem, p, x2d, out_gain, w_glu, w_out)


def _permute_kernel(w_ref, o_ref):
    c = 0
    for n in _ORDER:
        lo, hi = _ORIG[n]
        o_ref[c:c + hi - lo, :] = w_ref[lo:hi, :].astype(o_ref.dtype)
        c += hi - lo
    o_ref[c:, :] = jnp.zeros((IN_PAD - c, o_ref.shape[1]), o_ref.dtype)


def _permute_w_in(w_in):
    depth, rows, width = w_in.shape
    tl = 256
    return pl.pallas_call(
        _permute_kernel,
        grid=(depth, rows // tl),
        in_specs=[pl.BlockSpec((None, width, tl), lambda l, i: (l, 0, i))],
        out_specs=pl.BlockSpec((None, IN_PAD, tl), lambda l, i: (l, 0, i)),
        out_shape=jax.ShapeDtypeStruct((depth, IN_PAD, rows), BF16),
        compiler_params=_cparams("arbitrary", "arbitrary"),
        name="permute_w_in",
    )(jnp.swapaxes(w_in, 1, 2))


def kernel(x, mem, positions, ln_gain, w_in, gla_w_up, gla_b_up, s5_a_re, s5_a_im, s5_log_step, s5_b_re, s5_b_im, s5_c_re, s5_c_im, s5_d, s5_w_glu, dsa_q_norm, dsa_k_norm, mem_norm, w_mem_kv, mem_q_norm, mem_k_norm, out_norm, w_out):
    B, S, _ = x.shape
    depth = w_in.shape[0]
    T = B * S
    x2d = x.reshape(T, D_MODEL)
    mem2d = mem.reshape(B * mem.shape[1], D_MODEL)
    w_in_p = _permute_w_in(w_in)
    wup_pad = jnp.zeros((depth, 128, GLA_HEADS * GLA_DK), F32)
    wup_pad = wup_pad.at[:, GLR_LANE:GLR_LANE + GLA_GATE_RANK].set(gla_w_up).astype(BF16)
    w_kv = w_mem_kv.astype(BF16)
    w_glu = s5_w_glu.astype(BF16)
    w_o = w_out.astype(BF16)
    tables = _rope_tables(positions)
    for l in range(depth):
        p, su = _in_proj(x2d, ln_gain[l][None], w_in_p, l)
        o_gla = _gla(p, wup_pad[l], gla_b_up[l][None], B, S)
        mats = _s5_mats(s5_a_re[l], s5_a_im[l], s5_log_step[l], s5_b_re[l], s5_b_im[l],
                        s5_c_re[l], s5_c_im[l], s5_d[l])
        y_s5 = _s5(su, mats, B, S)
        qt, k_rot, vt, iqt, ik_rot, wt = _dsa_prep(p, tables, dsa_q_norm[l][None], dsa_k_norm[l][None])
        o_dsa = _dsa(qt, k_rot, vt, iqt, ik_rot, wt, B, S)
        mk, mv = _mem_kv(mem2d, mem_norm[l][None], w_kv, mem_k_norm[l][None], B, l)
        o_mem = _mem_attn(p, mk, mv, mem_q_norm[l][None], B, S)
        x2d = _out_proj(o_gla, y_s5, o_dsa, o_mem, p, x2d, out_norm[l][None], w_glu, w_o, l)
    return x2d.reshape(B, S, D_MODEL)
```

```python
import functools

import numpy as np
import jax
import jax.numpy as jnp
from jax import lax
from jax.experimental import pallas as pl
from jax.experimental.pallas import tpu as pltpu

F32 = jnp.float32
BF16 = jnp.bfloat16

D_MODEL = 2048
MEM_LEN = 256
ROPE_THETA = 500000.0
ROPE_FRAC = 4
NORM_EPS = 1e-6

GLA_HEADS = 4
GLA_DK = 64
GLA_DV = 128
GLA_GATE_RANK = 16
GLA_TAU = 16.0
GLA_CHUNK = 64

S5_GROUPS = 32
S5_CH = 16
S5_STATE = 64
S5_WIDTH = S5_GROUPS * S5_CH
S5_BLOCK = 16

DSA_HEADS = 4
DSA_HD = 128
DSA_IDX_HEADS = 16
DSA_IDX_DIM = 64
DSA_TOPK = 256
DSA_QBLOCK = 256
DSA_KCHUNK = 256

MEM_HEADS = 4
MEM_HD = 128
HEAD = 128
MIX_WIDTH = 2048

VMEM_LIMIT_BYTES = 52 * 1024 * 1024

_ORIG = dict(gq=(0, 256), gk=(256, 512), gv=(512, 1024), glr=(1024, 1040), gz=(1040, 1552),
             su=(1552, 2064), sz=(2064, 2576), dq=(2576, 3088), dk=(3088, 3216), dv=(3216, 3344),
             diq=(3344, 4368), dik=(4368, 4432), diw=(4432, 4448), dz=(4448, 4960),
             mq=(4960, 5472), mz=(5472, 5984))
_ORDER = ("gz", "sz", "dz", "mz", "diq", "gv", "su", "dq", "mq", "gq", "gk", "dk", "dv",
          "dik", "glr", "diw")
_COL = {}
_c = 0
for _n in _ORDER:
    _COL[_n] = _c
    _c += _ORIG[_n][1] - _ORIG[_n][0]
IN_PAD = 6144
MISC_COL = _COL["dik"]
GLR_LANE = 64
DIW_LANE = 80
INT_MIN = -(2 ** 31)
LOG2E = 1.4426950408889634


def _cparams(*sem):
    return pltpu.CompilerParams(dimension_semantics=sem, vmem_limit_bytes=VMEM_LIMIT_BYTES)


def _dot(a, b):
    return jnp.dot(a, b, preferred_element_type=F32)


def _dot_nt(a, b):
    return lax.dot_general(a, b, (((1,), (1,)), ((), ())), preferred_element_type=F32)


def _rms(x, gain):
    return x * lax.rsqrt(jnp.mean(x * x, axis=-1, keepdims=True) + NORM_EPS) * gain


def _in_proj_kernel(x_ref, g_ref, w_ref, o_ref, su_ref, h_ref):
    h_ref[...] = _rms(x_ref[...], g_ref[...]).astype(BF16)
    tn = S5_WIDTH
    for j in range(IN_PAD // tn):
        acc = _dot_nt(h_ref[...], w_ref[j * tn:(j + 1) * tn, :])
        o_ref[:, j * tn:(j + 1) * tn] = acc.astype(o_ref.dtype)
        if j == _COL["su"] // tn:
            for q in range(tn // 128):
                su_ref[q] = acc[:, q * 128:(q + 1) * 128]


def _resident(shape):
    return pl.BlockSpec(shape, lambda *_: (0,) * len(shape), pipeline_mode=pl.Buffered(1))


def _layer(shape, l):
    return pl.BlockSpec((None,) + shape, lambda *_: (l,) + (0,) * len(shape), pipeline_mode=pl.Buffered(1))


def _in_proj(x2d, gain, w, l):
    T = x2d.shape[0]
    tm = min(256, T)
    return pl.pallas_call(
        _in_proj_kernel,
        grid=(T // tm,),
        in_specs=[pl.BlockSpec((tm, D_MODEL), lambda i: (i, 0)),
                  _resident((1, D_MODEL)),
                  _layer((IN_PAD, D_MODEL), l)],
        out_specs=[pl.BlockSpec((tm, IN_PAD), lambda i: (i, 0)),
                   pl.BlockSpec((S5_WIDTH // 128, tm, 128), lambda i: (0, i, 0))],
        out_shape=[jax.ShapeDtypeStruct((T, IN_PAD), BF16),
                   jax.ShapeDtypeStruct((S5_WIDTH // 128, T, 128), F32)],
        scratch_shapes=[pltpu.VMEM((tm, D_MODEL), BF16)],
        compiler_params=_cparams("arbitrary"),
        name="in_proj",
    )(x2d, gain, w)


def _gla_kernel(q_ref, k_ref, v_ref, misc_ref, wup_ref, bup_ref, o_ref, st_ref):
    @pl.when(pl.program_id(1) == 0)
    def _():
        st_ref[...] = jnp.zeros_like(st_ref)

    C = GLA_CHUNK
    tc = q_ref.shape[0]
    nchunk = tc // C
    x = _dot(misc_ref[...], wup_ref[...]) + bup_ref[...]
    log_a = (jnp.minimum(x, 0.0) - jnp.log(1.0 + jnp.exp(-jnp.abs(x)))) * (1.0 / GLA_TAU)
    row = lax.broadcasted_iota(jnp.int32, (tc, tc), 0)
    col = lax.broadcasted_iota(jnp.int32, (tc, tc), 1)
    tril = (row >= col) & ((row // C) == (col // C))
    la_hi = log_a.astype(BF16)
    r1 = log_a - la_hi.astype(F32)
    la_mid = r1.astype(BF16)
    la_lo = (r1 - la_mid.astype(F32)).astype(BF16)
    tri3 = jnp.concatenate([tril.astype(BF16)] * 3, axis=1)
    b = _dot(tri3, jnp.concatenate([la_hi, la_mid, la_lo], axis=0))
    lane = lax.broadcasted_iota(jnp.int32, (1, 2 * GLA_DK), 1)
    low = lane < GLA_DK
    srow = lax.broadcasted_iota(jnp.int32, (2 * GLA_DV, 1), 0)
    diag = (srow < GLA_DV) == low
    for j in range(GLA_HEADS // 2):
        ls = slice(j * 128, (j + 1) * 128)
        bp = b[:, ls]
        mid = jnp.concatenate([jnp.broadcast_to(bp[c * C + C // 2:c * C + C // 2 + 1], (C, 128))
                               for c in range(nchunk)], axis=0)
        last = jnp.concatenate([jnp.broadcast_to(bp[c * C + C - 1:c * C + C], (C, 128))
                                for c in range(nchunk)], axis=0)
        qf = q_ref[:, ls].astype(F32) * (GLA_DK ** -0.5)
        kf = k_ref[:, ls].astype(F32)
        qe = qf * jnp.exp(bp - mid)
        ke_p = (kf * jnp.exp(mid - bp)).astype(BF16)
        kd_p = (kf * jnp.exp(last - bp)).astype(BF16)
        qi_p = (qf * jnp.exp(bp)).astype(BF16)
        intra = []
        for hh in range(2):
            h = 2 * j + hh
            sel = low if hh == 0 else jnp.logical_not(low)
            qe_h = jnp.where(sel, qe, 0.0).astype(BF16)
            attn = jnp.where(tril, _dot_nt(qe_h, ke_p), 0.0)
            intra.append(_dot(attn.astype(BF16), v_ref[:, h * GLA_DV:(h + 1) * GLA_DV]))
        intra = jnp.concatenate(intra, axis=1)
        vs = slice(2 * j * GLA_DV, (2 * j + 2) * GLA_DV)
        st = st_ref[j]
        for c in range(nchunk):
            rs = slice(c * C, (c + 1) * C)
            o = intra[rs] + _dot_nt(qi_p[rs], st.astype(BF16))
            o_ref[rs, vs] = o.astype(o_ref.dtype)
            v_t = v_ref[rs, vs].astype(F32).T.astype(BF16)
            dec = jnp.exp(bp[c * C + C - 1:c * C + C])
            st = jnp.where(diag, st * dec + _dot(v_t, kd_p[rs]), 0.0)
        st_ref[j] = st


def _gla(p, wup_pad, bup, B, S):
    tc = min(256, S)
    nc = S // tc
    row = lambda b, c: b * nc + c
    return pl.pallas_call(
        _gla_kernel,
        grid=(B, nc),
        in_specs=[pl.BlockSpec((tc, 256), lambda b, c: (row(b, c), _COL["gq"] // 256)),
                  pl.BlockSpec((tc, 256), lambda b, c: (row(b, c), _COL["gk"] // 256)),
                  pl.BlockSpec((tc, 512), lambda b, c: (row(b, c), _COL["gv"] // 512)),
                  pl.BlockSpec((tc, 128), lambda b, c: (row(b, c), MISC_COL // 128)),
                  _resident((128, 256)),
                  _resident((1, 256))],
        out_specs=pl.BlockSpec((tc, 512), lambda b, c: (row(b, c), 0)),
        out_shape=jax.ShapeDtypeStruct((B * S, GLA_HEADS * GLA_DV), BF16),
        scratch_shapes=[pltpu.VMEM((GLA_HEADS // 2, 2 * GLA_DV, 2 * GLA_DK), F32)],
        compiler_params=_cparams("arbitrary", "arbitrary"),
        name="gla",
    )(p, p, p, p, wup_pad, bup)


def _s5_mats(a_re, a_im, log_step, b_re, b_im, c_re, c_im, d_skip):
    L, P, CH, G = S5_BLOCK, S5_STATE, S5_CH, S5_GROUPS
    hi = lax.Precision.HIGHEST
    step = jnp.exp(log_step.astype(F32))[:, None]
    are, aim = a_re.astype(F32), a_im.astype(F32)

    def power(j):
        mag = jnp.exp(are[:, None, :] * step[:, None, :] * j[None, :, None])
        ang = aim[:, None, :] * step[:, None, :] * j[None, :, None]
        return mag * jnp.cos(ang), mag * jnp.sin(ang)

    t = jnp.arange(L, dtype=F32)
    lb_re, lb_im = (v[:, 0] for v in power(jnp.ones((1,), F32)))
    den = are * are + aim * aim
    f_re = ((lb_re - 1.0) * are + lb_im * aim) / den
    f_im = (lb_im * are - (lb_re - 1.0) * aim) / den
    bt_re, bt_im = jnp.swapaxes(b_re, 1, 2), jnp.swapaxes(b_im, 1, 2)
    bb_re = f_re[:, None] * bt_re - f_im[:, None] * bt_im
    bb_im = f_re[:, None] * bt_im + f_im[:, None] * bt_re
    n_re, n_im = power(-t)
    a_r = (n_re[:, :, None] * bb_re[:, None] - n_im[:, :, None] * bb_im[:, None]).reshape(G, L * CH, P)
    a_i = (n_re[:, :, None] * bb_im[:, None] + n_im[:, :, None] * bb_re[:, None]).reshape(G, L * CH, P)
    p_re, p_im = power(t)
    ct_re, ct_im = jnp.swapaxes(c_re, 1, 2), jnp.swapaxes(c_im, 1, 2)
    pt_re, pt_im = jnp.swapaxes(p_re, 1, 2), jnp.swapaxes(p_im, 1, 2)
    b_r = (pt_re[..., None] * ct_re[:, :, None] - pt_im[..., None] * ct_im[:, :, None]).reshape(G, P, L * CH)
    b_i = (pt_re[..., None] * ct_im[:, :, None] + pt_im[..., None] * ct_re[:, :, None]).reshape(G, P, L * CH)
    m = (jnp.einsum("gxp,gpy->gxy", a_r, b_r, precision=hi) - jnp.einsum("gxp,gpy->gxy", a_i, b_i, precision=hi))
    blk = np.arange(L * CH) // CH
    m = jnp.where((blk[None, :] >= blk[:, None])[None], m, 0.0)
    m = m + jnp.eye(L * CH, dtype=F32)[None] * jnp.tile(d_skip.reshape(G, 1, CH), (1, 1, L))
    e_re, e_im = (v[:, 0][:, None] for v in power(jnp.full((1,), L - 1.0, F32)))
    pr, pi = a_r * e_re - a_i * e_im, a_r * e_im + a_i * e_re
    pm = jnp.concatenate([pr, pi, pi, pr], axis=-1)
    l_re, l_im = lb_re[..., None], lb_im[..., None]
    qm = jnp.concatenate([b_r * l_re - b_i * l_im, -(b_r * l_im + b_i * l_re)], axis=1)
    al_re, al_im = (v[:, 0] for v in power(jnp.full((1,), float(L), F32)))
    ar = jnp.concatenate([al_re, al_re], axis=-1).reshape(1, G * 2 * P)
    ai_a = jnp.concatenate([-al_im, al_im], axis=-1).reshape(1, G * 2 * P)
    ai_b = jnp.concatenate([al_im, -al_im], axis=-1).reshape(1, G * 2 * P)
    return m.astype(BF16), pm.astype(BF16), qm.astype(BF16), ar, ai_a, ai_b


def _s5_perm():
    src = np.arange(1024).reshape(8, 8, S5_CH).transpose(1, 0, 2).reshape(-1)
    m = np.zeros((1024, 1024), np.float32)
    m[np.arange(1024), src] = 1.0
    return jnp.asarray(m, BF16)


def _s5_kernel(su_ref, perm_ref, m_ref, p_ref, q_ref, ar_ref, aia_ref, aib_ref, y_ref,
               xcat_ref, u_ref, va_ref, vb_ref, x_ref, ycat_ref):
    L = S5_BLOCK
    NB = su_ref.shape[1] // L
    GH = 8
    W = 2 * S5_STATE
    for s in range(L):
        xcat_ref[:, s * 128:(s + 1) * 128] = su_ref[0, pl.ds(s, NB, stride=L), :].astype(BF16)
    for half in range(2):
        u_ref[half] = _dot(xcat_ref[:, half * 1024:(half + 1) * 1024], perm_ref[...]).astype(BF16)

    def u_of(g):
        return jnp.concatenate([u_ref[0, :, g * 128:(g + 1) * 128], u_ref[1, :, g * 128:(g + 1) * 128]], axis=1)

    for g in range(GH):
        v2 = _dot(u_of(g), p_ref[g])
        va_ref[:, g * W:(g + 1) * W] = v2[:, :W]
        vb_ref[:, g * W:(g + 1) * W] = v2[:, W:]
    ar, aia, aib = ar_ref[...], aia_ref[...], aib_ref[...]

    def step(n, carry):
        xa, xb = carry
        x_ref[pl.ds(n, 1), :] = xa
        na = xa * ar + xb * aia + va_ref[pl.ds(n, 1), :]
        nb = xb * ar + xa * aib + vb_ref[pl.ds(n, 1), :]
        return na, nb

    z = jnp.zeros((1, GH * W), F32)
    lax.fori_loop(0, NB, step, (z, z), unroll=8)
    for g in range(GH):
        y = _dot(u_of(g), m_ref[g]) + _dot(x_ref[:, g * W:(g + 1) * W].astype(BF16), q_ref[g])
        for half in range(2):
            ycat_ref[half, :, g * 128:(g + 1) * 128] = y[:, half * 128:(half + 1) * 128].astype(BF16)
    for half in range(2):
        out = _dot(ycat_ref[half], perm_ref[...])
        for t in range(8):
            y_ref[0, pl.ds(half * 8 + t, NB, stride=L), :] = out[:, t * 128:(t + 1) * 128]


def _s5(su, mats, B, S):
    m, pm, qm, ar, aia, aib = mats
    L, G, CH = S5_BLOCK, S5_GROUPS, S5_CH
    NB = S // L
    GH = 8
    W = 2 * S5_STATE
    slab = pl.BlockSpec((1, S, 128), lambda b, h: (h, b, 0))
    coef = pl.BlockSpec((1, GH * W), lambda b, h: (0, h))
    return pl.pallas_call(
        _s5_kernel,
        grid=(B, G // GH),
        in_specs=[slab,
                  pl.BlockSpec((1024, 1024), lambda b, h: (0, 0)),
                  pl.BlockSpec((GH, L * CH, L * CH), lambda b, h: (h, 0, 0)),
                  pl.BlockSpec((GH, L * CH, 2 * W), lambda b, h: (h, 0, 0)),
                  pl.BlockSpec((GH, W, L * CH), lambda b, h: (h, 0, 0)),
                  coef, coef, coef],
        out_specs=slab,
        out_shape=jax.ShapeDtypeStruct((G // GH, B * S, 128), F32),
        scratch_shapes=[pltpu.VMEM((NB, L * 128), BF16), pltpu.VMEM((2, NB, 1024), BF16),
                        pltpu.VMEM((NB, GH * W), F32), pltpu.VMEM((NB, GH * W), F32),
                        pltpu.VMEM((NB, GH * W), F32), pltpu.VMEM((2, NB, 1024), BF16)],
        compiler_params=_cparams("arbitrary", "arbitrary"),
        name="s5_scan",
    )(su, _s5_perm(), m, pm, qm, ar, aia, aib)


def _rope_kernel(pos_ref, f128_ref, f64_ref, c128_ref, s128_ref, c64_ref, s64_ref):
    pos = pos_ref[...]
    a = pos * f128_ref[...]
    c128_ref[...] = jnp.cos(a)
    s128_ref[...] = jnp.sin(a)
    a = pos * f64_ref[...]
    c64_ref[...] = jnp.cos(a)
    s64_ref[...] = jnp.sin(a)


def _rope_freq(d, width):
    rd = d // ROPE_FRAC
    half = rd // 2
    inv = ROPE_THETA ** (-jnp.arange(half, dtype=F32) * 2.0 / rd)
    per_head = jnp.concatenate([inv, inv, jnp.zeros((d - rd,), F32)])
    return jnp.tile(per_head, width // d).reshape(1, width)


def _rope_tables(positions):
    T = positions.size
    pos = positions.astype(F32).reshape(T, 1)
    tq = min(512, T)
    out = jax.ShapeDtypeStruct((T, 128), F32)
    return pl.pallas_call(
        _rope_kernel,
        grid=(T // tq,),
        in_specs=[pl.BlockSpec((tq, 1), lambda i: (i, 0)),
                  pl.BlockSpec((1, 128), lambda i: (0, 0)),
                  pl.BlockSpec((1, 128), lambda i: (0, 0))],
        out_specs=[pl.BlockSpec((tq, 128), lambda i: (i, 0))] * 4,
        out_shape=[out] * 4,
        compiler_params=_cparams("arbitrary"),
        name="rope_tables",
    )(pos, _rope_freq(DSA_HD, 128), _rope_freq(DSA_IDX_DIM, 128))


def _rope(x, cos, sin, head_dim):
    width = x.shape[-1]
    half = head_dim // ROPE_FRAC // 2
    lane = lax.broadcasted_iota(jnp.int32, (1, width), 1)
    per = lane % head_dim
    sgn_lo = jnp.where(per < half, -1.0, 0.0)
    sgn_hi = jnp.where((per >= half) & (per < 2 * half), 1.0, 0.0)
    up = pltpu.roll(x, width - half, axis=1)
    dn = pltpu.roll(x, half, axis=1)
    return x * cos + sin * (up * sgn_lo + dn * sgn_hi)


def _dsa_prep_kernel(dq_ref, dk_ref, dv_ref, diq_ref, misc_ref, c128_ref, s128_ref, c64_ref, s64_ref,
                     qg_ref, kg_ref, qt_out, k_out, vt_out, iqt_out, ik_out, wt_out):
    c128, s128 = c128_ref[...], s128_ref[...]
    c64, s64 = c64_ref[...], s64_ref[...]
    for h in range(DSA_HEADS):
        sl = slice(h * DSA_HD, (h + 1) * DSA_HD)
        q = _rope(_rms(dq_ref[:, sl].astype(F32), qg_ref[...]), c128, s128, DSA_HD)
        qt_out[h] = (q * (DSA_HD ** -0.5 * LOG2E)).T.astype(qt_out.dtype)
    k = _rope(_rms(dk_ref[...].astype(F32), kg_ref[...]), c128, s128, DSA_HD)
    k_out[...] = k.astype(k_out.dtype)
    for c in range(vt_out.shape[0]):
        vt_out[c] = dv_ref[c * DSA_KCHUNK:(c + 1) * DSA_KCHUNK, :].astype(F32).T.astype(vt_out.dtype)
    for j in range(DSA_IDX_HEADS // 2):
        sl = slice(j * 128, (j + 1) * 128)
        xt = _rope(diq_ref[:, sl].astype(F32), c64, s64, DSA_IDX_DIM).T
        iqt_out[2 * j] = xt[:DSA_IDX_DIM].astype(iqt_out.dtype)
        iqt_out[2 * j + 1] = xt[DSA_IDX_DIM:].astype(iqt_out.dtype)
    misc = misc_ref[...].astype(F32)
    ik_out[...] = _rope(misc, c64, s64, DSA_IDX_DIM)[:, :DSA_IDX_DIM].astype(ik_out.dtype)
    wt_out[...] = misc.T[DIW_LANE:DIW_LANE + DSA_IDX_HEADS]


def _dsa_prep(p, tables, q_gain, k_gain):
    T = p.shape[0]
    tq = min(512, T)
    assert tq % DSA_KCHUNK == 0
    tab = pl.BlockSpec((tq, 128), lambda i: (i, 0))
    gain = pl.BlockSpec((1, 128), lambda i: (0, 0))
    return pl.pallas_call(
        _dsa_prep_kernel,
        grid=(T // tq,),
        in_specs=[pl.BlockSpec((tq, 512), lambda i: (i, _COL["dq"] // 512)),
                  pl.BlockSpec((tq, 128), lambda i: (i, _COL["dk"] // 128)),
                  pl.BlockSpec((tq, 128), lambda i: (i, _COL["dv"] // 128)),
                  pl.BlockSpec((tq, 1024), lambda i: (i, _COL["diq"] // 1024)),
                  pl.BlockSpec((tq, 128), lambda i: (i, MISC_COL // 128)),
                  tab, tab, tab, tab, gain, gain],
        out_specs=[pl.BlockSpec((DSA_HEADS, DSA_HD, tq), lambda i: (0, 0, i)),
                   pl.BlockSpec((tq, DSA_HD), lambda i: (i, 0)),
                   pl.BlockSpec((tq // DSA_KCHUNK, DSA_HD, DSA_KCHUNK), lambda i: (i, 0, 0)),
                   pl.BlockSpec((DSA_IDX_HEADS, DSA_IDX_DIM, tq), lambda i: (0, 0, i)),
                   pl.BlockSpec((tq, DSA_IDX_DIM), lambda i: (i, 0)),
                   pl.BlockSpec((DSA_IDX_HEADS, tq), lambda i: (0, i))],
        out_shape=[jax.ShapeDtypeStruct((DSA_HEADS, DSA_HD, T), BF16),
                   jax.ShapeDtypeStruct((T, DSA_HD), BF16),
                   jax.ShapeDtypeStruct((T // DSA_KCHUNK, DSA_HD, DSA_KCHUNK), BF16),
                   jax.ShapeDtypeStruct((DSA_IDX_HEADS, DSA_IDX_DIM, T), BF16),
                   jax.ShapeDtypeStruct((T, DSA_IDX_DIM), BF16),
                   jax.ShapeDtypeStruct((DSA_IDX_HEADS, T), F32)],
        compiler_params=_cparams("arbitrary"),
        name="dsa_prep",
    )(p, p, p, p, p, *tables, q_gain, k_gain)


def _bit_transpose32(words):
    a = list(words)
    j, m = 16, 0x0000FFFF
    while j:
        k = 0
        while k < 32:
            t = (a[k] ^ lax.shift_right_logical(a[k + j], jnp.int32(j))) & jnp.int32(m)
            a[k] = a[k] ^ t
            a[k + j] = a[k + j] ^ (t << j)
            k = (k + j + 1) & ~j
        j >>= 1
        m = (m ^ (m << j)) & 0xFFFFFFFF
    return a


def _dsa_kernel(iqt_ref, wt_ref, qt_ref, ik_ref, k_ref, vt_ref, o_ref,
                kbuf_ref, plane_ref, valid_ref, m_ref, g_ref, bias_ref, s_ref, p_ref, acc_ref, *, ksel):
    TQ, CK = DSA_QBLOCK, DSA_KCHUNK
    HK = CK // 2
    WORD = 32
    i = pl.program_id(1)
    nk = (i * TQ + TQ + CK - 1) // CK
    t_idx = i * TQ + lax.broadcasted_iota(jnp.int32, (1, TQ), 1)
    w_all = wt_ref[...] * (DSA_IDX_HEADS ** -0.5 * DSA_IDX_DIM ** -0.5)
    valid_ref[...] = jnp.zeros_like(valid_ref)
    sub = lax.broadcasted_iota(jnp.int32, (8, 1), 0)
    off = t_idx - (nk - 1) * CK
    diag_valid = jnp.zeros((8, TQ), jnp.int32)
    for p in range(WORD):
        bit = (1 << p) if p < WORD - 1 else INT_MIN
        diag_valid = diag_valid | jnp.where(8 * p + sub <= off, bit, 0)

    def score_chunk(kc, carry):
        for half in range(2):
            lhs = ik_ref[0, kc, half * HK:(half + 1) * HK, :]
            acc = None
            for h in range(DSA_IDX_HEADS):
                term = w_all[h:h + 1] * jnp.maximum(_dot(lhs, iqt_ref[h]), 0.0)
                acc = term if acc is None else acc + term
            bits = lax.bitcast_convert_type(acc, jnp.int32)
            kbuf_ref[half * HK:(half + 1) * HK, :] = bits ^ ((bits >> 31) | INT_MIN)
        planes = _bit_transpose32([kbuf_ref[8 * (WORD - 1 - j):8 * (WORD - j), :] for j in range(WORD)])
        for p in range(WORD):
            plane_ref[kc, p] = planes[p]
        valid_ref[kc] = jnp.where(kc == nk - 1, diag_valid, -1)
        return carry

    lax.fori_loop(0, nk, score_chunk, 0)

    step = 4 if plane_ref.shape[0] % 4 == 0 else 2
    nstep = (nk + step - 1) // step

    def blank(kc, carry):
        plane_ref[kc] = jnp.zeros(plane_ref.shape[1:], jnp.int32)
        return carry

    lax.fori_loop(nk, nstep * step, blank, 0)

    def total(acc):
        return jnp.sum(acc.astype(F32), axis=0, keepdims=True)

    def first(it, acc):
        for c in range(step):
            kc = it * step + c
            m = valid_ref[kc]
            m_ref[kc] = m
            g_ref[kc] = jnp.zeros_like(m)
            acc = acc + lax.population_count(m & plane_ref[kc, 0])
        return acc

    def narrow(kc, n, take):
        m, g, pp = m_ref[kc], g_ref[kc], plane_ref[kc, n]
        hit = m & pp
        return jnp.where(take, hit, m & ~pp), jnp.where(take, g, g | hit)

    def decide(cnt, need):
        take = cnt >= need
        return take.astype(jnp.int32), jnp.where(take, need, need - cnt)

    zero = jnp.zeros((8, TQ), jnp.int32)
    take0, need0 = decide(total(lax.fori_loop(0, nstep, first, zero)), jnp.full((1, TQ), float(ksel), F32))

    def sweep(n, carry):
        take = carry[0] != 0

        def body(it, acc):
            for c in range(step):
                kc = it * step + c
                m, g = narrow(kc, n - 1, take)
                m_ref[kc] = m
                g_ref[kc] = g
                acc = acc + lax.population_count(m & plane_ref[kc, n])
            return acc

        return decide(total(lax.fori_loop(0, nstep, body, zero)), carry[1])

    take_l, _ = lax.fori_loop(1, WORD, sweep, (take0, need0))

    def finish(it, carry):
        for c in range(step):
            kc = it * step + c
            m, g = narrow(kc, WORD - 1, take_l != 0)
            m_ref[kc] = m | g
        return carry

    lax.fori_loop(0, nstep, finish, 0)

    kg = CK // 16
    unit = 64

    def logits(kc, buf, m_run):
        sel = m_ref[kc]
        for p in range(WORD):
            on = (lax.shift_right_logical(sel, jnp.int32(p)) & 1) != 0
            bias_ref[8 * p:8 * (p + 1), :] = jnp.where(on, 0.0, -1e30)
        kk = k_ref[0, kc]
        new = []
        for h in range(DSA_HEADS):
            s = _dot(kk, qt_ref[h]) + bias_ref[...]
            s_ref[buf, h] = s
            mx = jnp.max(s.reshape(kg, 2, 8, TQ), axis=0)
            new.append(jnp.maximum(m_run[h], jnp.max(jnp.max(mx, axis=0), axis=0, keepdims=True)))
        return tuple(new)

    def attn_chunk(kc, carry):
        m_old, m_run, ls = carry
        cur = kc % 2
        vt = vt_ref[0, kc]
        new_l = []
        for h in range(DSA_HEADS):
            alpha = jnp.exp2(m_old[h] - m_run[h])
            mb = jnp.broadcast_to(m_run[h], (8, TQ))[None]
            part = ls[h] * alpha
            for u in range(CK // unit):
                rows = slice(u * unit, (u + 1) * unit)
                pr = jnp.exp2(s_ref[cur, h, rows, :].reshape(unit // 8, 8, TQ) - mb)
                part = part + jnp.sum(pr.reshape(unit // 16, 2, 8, TQ), axis=0)
                p_ref[h, rows, :] = pr.reshape(unit, TQ).astype(BF16)
            new_l.append(part)
            acc_ref[h] = acc_ref[h] * alpha + _dot(vt, p_ref[h])
        m_next = logits(jnp.minimum(kc + 1, nk - 1), 1 - cur, m_run)
        return m_run, m_next, tuple(new_l)

    acc_ref[...] = jnp.zeros_like(acc_ref)
    floor = tuple(jnp.full((1, TQ), -1e30, F32) for _ in range(DSA_HEADS))
    init = (floor, logits(0, 0, floor), tuple(jnp.zeros((2, 8, TQ), F32) for _ in range(DSA_HEADS)))
    _, _, ls = lax.fori_loop(0, nk, attn_chunk, init)
    for h in range(DSA_HEADS):
        denom = jnp.sum(jnp.sum(ls[h], axis=0), axis=0, keepdims=True)
        o_ref[:, h * DSA_HD:(h + 1) * DSA_HD] = (acc_ref[h] / denom).T.astype(o_ref.dtype)


def _dsa(qt, k_rot, vt, iqt, ik_rot, wt, B, S):
    TQ, CK = DSA_QBLOCK, DSA_KCHUNK
    NC = S // CK
    nq = S // TQ
    assert NC % 2 == 0 and TQ == CK == 8 * 32
    ksel = min(DSA_TOPK, S // 4)
    ik = ik_rot.reshape(B, NC, CK, DSA_IDX_DIM)
    k = k_rot.reshape(B, NC, CK, DSA_HD)
    vt = vt.reshape(B, NC, DSA_HD, CK)
    return pl.pallas_call(
        functools.partial(_dsa_kernel, ksel=ksel),
        grid=(B, nq),
        in_specs=[pl.BlockSpec((DSA_IDX_HEADS, DSA_IDX_DIM, TQ), lambda b, i: (0, 0, b * nq + i)),
                  pl.BlockSpec((DSA_IDX_HEADS, TQ), lambda b, i: (0, b * nq + i)),
                  pl.BlockSpec((DSA_HEADS, DSA_HD, TQ), lambda b, i: (0, 0, b * nq + i)),
                  pl.BlockSpec((1, NC, CK, DSA_IDX_DIM), lambda b, i: (b, 0, 0, 0)),
                  pl.BlockSpec((1, NC, CK, DSA_HD), lambda b, i: (b, 0, 0, 0)),
                  pl.BlockSpec((1, NC, DSA_HD, CK), lambda b, i: (b, 0, 0, 0))],
        out_specs=pl.BlockSpec((TQ, DSA_HEADS * DSA_HD), lambda b, i: (b * nq + i, 0)),
        out_shape=jax.ShapeDtypeStruct((B * S, DSA_HEADS * DSA_HD), BF16),
        scratch_shapes=[pltpu.VMEM((CK, TQ), jnp.int32),
                        pltpu.VMEM((NC, 32, 8, TQ), jnp.int32),
                        pltpu.VMEM((NC, 8, TQ), jnp.int32),
                        pltpu.VMEM((NC, 8, TQ), jnp.int32),
                        pltpu.VMEM((NC, 8, TQ), jnp.int32),
                        pltpu.VMEM((CK, TQ), F32),
                        pltpu.VMEM((2, DSA_HEADS, CK, TQ), F32),
                        pltpu.VMEM((DSA_HEADS, CK, TQ), BF16),
                        pltpu.VMEM((DSA_HEADS, DSA_HD, TQ), F32)],
        compiler_params=_cparams("arbitrary", "arbitrary"),
        name="dsa_attn",
    )(iqt, wt, qt, ik, k, vt)


def _mem_kv_kernel(mem_ref, g_ref, w_ref, kg_ref, k_out, v_out):
    h = _rms(mem_ref[...], g_ref[...]).astype(BF16)
    kv = _dot(h, w_ref[...])
    width = MEM_HEADS * MEM_HD
    for hd in range(MEM_HEADS):
        sl = slice(hd * MEM_HD, (hd + 1) * MEM_HD)
        k_out[:, sl] = _rms(kv[:, sl], kg_ref[...]).astype(k_out.dtype)
    v_out[...] = kv[:, width:].astype(v_out.dtype)


def _mem_kv(mem2d, gain, w_kv, k_gain, B, l):
    width = MEM_HEADS * MEM_HD
    out = jax.ShapeDtypeStruct((B * MEM_LEN, width), BF16)
    return pl.pallas_call(
        _mem_kv_kernel,
        grid=(B,),
        in_specs=[pl.BlockSpec((MEM_LEN, D_MODEL), lambda b: (b, 0)),
                  pl.BlockSpec((1, D_MODEL), lambda b: (0, 0)),
                  _layer((D_MODEL, 2 * width), l),
                  pl.BlockSpec((1, MEM_HD), lambda b: (0, 0))],
        out_specs=[pl.BlockSpec((MEM_LEN, width), lambda b: (b, 0))] * 2,
        out_shape=[out, out],
        compiler_params=_cparams("arbitrary"),
        name="mem_kv",
    )(mem2d, gain, w_kv, k_gain)


def _mem_attn_kernel(q_ref, k_ref, v_ref, qg_ref, o_ref):
    for h in range(MEM_HEADS):
        sl = slice(h * MEM_HD, (h + 1) * MEM_HD)
        q = _rms(q_ref[:, sl].astype(F32), qg_ref[...]) * (MEM_HD ** -0.5)
        s = _dot_nt(q.astype(BF16), k_ref[:, sl])
        pr = jnp.exp(s - jnp.max(s, axis=1, keepdims=True))
        o = _dot(pr.astype(BF16), v_ref[:, sl]) / jnp.sum(pr, axis=1, keepdims=True)
        o_ref[:, sl] = o.astype(o_ref.dtype)


def _mem_attn(p, mk, mv, q_gain, B, S):
    tq = min(512, S)
    nq = S // tq
    width = MEM_HEADS * MEM_HD
    return pl.pallas_call(
        _mem_attn_kernel,
        grid=(B, nq),
        in_specs=[pl.BlockSpec((tq, width), lambda b, i: (b * nq + i, _COL["mq"] // width)),
                  pl.BlockSpec((MEM_LEN, width), lambda b, i: (b, 0)),
                  pl.BlockSpec((MEM_LEN, width), lambda b, i: (b, 0)),
                  pl.BlockSpec((1, MEM_HD), lambda b, i: (0, 0))],
        out_specs=pl.BlockSpec((tq, width), lambda b, i: (b * nq + i, 0)),
        out_shape=jax.ShapeDtypeStruct((B * S, width), BF16),
        compiler_params=_cparams("arbitrary", "arbitrary"),
        name="mem_attn",
    )(p, mk, mv, q_gain)


def _out_proj_kernel(og_ref, ys_ref, od_ref, om_ref, z_ref, x_ref, gn_ref, wglu_ref, w_ref, o_ref, lhs_ref):
    y = jax.nn.gelu(jnp.concatenate([ys_ref[q] for q in range(ys_ref.shape[0])], axis=1))
    o_s5 = y * jax.nn.sigmoid(_dot(y.astype(BF16), wglu_ref[...]))
    branches = (og_ref[...].astype(F32), o_s5, od_ref[...].astype(F32), om_ref[...].astype(F32))
    for bi, o in enumerate(branches):
        for h in range(o.shape[1] // HEAD):
            c0 = bi * 512 + h * HEAD
            z = z_ref[:, c0:c0 + HEAD].astype(F32)
            n = _rms(o[:, h * HEAD:(h + 1) * HEAD], gn_ref[:, c0:c0 + HEAD])
            lhs_ref[:, c0:c0 + HEAD] = (n * (z * jax.nn.sigmoid(z))).astype(BF16)
    o_ref[...] = x_ref[...] + _dot(lhs_ref[...], w_ref[...])


def _out_proj(o_gla, y_s5, o_dsa, o_mem, p, x2d, out_gain, w_glu, w_out, l):
    T = x2d.shape[0]
    tm = min(512, T)
    branch = pl.BlockSpec((tm, 512), lambda i: (i, 0))
    return pl.pallas_call(
        _out_proj_kernel,
        grid=(T // tm,),
        in_specs=[branch, pl.BlockSpec((S5_WIDTH // 128, tm, 128), lambda i: (0, i, 0)), branch, branch,
                  pl.BlockSpec((tm, MIX_WIDTH), lambda i: (i, 0)),
                  pl.BlockSpec((tm, D_MODEL), lambda i: (i, 0)),
                  _resident((1, MIX_WIDTH)),
                  _layer((S5_WIDTH, S5_WIDTH), l),
                  _layer((MIX_WIDTH, D_MODEL), l)],
        out_specs=pl.BlockSpec((tm, D_MODEL), lambda i: (i, 0)),
        out_shape=jax.ShapeDtypeStruct((T, D_MODEL), F32),
        scratch_shapes=[pltpu.VMEM((tm, MIX_WIDTH), BF16)],
        compiler_params=_cparams("arbitrary"),
        name="out_proj",
    )(o_gla, y_s5, o_dsa, o_mem, p, x2d, out_gain, w_glu, w_out)


def _permute_kernel(w_ref, o_ref):
    c = 0
    for n in _ORDER:
        lo, hi = _ORIG[n]
        o_ref[c:c + hi - lo, :] = w_ref[lo:hi, :].astype(o_ref.dtype)
        c += hi - lo
    o_ref[c:, :] = jnp.zeros((IN_PAD - c, o_ref.shape[1]), o_ref.dtype)


def _permute_w_in(w_in):
    depth, rows, width = w_in.shape
    tl = 256
    return pl.pallas_call(
        _permute_kernel,
        grid=(depth, rows // tl),
        in_specs=[pl.BlockSpec((None, width, tl), lambda l, i: (l, 0, i))],
        out_specs=pl.BlockSpec((None, IN_PAD, tl), lambda l, i: (l, 0, i)),
        out_shape=jax.ShapeDtypeStruct((depth, IN_PAD, rows), BF16),
        compiler_params=_cparams("arbitrary", "arbitrary"),
        name="permute_w_in",
    )(jnp.swapaxes(w_in, 1, 2))


def kernel(x, mem, positions, ln_gain, w_in, gla_w_up, gla_b_up, s5_a_re, s5_a_im, s5_log_step, s5_b_re, s5_b_im, s5_c_re, s5_c_im, s5_d, s5_w_glu, dsa_q_norm, dsa_k_norm, mem_norm, w_mem_kv, mem_q_norm, mem_k_norm, out_norm, w_out):
    B, S, _ = x.shape
    depth = w_in.shape[0]
    T = B * S
    x2d = x.reshape(T, D_MODEL)
    mem2d = mem.reshape(B * mem.shape[1], D_MODEL)
    w_in_p = _permute_w_in(w_in)
    wup_pad = jnp.zeros((depth, 128, GLA_HEADS * GLA_DK), F32)
    wup_pad = wup_pad.at[:, GLR_LANE:GLR_LANE + GLA_GATE_RANK].set(gla_w_up).astype(BF16)
    w_kv = w_mem_kv.astype(BF16)
    w_glu = s5_w_glu.astype(BF16)
    w_o = w_out.astype(BF16)
    tables = _rope_tables(positions)
    for l in range(depth):
        p, su = _in_proj(x2d, ln_gain[l][None], w_in_p, l)
        o_gla = _gla(p, wup_pad[l], gla_b_up[l][None], B, S)
        mats = _s5_mats(s5_a_re[l], s5_a_im[l], s5_log_step[l], s5_b_re[l], s5_b_im[l],
                        s5_c_re[l], s5_c_im[l], s5_d[l])
        y_s5 = _s5(su, mats, B, S)
        qt, k_rot, vt, iqt, ik_rot, wt = _dsa_prep(p, tables, dsa_q_norm[l][None], dsa_k_norm[l][None])
        o_dsa = _dsa(qt, k_rot, vt, iqt, ik_rot, wt, B, S)
        mk, mv = _mem_kv(mem2d, mem_norm[l][None], w_kv, mem_k_norm[l][None], B, l)
        o_mem = _mem_attn(p, mk, mv, mem_q_norm[l][None], B, S)
        x2d = _out_proj(o_gla, y_s5, o_dsa, o_mem, p, x2d, out_norm[l][None], w_glu, w_o, l)
    return x2d.reshape(B, S, D_MODEL)
```

```python
import functools

import numpy as np
import jax
import jax.numpy as jnp
from jax import lax
from jax.experimental import pallas as pl
from jax.experimental.pallas import tpu as pltpu

F32 = jnp.float32
BF16 = jnp.bfloat16

D_MODEL = 2048
MEM_LEN = 256
ROPE_THETA = 500000.0
ROPE_FRAC = 4
NORM_EPS = 1e-6

GLA_HEADS = 4
GLA_DK = 64
GLA_DV = 128
GLA_GATE_RANK = 16
GLA_TAU = 16.0
GLA_CHUNK = 64

S5_GROUPS = 32
S5_CH = 16
S5_STATE = 64
S5_WIDTH = S5_GROUPS * S5_CH
S5_BLOCK = 16

DSA_HEADS = 4
DSA_HD = 128
DSA_IDX_HEADS = 16
DSA_IDX_DIM = 64
DSA_TOPK = 256
DSA_QBLOCK = 256
DSA_KCHUNK = 256

MEM_HEADS = 4
MEM_HD = 128
HEAD = 128
MIX_WIDTH = 2048

VMEM_LIMIT_BYTES = 52 * 1024 * 1024

_ORIG = dict(gq=(0, 256), gk=(256, 512), gv=(512, 1024), glr=(1024, 1040), gz=(1040, 1552),
             su=(1552, 2064), sz=(2064, 2576), dq=(2576, 3088), dk=(3088, 3216), dv=(3216, 3344),
             diq=(3344, 4368), dik=(4368, 4432), diw=(4432, 4448), dz=(4448, 4960),
             mq=(4960, 5472), mz=(5472, 5984))
_ORDER = ("gz", "sz", "dz", "mz", "diq", "gv", "su", "dq", "mq", "gq", "gk", "dk", "dv",
          "dik", "glr", "diw")
_COL = {}
_c = 0
for _n in _ORDER:
    _COL[_n] = _c
    _c += _ORIG[_n][1] - _ORIG[_n][0]
IN_PAD = 6144
MISC_COL = _COL["dik"]
GLR_LANE = 64
DIW_LANE = 80
INT_MIN = -(2 ** 31)
LOG2E = 1.4426950408889634


def _cparams(*sem):
    return pltpu.CompilerParams(dimension_semantics=sem, vmem_limit_bytes=VMEM_LIMIT_BYTES)


def _dot(a, b):
    return jnp.dot(a, b, preferred_element_type=F32)


def _dot_nt(a, b):
    return lax.dot_general(a, b, (((1,), (1,)), ((), ())), preferred_element_type=F32)


def _rms(x, gain):
    return x * lax.rsqrt(jnp.mean(x * x, axis=-1, keepdims=True) + NORM_EPS) * gain


def _in_proj_kernel(x_ref, g_ref, w_ref, o_ref, su_ref, h_ref):
    h_ref[...] = _rms(x_ref[...], g_ref[...]).astype(BF16)
    tn = S5_WIDTH
    for j in range(IN_PAD // tn):
        acc = _dot_nt(h_ref[...], w_ref[j * tn:(j + 1) * tn, :])
        o_ref[:, j * tn:(j + 1) * tn] = acc.astype(o_ref.dtype)
        if j == _COL["su"] // tn:
            for q in range(tn // 128):
                su_ref[q] = acc[:, q * 128:(q + 1) * 128]


def _resident(shape):
    return pl.BlockSpec(shape, lambda *_: (0,) * len(shape), pipeline_mode=pl.Buffered(1))


def _layer(shape, l):
    return pl.BlockSpec((None,) + shape, lambda *_: (l,) + (0,) * len(shape), pipeline_mode=pl.Buffered(1))


def _in_proj(x2d, gain, w, l):
    T = x2d.shape[0]
    tm = min(256, T)
    return pl.pallas_call(
        _in_proj_kernel,
        grid=(T // tm,),
        in_specs=[pl.BlockSpec((tm, D_MODEL), lambda i: (i, 0)),
                  _resident((1, D_MODEL)),
                  _layer((IN_PAD, D_MODEL), l)],
        out_specs=[pl.BlockSpec((tm, IN_PAD), lambda i: (i, 0)),
                   pl.BlockSpec((S5_WIDTH // 128, tm, 128), lambda i: (0, i, 0))],
        out_shape=[jax.ShapeDtypeStruct((T, IN_PAD), BF16),
                   jax.ShapeDtypeStruct((S5_WIDTH // 128, T, 128), F32)],
        scratch_shapes=[pltpu.VMEM((tm, D_MODEL), BF16)],
        compiler_params=_cparams("arbitrary"),
        name="in_proj",
    )(x2d, gain, w)


def _gla_kernel(q_ref, k_ref, v_ref, misc_ref, wup_ref, bup_ref, o_ref, st_ref):
    @pl.when(pl.program_id(1) == 0)
    def _():
        st_ref[...] = jnp.zeros_like(st_ref)

    C = GLA_CHUNK
    tc = q_ref.shape[0]
    nchunk = tc // C
    x = _dot(misc_ref[...], wup_ref[...]) + bup_ref[...]
    log_a = (jnp.minimum(x, 0.0) - jnp.log(1.0 + jnp.exp(-jnp.abs(x)))) * (1.0 / GLA_TAU)
    row = lax.broadcasted_iota(jnp.int32, (tc, tc), 0)
    col = lax.broadcasted_iota(jnp.int32, (tc, tc), 1)
    tril = (row >= col) & ((row // C) == (col // C))
    la_hi = log_a.astype(BF16)
    r1 = log_a - la_hi.astype(F32)
    la_mid = r1.astype(BF16)
    la_lo = (r1 - la_mid.astype(F32)).astype(BF16)
    tri3 = jnp.concatenate([tril.astype(BF16)] * 3, axis=1)
    b = _dot(tri3, jnp.concatenate([la_hi, la_mid, la_lo], axis=0))
    lane = lax.broadcasted_iota(jnp.int32, (1, 2 * GLA_DK), 1)
    low = lane < GLA_DK
    srow = lax.broadcasted_iota(jnp.int32, (2 * GLA_DV, 1), 0)
    diag = (srow < GLA_DV) == low
    for j in range(GLA_HEADS // 2):
        ls = slice(j * 128, (j + 1) * 128)
        bp = b[:, ls]
        mid = jnp.concatenate([jnp.broadcast_to(bp[c * C + C // 2:c * C + C // 2 + 1], (C, 128))
                               for c in range(nchunk)], axis=0)
        last = jnp.concatenate([jnp.broadcast_to(bp[c * C + C - 1:c * C + C], (C, 128))
                                for c in range(nchunk)], axis=0)
        qf = q_ref[:, ls].astype(F32) * (GLA_DK ** -0.5)
        kf = k_ref[:, ls].astype(F32)
        qe = qf * jnp.exp(bp - mid)
        ke_p = (kf * jnp.exp(mid - bp)).astype(BF16)
        kd_p = (kf * jnp.exp(last - bp)).astype(BF16)
        qi_p = (qf * jnp.exp(bp)).astype(BF16)
        intra = []
        for hh in range(2):
            h = 2 * j + hh
            sel = low if hh == 0 else jnp.logical_not(low)
            qe_h = jnp.where(sel, qe, 0.0).astype(BF16)
            attn = jnp.where(tril, _dot_nt(qe_h, ke_p), 0.0)
            intra.append(_dot(attn.astype(BF16), v_ref[:, h * GLA_DV:(h + 1) * GLA_DV]))
        intra = jnp.concatenate(intra, axis=1)
        vs = slice(2 * j * GLA_DV, (2 * j + 2) * GLA_DV)
        st = st_ref[j]
        for c in range(nchunk):
            rs = slice(c * C, (c + 1) * C)
            o = intra[rs] + _dot_nt(qi_p[rs], st.astype(BF16))
            o_ref[rs, vs] = o.astype(o_ref.dtype)
            v_t = v_ref[rs, vs].astype(F32).T.astype(BF16)
            dec = jnp.exp(bp[c * C + C - 1:c * C + C])
            st = jnp.where(diag, st * dec + _dot(v_t, kd_p[rs]), 0.0)
        st_ref[j] = st


def _gla(p, wup_pad, bup, B, S):
    tc = min(256, S)
    nc = S // tc
    row = lambda b, c: b * nc + c
    return pl.pallas_call(
        _gla_kernel,
        grid=(B, nc),
        in_specs=[pl.BlockSpec((tc, 256), lambda b, c: (row(b, c), _COL["gq"] // 256)),
                  pl.BlockSpec((tc, 256), lambda b, c: (row(b, c), _COL["gk"] // 256)),
                  pl.BlockSpec((tc, 512), lambda b, c: (row(b, c), _COL["gv"] // 512)),
                  pl.BlockSpec((tc, 128), lambda b, c: (row(b, c), MISC_COL // 128)),
                  _resident((128, 256)),
                  _resident((1, 256))],
        out_specs=pl.BlockSpec((tc, 512), lambda b, c: (row(b, c), 0)),
        out_shape=jax.ShapeDtypeStruct((B * S, GLA_HEADS * GLA_DV), BF16),
        scratch_shapes=[pltpu.VMEM((GLA_HEADS // 2, 2 * GLA_DV, 2 * GLA_DK), F32)],
        compiler_params=_cparams("arbitrary", "arbitrary"),
        name="gla",
    )(p, p, p, p, wup_pad, bup)


def _s5_mats(a_re, a_im, log_step, b_re, b_im, c_re, c_im, d_skip):
    L, P, CH, G = S5_BLOCK, S5_STATE, S5_CH, S5_GROUPS
    hi = lax.Precision.HIGHEST
    step = jnp.exp(log_step.astype(F32))[:, None]
    are, aim = a_re.astype(F32), a_im.astype(F32)

    def power(j):
        mag = jnp.exp(are[:, None, :] * step[:, None, :] * j[None, :, None])
        ang = aim[:, None, :] * step[:, None, :] * j[None, :, None]
        return mag * jnp.cos(ang), mag * jnp.sin(ang)

    t = jnp.arange(L, dtype=F32)
    lb_re, lb_im = (v[:, 0] for v in power(jnp.ones((1,), F32)))
    den = are * are + aim * aim
    f_re = ((lb_re - 1.0) * are + lb_im * aim) / den
    f_im = (lb_im * are - (lb_re - 1.0) * aim) / den
    bt_re, bt_im = jnp.swapaxes(b_re, 1, 2), jnp.swapaxes(b_im, 1, 2)
    bb_re = f_re[:, None] * bt_re - f_im[:, None] * bt_im
    bb_im = f_re[:, None] * bt_im + f_im[:, None] * bt_re
    n_re, n_im = power(-t)
    a_r = (n_re[:, :, None] * bb_re[:, None] - n_im[:, :, None] * bb_im[:, None]).reshape(G, L * CH, P)
    a_i = (n_re[:, :, None] * bb_im[:, None] + n_im[:, :, None] * bb_re[:, None]).reshape(G, L * CH, P)
    p_re, p_im = power(t)
    ct_re, ct_im = jnp.swapaxes(c_re, 1, 2), jnp.swapaxes(c_im, 1, 2)
    pt_re, pt_im = jnp.swapaxes(p_re, 1, 2), jnp.swapaxes(p_im, 1, 2)
    b_r = (pt_re[..., None] * ct_re[:, :, None] - pt_im[..., None] * ct_im[:, :, None]).reshape(G, P, L * CH)
    b_i = (pt_re[..., None] * ct_im[:, :, None] + pt_im[..., None] * ct_re[:, :, None]).reshape(G, P, L * CH)
    m = (jnp.einsum("gxp,gpy->gxy", a_r, b_r, precision=hi) - jnp.einsum("gxp,gpy->gxy", a_i, b_i, precision=hi))
    blk = np.arange(L * CH) // CH
    m = jnp.where((blk[None, :] >= blk[:, None])[None], m, 0.0)
    m = m + jnp.eye(L * CH, dtype=F32)[None] * jnp.tile(d_skip.reshape(G, 1, CH), (1, 1, L))
    e_re, e_im = (v[:, 0][:, None] for v in power(jnp.full((1,), L - 1.0, F32)))
    pr, pi = a_r * e_re - a_i * e_im, a_r * e_im + a_i * e_re
    pm = jnp.concatenate([pr, pi, pi, pr], axis=-1)
    l_re, l_im = lb_re[..., None], lb_im[..., None]
    qm = jnp.concatenate([b_r * l_re - b_i * l_im, -(b_r * l_im + b_i * l_re)], axis=1)
    al_re, al_im = (v[:, 0] for v in power(jnp.full((1,), float(L), F32)))
    ar = jnp.concatenate([al_re, al_re], axis=-1).reshape(1, G * 2 * P)
    ai_a = jnp.concatenate([-al_im, al_im], axis=-1).reshape(1, G * 2 * P)
    ai_b = jnp.concatenate([al_im, -al_im], axis=-1).reshape(1, G * 2 * P)
    return m.astype(BF16), pm.astype(BF16), qm.astype(BF16), ar, ai_a, ai_b


def _s5_perm():
    src = np.arange(1024).reshape(8, 8, S5_CH).transpose(1, 0, 2).reshape(-1)
    m = np.zeros((1024, 1024), np.float32)
    m[np.arange(1024), src] = 1.0
    return jnp.asarray(m, BF16)


def _s5_kernel(su_ref, perm_ref, m_ref, p_ref, q_ref, ar_ref, aia_ref, aib_ref, y_ref,
               xcat_ref, u_ref, va_ref, vb_ref, x_ref, ycat_ref):
    L = S5_BLOCK
    NB = su_ref.shape[1] // L
    GH = 8
    W = 2 * S5_STATE
    for s in range(L):
        xcat_ref[:, s * 128:(s + 1) * 128] = su_ref[0, pl.ds(s, NB, stride=L), :].astype(BF16)
    for half in range(2):
        u_ref[half] = _dot(xcat_ref[:, half * 1024:(half + 1) * 1024], perm_ref[...]).astype(BF16)

    def u_of(g):
        return jnp.concatenate([u_ref[0, :, g * 128:(g + 1) * 128], u_ref[1, :, g * 128:(g + 1) * 128]], axis=1)

    for g in range(GH):
        v2 = _dot(u_of(g), p_ref[g])
        va_ref[:, g * W:(g + 1) * W] = v2[:, :W]
        vb_ref[:, g * W:(g + 1) * W] = v2[:, W:]
    ar, aia, aib = ar_ref[...], aia_ref[...], aib_ref[...]

    def step(n, carry):
        xa, xb = carry
        x_ref[pl.ds(n, 1), :] = xa
        na = xa * ar + xb * aia + va_ref[pl.ds(n, 1), :]
        nb = xb * ar + xa * aib + vb_ref[pl.ds(n, 1), :]
        return na, nb

    z = jnp.zeros((1, GH * W), F32)
    lax.fori_loop(0, NB, step, (z, z), unroll=8)
    for g in range(GH):
        y = _dot(u_of(g), m_ref[g]) + _dot(x_ref[:, g * W:(g + 1) * W].astype(BF16), q_ref[g])
        for half in range(2):
            ycat_ref[half, :, g * 128:(g + 1) * 128] = y[:, half * 128:(half + 1) * 128].astype(BF16)
    for half in range(2):
        out = _dot(ycat_ref[half], perm_ref[...])
        for t in range(8):
            y_ref[0, pl.ds(half * 8 + t, NB, stride=L), :] = out[:, t * 128:(t + 1) * 128]


def _s5(su, mats, B, S):
    m, pm, qm, ar, aia, aib = mats
    L, G, CH = S5_BLOCK, S5_GROUPS, S5_CH
    NB = S // L
    GH = 8
    W = 2 * S5_STATE
    slab = pl.BlockSpec((1, S, 128), lambda b, h: (h, b, 0))
    coef = pl.BlockSpec((1, GH * W), lambda b, h: (0, h))
    return pl.pallas_call(
        _s5_kernel,
        grid=(B, G // GH),
        in_specs=[slab,
                  pl.BlockSpec((1024, 1024), lambda b, h: (0, 0)),
                  pl.BlockSpec((GH, L * CH, L * CH), lambda b, h: (h, 0, 0)),
                  pl.BlockSpec((GH, L * CH, 2 * W), lambda b, h: (h, 0, 0)),
                  pl.BlockSpec((GH, W, L * CH), lambda b, h: (h, 0, 0)),
                  coef, coef, coef],
        out_specs=slab,
        out_shape=jax.ShapeDtypeStruct((G // GH, B * S, 128), F32),
        scratch_shapes=[pltpu.VMEM((NB, L * 128), BF16), pltpu.VMEM((2, NB, 1024), BF16),
                        pltpu.VMEM((NB, GH * W), F32), pltpu.VMEM((NB, GH * W), F32),
                        pltpu.VMEM((NB, GH * W), F32), pltpu.VMEM((2, NB, 1024), BF16)],
        compiler_params=_cparams("arbitrary", "arbitrary"),
        name="s5_scan",
    )(su, _s5_perm(), m, pm, qm, ar, aia, aib)


def _rope_kernel(pos_ref, prow_ref, f128_ref, f64_ref, i128_ref, i64_ref,
                 c128_ref, s128_ref, c64_ref, s64_ref, ct128_ref, st128_ref, ct64_ref, st64_ref):
    pos = pos_ref[...]
    a = pos * f128_ref[...]
    c128_ref[...] = jnp.cos(a)
    s128_ref[...] = jnp.sin(a)
    a = pos * f64_ref[...]
    c64_ref[...] = jnp.cos(a)
    s64_ref[...] = jnp.sin(a)
    a = i128_ref[...] * prow_ref[...]
    ct128_ref[...] = jnp.cos(a)
    st128_ref[...] = jnp.sin(a)
    a = i64_ref[...] * prow_ref[...]
    ct64_ref[...] = jnp.cos(a)
    st64_ref[...] = jnp.sin(a)


def _rope_inv(d):
    rd = d // ROPE_FRAC
    return ROPE_THETA ** (-jnp.arange(rd // 2, dtype=F32) * 2.0 / rd)


def _rope_freq(d, width):
    inv = _rope_inv(d)
    per_head = jnp.concatenate([inv, inv, jnp.zeros((d - d // ROPE_FRAC,), F32)])
    return jnp.tile(per_head, width // d).reshape(1, width)


def _rope_tables(positions):
    T = positions.size
    pos = positions.astype(F32)
    tq = min(512, T)
    h128, h64 = DSA_HD // ROPE_FRAC // 2, DSA_IDX_DIM // ROPE_FRAC // 2
    full = lambda shape: pl.BlockSpec(shape, lambda i: (0, 0))
    return pl.pallas_call(
        _rope_kernel,
        grid=(T // tq,),
        in_specs=[pl.BlockSpec((tq, 1), lambda i: (i, 0)), pl.BlockSpec((1, tq), lambda i: (0, i)),
                  full((1, 128)), full((1, 128)), full((h128, 1)), full((h64, 1))],
        out_specs=[pl.BlockSpec((tq, 128), lambda i: (i, 0))] * 4
        + [pl.BlockSpec((h128, tq), lambda i: (0, i))] * 2 + [pl.BlockSpec((h64, tq), lambda i: (0, i))] * 2,
        out_shape=[jax.ShapeDtypeStruct((T, 128), F32)] * 4
        + [jax.ShapeDtypeStruct((h128, T), F32)] * 2 + [jax.ShapeDtypeStruct((h64, T), F32)] * 2,
        compiler_params=_cparams("arbitrary"),
        name="rope_tables",
    )(pos.reshape(T, 1), pos.reshape(1, T), _rope_freq(DSA_HD, 128), _rope_freq(DSA_IDX_DIM, 128),
      _rope_inv(DSA_HD).reshape(h128, 1), _rope_inv(DSA_IDX_DIM).reshape(h64, 1))


def _rope(x, cos, sin, head_dim):
    width = x.shape[-1]
    half = head_dim // ROPE_FRAC // 2
    lane = lax.broadcasted_iota(jnp.int32, (1, width), 1)
    per = lane % head_dim
    sgn_lo = jnp.where(per < half, -1.0, 0.0)
    sgn_hi = jnp.where((per >= half) & (per < 2 * half), 1.0, 0.0)
    up = pltpu.roll(x, width - half, axis=1)
    dn = pltpu.roll(x, half, axis=1)
    return x * cos + sin * (up * sgn_lo + dn * sgn_hi)


def _rope_t(xt, cos, sin):
    half = cos.shape[0]
    x1, x2 = xt[:half], xt[half:2 * half]
    return jnp.concatenate([x1 * cos - x2 * sin, x2 * cos + x1 * sin, xt[2 * half:]], axis=0)


def _dsa_prep_kernel(dq_ref, dk_ref, dv_ref, diq_ref, misc_ref, c128_ref, s128_ref, c64_ref, s64_ref,
                     ct128_ref, st128_ref, ct64_ref, st64_ref, qg_ref, kg_ref,
                     qt_out, k_out, vt_out, iqt_out, ik_out, wt_out):
    ct128, st128 = ct128_ref[...], st128_ref[...]
    ct64, st64 = ct64_ref[...], st64_ref[...]
    for h in range(DSA_HEADS):
        xt = dq_ref[:, h * DSA_HD:(h + 1) * DSA_HD].astype(F32).T
        xt = xt * lax.rsqrt(jnp.mean(xt * xt, axis=0, keepdims=True) + NORM_EPS) * qg_ref[...]
        qt_out[h] = (_rope_t(xt, ct128, st128) * (DSA_HD ** -0.5 * LOG2E)).astype(qt_out.dtype)
    k = _rope(_rms(dk_ref[...].astype(F32), kg_ref[...]), c128_ref[...], s128_ref[...], DSA_HD)
    k_out[...] = k.astype(k_out.dtype)
    for c in range(vt_out.shape[0]):
        vt_out[c] = dv_ref[c * DSA_KCHUNK:(c + 1) * DSA_KCHUNK, :].astype(F32).T.astype(vt_out.dtype)
    for j in range(DSA_IDX_HEADS // 2):
        xt = diq_ref[:, j * 128:(j + 1) * 128].astype(F32).T
        for hh in range(2):
            head = xt[hh * DSA_IDX_DIM:(hh + 1) * DSA_IDX_DIM]
            iqt_out[2 * j + hh] = _rope_t(head, ct64, st64).astype(iqt_out.dtype)
    misc = misc_ref[...].astype(F32)
    ik_out[...] = _rope(misc, c64_ref[...], s64_ref[...], DSA_IDX_DIM)[:, :DSA_IDX_DIM].astype(ik_out.dtype)
    wt_out[...] = misc.T[DIW_LANE:DIW_LANE + DSA_IDX_HEADS]


def _dsa_prep(p, tables, q_gain_col, k_gain):
    T = p.shape[0]
    tq = min(512, T)
    assert tq % DSA_KCHUNK == 0
    tab = pl.BlockSpec((tq, 128), lambda i: (i, 0))
    h128, h64 = DSA_HD // ROPE_FRAC // 2, DSA_IDX_DIM // ROPE_FRAC // 2
    tab128 = pl.BlockSpec((h128, tq), lambda i: (0, i))
    tab64 = pl.BlockSpec((h64, tq), lambda i: (0, i))
    return pl.pallas_call(
        _dsa_prep_kernel,
        grid=(T // tq,),
        in_specs=[pl.BlockSpec((tq, 512), lambda i: (i, _COL["dq"] // 512)),
                  pl.BlockSpec((tq, 128), lambda i: (i, _COL["dk"] // 128)),
                  pl.BlockSpec((tq, 128), lambda i: (i, _COL["dv"] // 128)),
                  pl.BlockSpec((tq, 1024), lambda i: (i, _COL["diq"] // 1024)),
                  pl.BlockSpec((tq, 128), lambda i: (i, MISC_COL // 128)),
                  tab, tab, tab, tab, tab128, tab128, tab64, tab64,
                  pl.BlockSpec((DSA_HD, 1), lambda i: (0, 0)),
                  pl.BlockSpec((1, DSA_HD), lambda i: (0, 0))],
        out_specs=[pl.BlockSpec((DSA_HEADS, DSA_HD, tq), lambda i: (0, 0, i)),
                   pl.BlockSpec((tq, DSA_HD), lambda i: (i, 0)),
                   pl.BlockSpec((tq // DSA_KCHUNK, DSA_HD, DSA_KCHUNK), lambda i: (i, 0, 0)),
                   pl.BlockSpec((DSA_IDX_HEADS, DSA_IDX_DIM, tq), lambda i: (0, 0, i)),
                   pl.BlockSpec((tq, DSA_IDX_DIM), lambda i: (i, 0)),
                   pl.BlockSpec((DSA_IDX_HEADS, tq), lambda i: (0, i))],
        out_shape=[jax.ShapeDtypeStruct((DSA_HEADS, DSA_HD, T), BF16),
                   jax.ShapeDtypeStruct((T, DSA_HD), BF16),
                   jax.ShapeDtypeStruct((T // DSA_KCHUNK, DSA_HD, DSA_KCHUNK), BF16),
                   jax.ShapeDtypeStruct((DSA_IDX_HEADS, DSA_IDX_DIM, T), BF16),
                   jax.ShapeDtypeStruct((T, DSA_IDX_DIM), BF16),
                   jax.ShapeDtypeStruct((DSA_IDX_HEADS, T), F32)],
        compiler_params=_cparams("arbitrary"),
        name="dsa_prep",
    )(p, p, p, p, p, *tables, q_gain_col, k_gain)


def _bit_transpose32(words):
    a = list(words)
    j, m = 16, 0x0000FFFF
    while j:
        k = 0
        while k < 32:
            t = (a[k] ^ lax.shift_right_logical(a[k + j], jnp.int32(j))) & jnp.int32(m)
            a[k] = a[k] ^ t
            a[k + j] = a[k + j] ^ (t << j)
            k = (k + j + 1) & ~j
        j >>= 1
        m = (m ^ (m << j)) & 0xFFFFFFFF
    return a


def _dsa_kernel(iqt_ref, wt_ref, qt_ref, ik_ref, k_ref, vt_ref, o_ref,
                kbuf_ref, plane_ref, valid_ref, m_ref, g_ref, bias_ref, s_ref, p_ref, acc_ref, *, ksel):
    TQ, CK = DSA_QBLOCK, DSA_KCHUNK
    HK = CK // 2
    WORD = 32
    i = pl.program_id(1)
    nk = (i * TQ + TQ + CK - 1) // CK
    t_idx = i * TQ + lax.broadcasted_iota(jnp.int32, (1, TQ), 1)
    w_all = wt_ref[...] * (DSA_IDX_HEADS ** -0.5 * DSA_IDX_DIM ** -0.5)
    valid_ref[...] = jnp.zeros_like(valid_ref)
    sub = lax.broadcasted_iota(jnp.int32, (8, 1), 0)
    off = t_idx - (nk - 1) * CK
    diag_valid = jnp.zeros((8, TQ), jnp.int32)
    for p in range(WORD):
        bit = (1 << p) if p < WORD - 1 else INT_MIN
        diag_valid = diag_valid | jnp.where(8 * p + sub <= off, bit, 0)

    def score_chunk(kc, carry):
        for half in range(2):
            lhs = ik_ref[0, kc, half * HK:(half + 1) * HK, :]
            acc = None
            for h in range(DSA_IDX_HEADS):
                term = w_all[h:h + 1] * jnp.maximum(_dot(lhs, iqt_ref[h]), 0.0)
                acc = term if acc is None else acc + term
            bits = lax.bitcast_convert_type(acc, jnp.int32)
            kbuf_ref[half * HK:(half + 1) * HK, :] = bits ^ ((bits >> 31) | INT_MIN)
        planes = _bit_transpose32([kbuf_ref[8 * (WORD - 1 - j):8 * (WORD - j), :] for j in range(WORD)])
        for p in range(WORD):
            plane_ref[kc, p] = planes[p]
        valid_ref[kc] = jnp.where(kc == nk - 1, diag_valid, -1)
        return carry

    lax.fori_loop(0, nk, score_chunk, 0)

    step = 4 if plane_ref.shape[0] % 4 == 0 else 2
    nstep = (nk + step - 1) // step

    def blank(kc, carry):
        plane_ref[kc] = jnp.zeros(plane_ref.shape[1:], jnp.int32)
        return carry

    lax.fori_loop(nk, nstep * step, blank, 0)

    def total(acc):
        return jnp.sum(acc.astype(F32), axis=0, keepdims=True)

    def first(it, acc):
        for c in range(step):
            kc = it * step + c
            m = valid_ref[kc]
            m_ref[kc] = m
            g_ref[kc] = jnp.zeros_like(m)
            acc = acc + lax.population_count(m & plane_ref[kc, 0])
        return acc

    def narrow(kc, n, take):
        m, g, pp = m_ref[kc], g_ref[kc], plane_ref[kc, n]
        hit = m & pp
        return jnp.where(take, hit, m & ~pp), jnp.where(take, g, g | hit)

    def decide(cnt, need):
        take = cnt >= need
        return take.astype(jnp.int32), jnp.where(take, need, need - cnt)

    zero = jnp.zeros((8, TQ), jnp.int32)
    take0, need0 = decide(total(lax.fori_loop(0, nstep, first, zero)), jnp.full((1, TQ), float(ksel), F32))

    def sweep(n, carry):
        take = carry[0] != 0

        def body(it, acc):
            for c in range(step):
                kc = it * step + c
                m, g = narrow(kc, n - 1, take)
                m_ref[kc] = m
                g_ref[kc] = g
                acc = acc + lax.population_count(m & plane_ref[kc, n])
            return acc

        return decide(total(lax.fori_loop(0, nstep, body, zero)), carry[1])

    take_l, _ = lax.fori_loop(1, WORD, sweep, (take0, need0))

    def finish(it, carry):
        for c in range(step):
            kc = it * step + c
            m, g = narrow(kc, WORD - 1, take_l != 0)
            m_ref[kc] = m | g
        return carry

    lax.fori_loop(0, nstep, finish, 0)

    kg = CK // 16
    unit = 64

    def logits(kc, buf, m_run):
        sel = m_ref[kc]
        for p in range(WORD):
            on = (sel << (WORD - 1 - p)) < 0
            bias_ref[8 * p:8 * (p + 1), :] = jnp.where(on, 0.0, -1e30)
        kk = k_ref[0, kc]
        new = []
        for h in range(DSA_HEADS):
            s = _dot(kk, qt_ref[h]) + bias_ref[...]
            s_ref[buf, h] = s
            mx = jnp.max(s.reshape(kg, 2, 8, TQ), axis=0)
            new.append(jnp.maximum(m_run[h], jnp.max(jnp.max(mx, axis=0), axis=0, keepdims=True)))
        return tuple(new)

    def attn_chunk(kc, carry):
        m_old, m_run, ls = carry
        cur = kc % 2
        vt = vt_ref[0, kc]
        new_l = []
        for h in range(DSA_HEADS):
            alpha = jnp.exp2(m_old[h] - m_run[h])
            mb = jnp.broadcast_to(m_run[h], (8, TQ))[None]
            part = ls[h] * alpha
            for u in range(CK // unit):
                rows = slice(u * unit, (u + 1) * unit)
                pr = jnp.exp2(s_ref[cur, h, rows, :].reshape(unit // 8, 8, TQ) - mb)
                part = part + jnp.sum(pr.reshape(unit // 16, 2, 8, TQ), axis=0)
                p_ref[h, rows, :] = pr.reshape(unit, TQ).astype(BF16)
            new_l.append(part)
            acc_ref[h] = acc_ref[h] * alpha + _dot(vt, p_ref[h])
        m_next = logits(jnp.minimum(kc + 1, nk - 1), 1 - cur, m_run)
        return m_run, m_next, tuple(new_l)

    acc_ref[...] = jnp.zeros_like(acc_ref)
    floor = tuple(jnp.full((1, TQ), -1e30, F32) for _ in range(DSA_HEADS))
    init = (floor, logits(0, 0, floor), tuple(jnp.zeros((2, 8, TQ), F32) for _ in range(DSA_HEADS)))
    _, _, ls = lax.fori_loop(0, nk, attn_chunk, init)
    for h in range(DSA_HEADS):
        denom = jnp.sum(jnp.sum(ls[h], axis=0), axis=0, keepdims=True)
        o_ref[:, h * DSA_HD:(h + 1) * DSA_HD] = (acc_ref[h] / denom).T.astype(o_ref.dtype)


def _dsa(qt, k_rot, vt, iqt, ik_rot, wt, B, S):
    TQ, CK = DSA_QBLOCK, DSA_KCHUNK
    NC = S // CK
    nq = S // TQ
    assert NC % 2 == 0 and TQ == CK == 8 * 32
    ksel = min(DSA_TOPK, S // 4)
    ik = ik_rot.reshape(B, NC, CK, DSA_IDX_DIM)
    k = k_rot.reshape(B, NC, CK, DSA_HD)
    vt = vt.reshape(B, NC, DSA_HD, CK)
    return pl.pallas_call(
        functools.partial(_dsa_kernel, ksel=ksel),
        grid=(B, nq),
        in_specs=[pl.BlockSpec((DSA_IDX_HEADS, DSA_IDX_DIM, TQ), lambda b, i: (0, 0, b * nq + i)),
                  pl.BlockSpec((DSA_IDX_HEADS, TQ), lambda b, i: (0, b * nq + i)),
                  pl.BlockSpec((DSA_HEADS, DSA_HD, TQ), lambda b, i: (0, 0, b * nq + i)),
                  pl.BlockSpec((1, NC, CK, DSA_IDX_DIM), lambda b, i: (b, 0, 0, 0)),
                  pl.BlockSpec((1, NC, CK, DSA_HD), lambda b, i: (b, 0, 0, 0)),
                  pl.BlockSpec((1, NC, DSA_HD, CK), lambda b, i: (b, 0, 0, 0))],
        out_specs=pl.BlockSpec((TQ, DSA_HEADS * DSA_HD), lambda b, i: (b * nq + i, 0)),
        out_shape=jax.ShapeDtypeStruct((B * S, DSA_HEADS * DSA_HD), BF16),
        scratch_shapes=[pltpu.VMEM((CK, TQ), jnp.int32),
                        pltpu.VMEM((NC, 32, 8, TQ), jnp.int32),
                        pltpu.VMEM((NC, 8, TQ), jnp.int32),
                        pltpu.VMEM((NC, 8, TQ), jnp.int32),
                        pltpu.VMEM((NC, 8, TQ), jnp.int32),
                        pltpu.VMEM((CK, TQ), F32),
                        pltpu.VMEM((2, DSA_HEADS, CK, TQ), F32),
                        pltpu.VMEM((DSA_HEADS, CK, TQ), BF16),
                        pltpu.VMEM((DSA_HEADS, DSA_HD, TQ), F32)],
        compiler_params=_cparams("arbitrary", "arbitrary"),
        name="dsa_attn",
    )(iqt, wt, qt, ik, k, vt)


def _mem_kv_kernel(mem_ref, g_ref, w_ref, kg_ref, k_out, v_out):
    h = _rms(mem_ref[...], g_ref[...]).astype(BF16)
    kv = _dot(h, w_ref[...])
    width = MEM_HEADS * MEM_HD
    for hd in range(MEM_HEADS):
        sl = slice(hd * MEM_HD, (hd + 1) * MEM_HD)
        k_out[:, sl] = _rms(kv[:, sl], kg_ref[...]).astype(k_out.dtype)
    v_out[...] = kv[:, width:].astype(v_out.dtype)


def _mem_kv(mem2d, gain, w_kv, k_gain, B, l):
    width = MEM_HEADS * MEM_HD
    out = jax.ShapeDtypeStruct((B * MEM_LEN, width), BF16)
    return pl.pallas_call(
        _mem_kv_kernel,
        grid=(B,),
        in_specs=[pl.BlockSpec((MEM_LEN, D_MODEL), lambda b: (b, 0)),
                  pl.BlockSpec((1, D_MODEL), lambda b: (0, 0)),
                  _layer((D_MODEL, 2 * width), l),
                  pl.BlockSpec((1, MEM_HD), lambda b: (0, 0))],
        out_specs=[pl.BlockSpec((MEM_LEN, width), lambda b: (b, 0))] * 2,
        out_shape=[out, out],
        compiler_params=_cparams("arbitrary"),
        name="mem_kv",
    )(mem2d, gain, w_kv, k_gain)


def _mem_attn_kernel(q_ref, k_ref, v_ref, qg_ref, o_ref):
    for h in range(MEM_HEADS):
        sl = slice(h * MEM_HD, (h + 1) * MEM_HD)
        q = _rms(q_ref[:, sl].astype(F32), qg_ref[...]) * (MEM_HD ** -0.5)
        s = _dot_nt(q.astype(BF16), k_ref[:, sl])
        pr = jnp.exp(s - jnp.max(s, axis=1, keepdims=True))
        o = _dot(pr.astype(BF16), v_ref[:, sl]) / jnp.sum(pr, axis=1, keepdims=True)
        o_ref[:, sl] = o.astype(o_ref.dtype)


def _mem_attn(p, mk, mv, q_gain, B, S):
    tq = min(512, S)
    nq = S // tq
    width = MEM_HEADS * MEM_HD
    return pl.pallas_call(
        _mem_attn_kernel,
        grid=(B, nq),
        in_specs=[pl.BlockSpec((tq, width), lambda b, i: (b * nq + i, _COL["mq"] // width)),
                  pl.BlockSpec((MEM_LEN, width), lambda b, i: (b, 0)),
                  pl.BlockSpec((MEM_LEN, width), lambda b, i: (b, 0)),
                  pl.BlockSpec((1, MEM_HD), lambda b, i: (0, 0))],
        out_specs=pl.BlockSpec((tq, width), lambda b, i: (b * nq + i, 0)),
        out_shape=jax.ShapeDtypeStruct((B * S, width), BF16),
        compiler_params=_cparams("arbitrary", "arbitrary"),
        name="mem_attn",
    )(p, mk, mv, q_gain)


def _out_proj_kernel(og_ref, ys_ref, od_ref, om_ref, z_ref, x_ref, gn_ref, wglu_ref, w_ref, o_ref, lhs_ref):
    y = jax.nn.gelu(jnp.concatenate([ys_ref[q] for q in range(ys_ref.shape[0])], axis=1))
    o_s5 = y * jax.nn.sigmoid(_dot(y.astype(BF16), wglu_ref[...]))
    branches = (og_ref[...].astype(F32), o_s5, od_ref[...].astype(F32), om_ref[...].astype(F32))
    for bi, o in enumerate(branches):
        for h in range(o.shape[1] // HEAD):
            c0 = bi * 512 + h * HEAD
            z = z_ref[:, c0:c0 + HEAD].astype(F32)
            n = _rms(o[:, h * HEAD:(h + 1) * HEAD], gn_ref[:, c0:c0 + HEAD])
            lhs_ref[:, c0:c0 + HEAD] = (n * (z * jax.nn.sigmoid(z))).astype(BF16)
    o_ref[...] = x_ref[...] + _dot(lhs_ref[...], w_ref[...])


def _out_proj(o_gla, y_s5, o_dsa, o_mem, p, x2d, out_gain, w_glu, w_out, l):
    T = x2d.shape[0]
    tm = min(512, T)
    branch = pl.BlockSpec((tm, 512), lambda i: (i, 0))
    return pl.pallas_call(
        _out_proj_kernel,
        grid=(T // tm,),
        in_specs=[branch, pl.BlockSpec((S5_WIDTH // 128, tm, 128), lambda i: (0, i, 0)), branch, branch,
                  pl.BlockSpec((tm, MIX_WIDTH), lambda i: (i, 0)),
                  pl.BlockSpec((tm, D_MODEL), lambda i: (i, 0)),
                  _resident((1, MIX_WIDTH)),
                  _layer((S5_WIDTH, S5_WIDTH), l),
                  _layer((MIX_WIDTH, D_MODEL), l)],
        out_specs=pl.BlockSpec((tm, D_MODEL), lambda i: (i, 0)),
        out_shape=jax.ShapeDtypeStruct((T, D_MODEL), F32),
        scratch_shapes=[pltpu.VMEM((tm, MIX_WIDTH), BF16)],
        compiler_params=_cparams("arbitrary"),
        name="out_proj",
    )(o_gla, y_s5, o_dsa, o_mem, p, x2d, out_gain, w_glu, w_out)


def _permute_kernel(w_ref, o_ref):
    c = 0
    for n in _ORDER:
        lo, hi = _ORIG[n]
        o_ref[c:c + hi - lo, :] = w_ref[lo:hi, :].astype(o_ref.dtype)
        c += hi - lo
    o_ref[c:, :] = jnp.zeros((IN_PAD - c, o_ref.shape[1]), o_ref.dtype)


def _permute_w_in(w_in):
    depth, rows, width = w_in.shape
    tl = 256
    return pl.pallas_call(
        _permute_kernel,
        grid=(depth, rows // tl),
        in_specs=[pl.BlockSpec((None, width, tl), lambda l, i: (l, 0, i))],
        out_specs=pl.BlockSpec((None, IN_PAD, tl), lambda l, i: (l, 0, i)),
        out_shape=jax.ShapeDtypeStruct((depth, IN_PAD, rows), BF16),
        compiler_params=_cparams("arbitrary", "arbitrary"),
        name="permute_w_in",
    )(jnp.swapaxes(w_in, 1, 2))


def kernel(x, mem, positions, ln_gain, w_in, gla_w_up, gla_b_up, s5_a_re, s5_a_im, s5_log_step, s5_b_re, s5_b_im, s5_c_re, s5_c_im, s5_d, s5_w_glu, dsa_q_norm, dsa_k_norm, mem_norm, w_mem_kv, mem_q_norm, mem_k_norm, out_norm, w_out):
    B, S, _ = x.shape
    depth = w_in.shape[0]
    T = B * S
    x2d = x.reshape(T, D_MODEL)
    mem2d = mem.reshape(B * mem.shape[1], D_MODEL)
    w_in_p = _permute_w_in(w_in)
    wup_pad = jnp.zeros((depth, 128, GLA_HEADS * GLA_DK), F32)
    wup_pad = wup_pad.at[:, GLR_LANE:GLR_LANE + GLA_GATE_RANK].set(gla_w_up).astype(BF16)
    w_kv = w_mem_kv.astype(BF16)
    w_glu = s5_w_glu.astype(BF16)
    w_o = w_out.astype(BF16)
    tables = _rope_tables(positions)
    for l in range(depth):
        p, su = _in_proj(x2d, ln_gain[l][None], w_in_p, l)
        o_gla = _gla(p, wup_pad[l], gla_b_up[l][None], B, S)
        mats = _s5_mats(s5_a_re[l], s5_a_im[l], s5_log_step[l], s5_b_re[l], s5_b_im[l],
                        s5_c_re[l], s5_c_im[l], s5_d[l])
        y_s5 = _s5(su, mats, B, S)
        qt, k_rot, vt, iqt, ik_rot, wt = _dsa_prep(p, tables, dsa_q_norm[l][:, None], dsa_k_norm[l][None])
        o_dsa = _dsa(qt, k_rot, vt, iqt, ik_rot, wt, B, S)
        mk, mv = _mem_kv(mem2d, mem_norm[l][None], w_kv, mem_k_norm[l][None], B, l)
        o_mem = _mem_attn(p, mk, mv, mem_q_norm[l][None], B, S)
        x2d = _out_proj(o_gla, y_s5, o_dsa, o_mem, p, x2d, out_norm[l][None], w_glu, w_o, l)
    return x2d.reshape(B, S, D_MODEL)
```

```python
import functools

import numpy as np
import jax
import jax.numpy as jnp
from jax import lax
from jax.experimental import pallas as pl
from jax.experimental.pallas import tpu as pltpu

F32 = jnp.float32
BF16 = jnp.bfloat16

D_MODEL = 2048
MEM_LEN = 256
ROPE_THETA = 500000.0
ROPE_FRAC = 4
NORM_EPS = 1e-6

GLA_HEADS = 4
GLA_DK = 64
GLA_DV = 128
GLA_GATE_RANK = 16
GLA_TAU = 16.0
GLA_CHUNK = 64

S5_GROUPS = 32
S5_CH = 16
S5_STATE = 64
S5_WIDTH = S5_GROUPS * S5_CH
S5_BLOCK = 16

DSA_HEADS = 4
DSA_HD = 128
DSA_IDX_HEADS = 16
DSA_IDX_DIM = 64
DSA_TOPK = 256
DSA_QBLOCK = 256
DSA_KCHUNK = 256

MEM_HEADS = 4
MEM_HD = 128
HEAD = 128
MIX_WIDTH = 2048

VMEM_LIMIT_BYTES = 52 * 1024 * 1024

_ORIG = dict(gq=(0, 256), gk=(256, 512), gv=(512, 1024), glr=(1024, 1040), gz=(1040, 1552),
             su=(1552, 2064), sz=(2064, 2576), dq=(2576, 3088), dk=(3088, 3216), dv=(3216, 3344),
             diq=(3344, 4368), dik=(4368, 4432), diw=(4432, 4448), dz=(4448, 4960),
             mq=(4960, 5472), mz=(5472, 5984))
_ORDER = ("gz", "sz", "dz", "mz", "diq", "gv", "su", "dq", "mq", "gq", "gk", "dk", "dv",
          "dik", "glr", "diw")
_COL = {}
_c = 0
for _n in _ORDER:
    _COL[_n] = _c
    _c += _ORIG[_n][1] - _ORIG[_n][0]
IN_PAD = 6144
MISC_COL = _COL["dik"]
GLR_LANE = 64
DIW_LANE = 80
INT_MIN = -(2 ** 31)
LOG2E = 1.4426950408889634


def _cparams(*sem):
    return pltpu.CompilerParams(dimension_semantics=sem, vmem_limit_bytes=VMEM_LIMIT_BYTES)


def _dot(a, b):
    return jnp.dot(a, b, preferred_element_type=F32)


def _dot_nt(a, b):
    return lax.dot_general(a, b, (((1,), (1,)), ((), ())), preferred_element_type=F32)


def _rms(x, gain):
    return x * lax.rsqrt(jnp.mean(x * x, axis=-1, keepdims=True) + NORM_EPS) * gain


def _in_proj_kernel(x_ref, g_ref, w_ref, o_ref, su_ref, h_ref):
    h_ref[...] = _rms(x_ref[...], g_ref[...]).astype(BF16)
    tn = S5_WIDTH
    for j in range(IN_PAD // tn):
        acc = _dot_nt(h_ref[...], w_ref[j * tn:(j + 1) * tn, :])
        o_ref[:, j * tn:(j + 1) * tn] = acc.astype(o_ref.dtype)
        if j == _COL["su"] // tn:
            for q in range(tn // 128):
                su_ref[q] = acc[:, q * 128:(q + 1) * 128]


def _resident(shape):
    return pl.BlockSpec(shape, lambda *_: (0,) * len(shape), pipeline_mode=pl.Buffered(1))


def _layer(shape, l):
    return pl.BlockSpec((None,) + shape, lambda *_: (l,) + (0,) * len(shape), pipeline_mode=pl.Buffered(1))


def _in_proj(x2d, gain, w, l):
    T = x2d.shape[0]
    tm = min(256, T)
    return pl.pallas_call(
        _in_proj_kernel,
        grid=(T // tm,),
        in_specs=[pl.BlockSpec((tm, D_MODEL), lambda i: (i, 0)),
                  _resident((1, D_MODEL)),
                  _layer((IN_PAD, D_MODEL), l)],
        out_specs=[pl.BlockSpec((tm, IN_PAD), lambda i: (i, 0)),
                   pl.BlockSpec((S5_WIDTH // 128, tm, 128), lambda i: (0, i, 0))],
        out_shape=[jax.ShapeDtypeStruct((T, IN_PAD), BF16),
                   jax.ShapeDtypeStruct((S5_WIDTH // 128, T, 128), F32)],
        scratch_shapes=[pltpu.VMEM((tm, D_MODEL), BF16)],
        compiler_params=_cparams("arbitrary"),
        name="in_proj",
    )(x2d, gain, w)


def _gla_kernel(q_ref, k_ref, v_ref, misc_ref, wup_ref, bup_ref, o_ref, st_ref):
    @pl.when(pl.program_id(1) == 0)
    def _():
        st_ref[...] = jnp.zeros_like(st_ref)

    C = GLA_CHUNK
    tc = q_ref.shape[0]
    nchunk = tc // C
    x = _dot(misc_ref[...], wup_ref[...]) + bup_ref[...]
    log_a = (jnp.minimum(x, 0.0) - jnp.log(1.0 + jnp.exp(-jnp.abs(x)))) * (1.0 / GLA_TAU)
    row = lax.broadcasted_iota(jnp.int32, (tc, tc), 0)
    col = lax.broadcasted_iota(jnp.int32, (tc, tc), 1)
    tril = (row >= col) & ((row // C) == (col // C))
    la_hi = log_a.astype(BF16)
    r1 = log_a - la_hi.astype(F32)
    la_mid = r1.astype(BF16)
    la_lo = (r1 - la_mid.astype(F32)).astype(BF16)
    tri3 = jnp.concatenate([tril.astype(BF16)] * 3, axis=1)
    b = _dot(tri3, jnp.concatenate([la_hi, la_mid, la_lo], axis=0))
    lane = lax.broadcasted_iota(jnp.int32, (1, 2 * GLA_DK), 1)
    low = lane < GLA_DK
    srow = lax.broadcasted_iota(jnp.int32, (2 * GLA_DV, 1), 0)
    diag = (srow < GLA_DV) == low
    for j in range(GLA_HEADS // 2):
        ls = slice(j * 128, (j + 1) * 128)
        bp = b[:, ls]
        mid = jnp.concatenate([jnp.broadcast_to(bp[c * C + C // 2:c * C + C // 2 + 1], (C, 128))
                               for c in range(nchunk)], axis=0)
        last = jnp.concatenate([jnp.broadcast_to(bp[c * C + C - 1:c * C + C], (C, 128))
                                for c in range(nchunk)], axis=0)
        qf = q_ref[:, ls].astype(F32) * (GLA_DK ** -0.5)
        kf = k_ref[:, ls].astype(F32)
        qe = qf * jnp.exp(bp - mid)
        ke_p = (kf * jnp.exp(mid - bp)).astype(BF16)
        kd_p = (kf * jnp.exp(last - bp)).astype(BF16)
        qi_p = (qf * jnp.exp(bp)).astype(BF16)
        intra = []
        for hh in range(2):
            h = 2 * j + hh
            sel = low if hh == 0 else jnp.logical_not(low)
            qe_h = jnp.where(sel, qe, 0.0).astype(BF16)
            attn = jnp.where(tril, _dot_nt(qe_h, ke_p), 0.0)
            intra.append(_dot(attn.astype(BF16), v_ref[:, h * GLA_DV:(h + 1) * GLA_DV]))
        intra = jnp.concatenate(intra, axis=1)
        vs = slice(2 * j * GLA_DV, (2 * j + 2) * GLA_DV)
        st = st_ref[j]
        for c in range(nchunk):
            rs = slice(c * C, (c + 1) * C)
            o = intra[rs] + _dot_nt(qi_p[rs], st.astype(BF16))
            o_ref[rs, vs] = o.astype(o_ref.dtype)
            v_t = v_ref[rs, vs].astype(F32).T.astype(BF16)
            dec = jnp.exp(bp[c * C + C - 1:c * C + C])
            st = jnp.where(diag, st * dec + _dot(v_t, kd_p[rs]), 0.0)
        st_ref[j] = st


def _gla(p, wup_pad, bup, B, S):
    tc = min(256, S)
    nc = S // tc
    row = lambda b, c: b * nc + c
    return pl.pallas_call(
        _gla_kernel,
        grid=(B, nc),
        in_specs=[pl.BlockSpec((tc, 256), lambda b, c: (row(b, c), _COL["gq"] // 256)),
                  pl.BlockSpec((tc, 256), lambda b, c: (row(b, c), _COL["gk"] // 256)),
                  pl.BlockSpec((tc, 512), lambda b, c: (row(b, c), _COL["gv"] // 512)),
                  pl.BlockSpec((tc, 128), lambda b, c: (row(b, c), MISC_COL // 128)),
                  _resident((128, 256)),
                  _resident((1, 256))],
        out_specs=pl.BlockSpec((tc, 512), lambda b, c: (row(b, c), 0)),
        out_shape=jax.ShapeDtypeStruct((B * S, GLA_HEADS * GLA_DV), BF16),
        scratch_shapes=[pltpu.VMEM((GLA_HEADS // 2, 2 * GLA_DV, 2 * GLA_DK), F32)],
        compiler_params=_cparams("arbitrary", "arbitrary"),
        name="gla",
    )(p, p, p, p, wup_pad, bup)


def _s5_mats(a_re, a_im, log_step, b_re, b_im, c_re, c_im, d_skip):
    L, P, CH, G = S5_BLOCK, S5_STATE, S5_CH, S5_GROUPS
    hi = lax.Precision.HIGHEST
    step = jnp.exp(log_step.astype(F32))[:, None]
    are, aim = a_re.astype(F32), a_im.astype(F32)

    def power(j):
        mag = jnp.exp(are[:, None, :] * step[:, None, :] * j[None, :, None])
        ang = aim[:, None, :] * step[:, None, :] * j[None, :, None]
        return mag * jnp.cos(ang), mag * jnp.sin(ang)

    t = jnp.arange(L, dtype=F32)
    lb_re, lb_im = (v[:, 0] for v in power(jnp.ones((1,), F32)))
    den = are * are + aim * aim
    f_re = ((lb_re - 1.0) * are + lb_im * aim) / den
    f_im = (lb_im * are - (lb_re - 1.0) * aim) / den
    bt_re, bt_im = jnp.swapaxes(b_re, 1, 2), jnp.swapaxes(b_im, 1, 2)
    bb_re = f_re[:, None] * bt_re - f_im[:, None] * bt_im
    bb_im = f_re[:, None] * bt_im + f_im[:, None] * bt_re
    n_re, n_im = power(-t)
    a_r = (n_re[:, :, None] * bb_re[:, None] - n_im[:, :, None] * bb_im[:, None]).reshape(G, L * CH, P)
    a_i = (n_re[:, :, None] * bb_im[:, None] + n_im[:, :, None] * bb_re[:, None]).reshape(G, L * CH, P)
    p_re, p_im = power(t)
    ct_re, ct_im = jnp.swapaxes(c_re, 1, 2), jnp.swapaxes(c_im, 1, 2)
    pt_re, pt_im = jnp.swapaxes(p_re, 1, 2), jnp.swapaxes(p_im, 1, 2)
    b_r = (pt_re[..., None] * ct_re[:, :, None] - pt_im[..., None] * ct_im[:, :, None]).reshape(G, P, L * CH)
    b_i = (pt_re[..., None] * ct_im[:, :, None] + pt_im[..., None] * ct_re[:, :, None]).reshape(G, P, L * CH)
    m = (jnp.einsum("gxp,gpy->gxy", a_r, b_r, precision=hi) - jnp.einsum("gxp,gpy->gxy", a_i, b_i, precision=hi))
    blk = np.arange(L * CH) // CH
    m = jnp.where((blk[None, :] >= blk[:, None])[None], m, 0.0)
    m = m + jnp.eye(L * CH, dtype=F32)[None] * jnp.tile(d_skip.reshape(G, 1, CH), (1, 1, L))
    e_re, e_im = (v[:, 0][:, None] for v in power(jnp.full((1,), L - 1.0, F32)))
    pr, pi = a_r * e_re - a_i * e_im, a_r * e_im + a_i * e_re
    pm = jnp.concatenate([pr, pi, pi, pr], axis=-1)
    l_re, l_im = lb_re[..., None], lb_im[..., None]
    qm = jnp.concatenate([b_r * l_re - b_i * l_im, -(b_r * l_im + b_i * l_re)], axis=1)
    al_re, al_im = (v[:, 0] for v in power(jnp.full((1,), float(L), F32)))
    ar = jnp.concatenate([al_re, al_re], axis=-1).reshape(1, G * 2 * P)
    ai_a = jnp.concatenate([-al_im, al_im], axis=-1).reshape(1, G * 2 * P)
    ai_b = jnp.concatenate([al_im, -al_im], axis=-1).reshape(1, G * 2 * P)
    return m.astype(BF16), pm.astype(BF16), qm.astype(BF16), ar, ai_a, ai_b


def _s5_perm():
    src = np.arange(1024).reshape(8, 8, S5_CH).transpose(1, 0, 2).reshape(-1)
    m = np.zeros((1024, 1024), np.float32)
    m[np.arange(1024), src] = 1.0
    return jnp.asarray(m, BF16)


def _s5_kernel(su_ref, perm_ref, m_ref, p_ref, q_ref, ar_ref, aia_ref, aib_ref, y_ref,
               xcat_ref, u_ref, va_ref, vb_ref, x_ref, ycat_ref):
    L = S5_BLOCK
    NB = su_ref.shape[1] // L
    GH = 8
    W = 2 * S5_STATE
    for s in range(L):
        xcat_ref[:, s * 128:(s + 1) * 128] = su_ref[0, pl.ds(s, NB, stride=L), :].astype(BF16)
    for half in range(2):
        u_ref[half] = _dot(xcat_ref[:, half * 1024:(half + 1) * 1024], perm_ref[...]).astype(BF16)

    def u_of(g):
        return jnp.concatenate([u_ref[0, :, g * 128:(g + 1) * 128], u_ref[1, :, g * 128:(g + 1) * 128]], axis=1)

    for g in range(GH):
        v2 = _dot(u_of(g), p_ref[g])
        va_ref[:, g * W:(g + 1) * W] = v2[:, :W]
        vb_ref[:, g * W:(g + 1) * W] = v2[:, W:]
    ar, aia, aib = ar_ref[...], aia_ref[...], aib_ref[...]

    def step(n, carry):
        xa, xb = carry
        x_ref[pl.ds(n, 1), :] = xa
        na = xa * ar + xb * aia + va_ref[pl.ds(n, 1), :]
        nb = xb * ar + xa * aib + vb_ref[pl.ds(n, 1), :]
        return na, nb

    z = jnp.zeros((1, GH * W), F32)
    lax.fori_loop(0, NB, step, (z, z), unroll=8)
    for g in range(GH):
        y = _dot(u_of(g), m_ref[g]) + _dot(x_ref[:, g * W:(g + 1) * W].astype(BF16), q_ref[g])
        for half in range(2):
            ycat_ref[half, :, g * 128:(g + 1) * 128] = y[:, half * 128:(half + 1) * 128].astype(BF16)
    for half in range(2):
        out = _dot(ycat_ref[half], perm_ref[...])
        for t in range(8):
            y_ref[0, pl.ds(half * 8 + t, NB, stride=L), :] = out[:, t * 128:(t + 1) * 128]


def _s5(su, mats, B, S):
    m, pm, qm, ar, aia, aib = mats
    L, G, CH = S5_BLOCK, S5_GROUPS, S5_CH
    NB = S // L
    GH = 8
    W = 2 * S5_STATE
    slab = pl.BlockSpec((1, S, 128), lambda b, h: (h, b, 0))
    coef = pl.BlockSpec((1, GH * W), lambda b, h: (0, h))
    return pl.pallas_call(
        _s5_kernel,
        grid=(B, G // GH),
        in_specs=[slab,
                  pl.BlockSpec((1024, 1024), lambda b, h: (0, 0)),
                  pl.BlockSpec((GH, L * CH, L * CH), lambda b, h: (h, 0, 0)),
                  pl.BlockSpec((GH, L * CH, 2 * W), lambda b, h: (h, 0, 0)),
                  pl.BlockSpec((GH, W, L * CH), lambda b, h: (h, 0, 0)),
                  coef, coef, coef],
        out_specs=slab,
        out_shape=jax.ShapeDtypeStruct((G // GH, B * S, 128), F32),
        scratch_shapes=[pltpu.VMEM((NB, L * 128), BF16), pltpu.VMEM((2, NB, 1024), BF16),
                        pltpu.VMEM((NB, GH * W), F32), pltpu.VMEM((NB, GH * W), F32),
                        pltpu.VMEM((NB, GH * W), F32), pltpu.VMEM((2, NB, 1024), BF16)],
        compiler_params=_cparams("arbitrary", "arbitrary"),
        name="s5_scan",
    )(su, _s5_perm(), m, pm, qm, ar, aia, aib)


def _rope_kernel(pos_ref, prow_ref, f128_ref, f64_ref, i128_ref, i64_ref,
                 c128_ref, s128_ref, c64_ref, s64_ref, ct128_ref, st128_ref, ct64_ref, st64_ref):
    pos = pos_ref[...]
    a = pos * f128_ref[...]
    c128_ref[...] = jnp.cos(a)
    s128_ref[...] = jnp.sin(a)
    a = pos * f64_ref[...]
    c64_ref[...] = jnp.cos(a)
    s64_ref[...] = jnp.sin(a)
    a = i128_ref[...] * prow_ref[...]
    ct128_ref[...] = jnp.cos(a)
    st128_ref[...] = jnp.sin(a)
    a = i64_ref[...] * prow_ref[...]
    ct64_ref[...] = jnp.cos(a)
    st64_ref[...] = jnp.sin(a)


def _rope_inv(d):
    rd = d // ROPE_FRAC
    return ROPE_THETA ** (-jnp.arange(rd // 2, dtype=F32) * 2.0 / rd)


def _rope_freq(d, width):
    inv = _rope_inv(d)
    per_head = jnp.concatenate([inv, inv, jnp.zeros((d - d // ROPE_FRAC,), F32)])
    return jnp.tile(per_head, width // d).reshape(1, width)


def _rope_tables(positions):
    T = positions.size
    pos = positions.astype(F32)
    tq = min(512, T)
    h128, h64 = DSA_HD // ROPE_FRAC // 2, DSA_IDX_DIM // ROPE_FRAC // 2
    full = lambda shape: pl.BlockSpec(shape, lambda i: (0, 0))
    return pl.pallas_call(
        _rope_kernel,
        grid=(T // tq,),
        in_specs=[pl.BlockSpec((tq, 1), lambda i: (i, 0)), pl.BlockSpec((1, tq), lambda i: (0, i)),
                  full((1, 128)), full((1, 128)), full((h128, 1)), full((h64, 1))],
        out_specs=[pl.BlockSpec((tq, 128), lambda i: (i, 0))] * 4
        + [pl.BlockSpec((h128, tq), lambda i: (0, i))] * 2 + [pl.BlockSpec((h64, tq), lambda i: (0, i))] * 2,
        out_shape=[jax.ShapeDtypeStruct((T, 128), F32)] * 4
        + [jax.ShapeDtypeStruct((h128, T), F32)] * 2 + [jax.ShapeDtypeStruct((h64, T), F32)] * 2,
        compiler_params=_cparams("arbitrary"),
        name="rope_tables",
    )(pos.reshape(T, 1), pos.reshape(1, T), _rope_freq(DSA_HD, 128), _rope_freq(DSA_IDX_DIM, 128),
      _rope_inv(DSA_HD).reshape(h128, 1), _rope_inv(DSA_IDX_DIM).reshape(h64, 1))


def _rope(x, cos, sin, head_dim):
    width = x.shape[-1]
    half = head_dim // ROPE_FRAC // 2
    lane = lax.broadcasted_iota(jnp.int32, (1, width), 1)
    per = lane % head_dim
    sgn_lo = jnp.where(per < half, -1.0, 0.0)
    sgn_hi = jnp.where((per >= half) & (per < 2 * half), 1.0, 0.0)
    up = pltpu.roll(x, width - half, axis=1)
    dn = pltpu.roll(x, half, axis=1)
    return x * cos + sin * (up * sgn_lo + dn * sgn_hi)


def _rope_t(xt, cos, sin):
    half = cos.shape[0]
    x1, x2 = xt[:half], xt[half:2 * half]
    return jnp.concatenate([x1 * cos - x2 * sin, x2 * cos + x1 * sin, xt[2 * half:]], axis=0)


def _dsa_prep_kernel(dq_ref, dk_ref, dv_ref, diq_ref, misc_ref, c128_ref, s128_ref, c64_ref, s64_ref,
                     ct128_ref, st128_ref, ct64_ref, st64_ref, qg_ref, kg_ref,
                     qt_out, k_out, vt_out, iqt_out, ik_out, wt_out):
    ct128, st128 = ct128_ref[...], st128_ref[...]
    ct64, st64 = ct64_ref[...], st64_ref[...]
    for h in range(DSA_HEADS):
        xt = dq_ref[:, h * DSA_HD:(h + 1) * DSA_HD].astype(F32).T
        xt = xt * lax.rsqrt(jnp.mean(xt * xt, axis=0, keepdims=True) + NORM_EPS) * qg_ref[...]
        qt_out[h] = (_rope_t(xt, ct128, st128) * (DSA_HD ** -0.5 * LOG2E)).astype(qt_out.dtype)
    k = _rope(_rms(dk_ref[...].astype(F32), kg_ref[...]), c128_ref[...], s128_ref[...], DSA_HD)
    k_out[...] = k.astype(k_out.dtype)
    for c in range(vt_out.shape[0]):
        vt_out[c] = dv_ref[c * DSA_KCHUNK:(c + 1) * DSA_KCHUNK, :].astype(F32).T.astype(vt_out.dtype)
    for j in range(DSA_IDX_HEADS // 2):
        xt = diq_ref[:, j * 128:(j + 1) * 128].astype(F32).T
        for hh in range(2):
            head = xt[hh * DSA_IDX_DIM:(hh + 1) * DSA_IDX_DIM]
            iqt_out[2 * j + hh] = _rope_t(head, ct64, st64).astype(iqt_out.dtype)
    misc = misc_ref[...].astype(F32)
    ik_out[...] = _rope(misc, c64_ref[...], s64_ref[...], DSA_IDX_DIM)[:, :DSA_IDX_DIM].astype(ik_out.dtype)
    wt_out[...] = misc.T[DIW_LANE:DIW_LANE + DSA_IDX_HEADS]


def _dsa_prep(p, tables, q_gain_col, k_gain):
    T = p.shape[0]
    tq = min(512, T)
    assert tq % DSA_KCHUNK == 0
    tab = pl.BlockSpec((tq, 128), lambda i: (i, 0))
    h128, h64 = DSA_HD // ROPE_FRAC // 2, DSA_IDX_DIM // ROPE_FRAC // 2
    tab128 = pl.BlockSpec((h128, tq), lambda i: (0, i))
    tab64 = pl.BlockSpec((h64, tq), lambda i: (0, i))
    return pl.pallas_call(
        _dsa_prep_kernel,
        grid=(T // tq,),
        in_specs=[pl.BlockSpec((tq, 512), lambda i: (i, _COL["dq"] // 512)),
                  pl.BlockSpec((tq, 128), lambda i: (i, _COL["dk"] // 128)),
                  pl.BlockSpec((tq, 128), lambda i: (i, _COL["dv"] // 128)),
                  pl.BlockSpec((tq, 1024), lambda i: (i, _COL["diq"] // 1024)),
                  pl.BlockSpec((tq, 128), lambda i: (i, MISC_COL // 128)),
                  tab, tab, tab, tab, tab128, tab128, tab64, tab64,
                  pl.BlockSpec((DSA_HD, 1), lambda i: (0, 0)),
                  pl.BlockSpec((1, DSA_HD), lambda i: (0, 0))],
        out_specs=[pl.BlockSpec((DSA_HEADS, DSA_HD, tq), lambda i: (0, 0, i)),
                   pl.BlockSpec((tq, DSA_HD), lambda i: (i, 0)),
                   pl.BlockSpec((tq // DSA_KCHUNK, DSA_HD, DSA_KCHUNK), lambda i: (i, 0, 0)),
                   pl.BlockSpec((DSA_IDX_HEADS, DSA_IDX_DIM, tq), lambda i: (0, 0, i)),
                   pl.BlockSpec((tq, DSA_IDX_DIM), lambda i: (i, 0)),
                   pl.BlockSpec((DSA_IDX_HEADS, tq), lambda i: (0, i))],
        out_shape=[jax.ShapeDtypeStruct((DSA_HEADS, DSA_HD, T), BF16),
                   jax.ShapeDtypeStruct((T, DSA_HD), BF16),
                   jax.ShapeDtypeStruct((T // DSA_KCHUNK, DSA_HD, DSA_KCHUNK), BF16),
                   jax.ShapeDtypeStruct((DSA_IDX_HEADS, DSA_IDX_DIM, T), BF16),
                   jax.ShapeDtypeStruct((T, DSA_IDX_DIM), BF16),
                   jax.ShapeDtypeStruct((DSA_IDX_HEADS, T), F32)],
        compiler_params=_cparams("arbitrary"),
        name="dsa_prep",
    )(p, p, p, p, p, *tables, q_gain_col, k_gain)


_SWAP_MASKS = {16: 0x0000FFFF, 8: 0x00FF00FF, 4: 0x0F0F0F0F, 2: 0x33333333, 1: 0x55555555}


def _bit_swap_stage(a, j):
    a = list(a)
    k = 0
    while k < len(a):
        t = (a[k] ^ lax.shift_right_logical(a[k + j], jnp.int32(j))) & jnp.int32(_SWAP_MASKS[j])
        a[k] = a[k] ^ t
        a[k + j] = a[k + j] ^ (t << j)
        k = (k + j + 1) & ~j
    return a


def _dsa_kernel(iqt_ref, wt_ref, qt_ref, ik_ref, k_ref, vt_ref, o_ref,
                plane_ref, valid_ref, m_ref, g_ref, bias_ref, s_ref, p_ref, acc_ref, *, ksel):
    TQ, CK = DSA_QBLOCK, DSA_KCHUNK
    WORD = 32
    i = pl.program_id(1)
    nk = (i * TQ + TQ + CK - 1) // CK
    t_idx = i * TQ + lax.broadcasted_iota(jnp.int32, (1, TQ), 1)
    w_all = wt_ref[...] * (DSA_IDX_HEADS ** -0.5 * DSA_IDX_DIM ** -0.5)
    valid_ref[...] = jnp.zeros_like(valid_ref)
    sub = lax.broadcasted_iota(jnp.int32, (8, 1), 0)
    off = t_idx - (nk - 1) * CK
    diag_valid = jnp.zeros((8, TQ), jnp.int32)
    for p in range(WORD):
        bit = (1 << p) if p < WORD - 1 else INT_MIN
        diag_valid = diag_valid | jnp.where(8 * p + sub <= off, bit, 0)

    def score_chunk(kc, carry):
        QK = CK // 4
        quarters = []
        for part in range(4):
            lhs = ik_ref[0, kc, part * QK:(part + 1) * QK, :]
            acc = None
            for h in range(DSA_IDX_HEADS):
                term = w_all[h:h + 1] * jnp.maximum(_dot(lhs, iqt_ref[h]), 0.0)
                acc = term if acc is None else acc + term
            bits = lax.bitcast_convert_type(acc, jnp.int32)
            ukey = bits ^ ((bits >> 31) | INT_MIN)
            words = [ukey[8 * (QK // 8 - 1 - j):8 * (QK // 8 - j)] for j in range(QK // 8)]
            for j in (4, 2, 1):
                words = _bit_swap_stage(words, j)
            quarters.append(words)
        halves = [_bit_swap_stage(quarters[1] + quarters[0], 8), _bit_swap_stage(quarters[3] + quarters[2], 8)]
        planes = _bit_swap_stage(halves[1] + halves[0], 16)
        for p in range(WORD):
            plane_ref[kc, p] = planes[p]
        valid_ref[kc] = jnp.where(kc == nk - 1, diag_valid, -1)
        return carry

    lax.fori_loop(0, nk, score_chunk, 0)

    step = 4 if plane_ref.shape[0] % 4 == 0 else 2
    nstep = (nk + step - 1) // step

    def blank(kc, carry):
        plane_ref[kc] = jnp.zeros(plane_ref.shape[1:], jnp.int32)
        return carry

    lax.fori_loop(nk, nstep * step, blank, 0)

    def total(acc):
        return jnp.sum(acc.astype(F32), axis=0, keepdims=True)

    def first(it, acc):
        for c in range(step):
            kc = it * step + c
            m = valid_ref[kc]
            m_ref[kc] = m
            g_ref[kc] = jnp.zeros_like(m)
            acc = acc + lax.population_count(m & plane_ref[kc, 0])
        return acc

    def narrow(kc, n, take):
        m, g, pp = m_ref[kc], g_ref[kc], plane_ref[kc, n]
        hit = m & pp
        return jnp.where(take, hit, m & ~pp), jnp.where(take, g, g | hit)

    def decide(cnt, need):
        take = cnt >= need
        return take.astype(jnp.int32), jnp.where(take, need, need - cnt)

    zero = jnp.zeros((8, TQ), jnp.int32)
    take0, need0 = decide(total(lax.fori_loop(0, nstep, first, zero)), jnp.full((1, TQ), float(ksel), F32))

    def sweep(n, carry):
        take = carry[0] != 0

        def body(it, acc):
            for c in range(step):
                kc = it * step + c
                m, g = narrow(kc, n - 1, take)
                m_ref[kc] = m
                g_ref[kc] = g
                acc = acc + lax.population_count(m & plane_ref[kc, n])
            return acc

        return decide(total(lax.fori_loop(0, nstep, body, zero)), carry[1])

    take_l, _ = lax.fori_loop(1, WORD, sweep, (take0, need0))

    def finish(it, carry):
        for c in range(step):
            kc = it * step + c
            m, g = narrow(kc, WORD - 1, take_l != 0)
            m_ref[kc] = m | g
        return carry

    lax.fori_loop(0, nstep, finish, 0)

    kg = CK // 16
    unit = 64

    def logits(kc, buf, m_run):
        sel = m_ref[kc]
        for p in range(WORD):
            on = (sel << (WORD - 1 - p)) < 0
            bias_ref[8 * p:8 * (p + 1), :] = jnp.where(on, 0.0, -1e30)
        kk = k_ref[0, kc]
        new = []
        for h in range(DSA_HEADS):
            s = _dot(kk, qt_ref[h]) + bias_ref[...]
            s_ref[buf, h] = s
            mx = jnp.max(s.reshape(kg, 2, 8, TQ), axis=0)
            new.append(jnp.maximum(m_run[h], jnp.max(jnp.max(mx, axis=0), axis=0, keepdims=True)))
        return tuple(new)

    def attn_chunk(kc, carry):
        m_old, m_run, ls = carry
        cur = kc % 2
        vt = vt_ref[0, kc]
        new_l = []
        for h in range(DSA_HEADS):
            alpha = jnp.exp2(m_old[h] - m_run[h])
            mb = jnp.broadcast_to(m_run[h], (8, TQ))[None]
            part = ls[h] * alpha
            for u in range(CK // unit):
                rows = slice(u * unit, (u + 1) * unit)
                pr = jnp.exp2(s_ref[cur, h, rows, :].reshape(unit // 8, 8, TQ) - mb)
                part = part + jnp.sum(pr.reshape(unit // 16, 2, 8, TQ), axis=0)
                p_ref[h, rows, :] = pr.reshape(unit, TQ).astype(BF16)
            new_l.append(part)
            acc_ref[h] = acc_ref[h] * alpha + _dot(vt, p_ref[h])
        m_next = logits(jnp.minimum(kc + 1, nk - 1), 1 - cur, m_run)
        return m_run, m_next, tuple(new_l)

    acc_ref[...] = jnp.zeros_like(acc_ref)
    floor = tuple(jnp.full((1, TQ), -1e30, F32) for _ in range(DSA_HEADS))
    init = (floor, logits(0, 0, floor), tuple(jnp.zeros((2, 8, TQ), F32) for _ in range(DSA_HEADS)))
    _, _, ls = lax.fori_loop(0, nk, attn_chunk, init)
    for h in range(DSA_HEADS):
        denom = jnp.sum(jnp.sum(ls[h], axis=0), axis=0, keepdims=True)
        o_ref[:, h * DSA_HD:(h + 1) * DSA_HD] = (acc_ref[h] / denom).T.astype(o_ref.dtype)


def _dsa(qt, k_rot, vt, iqt, ik_rot, wt, B, S):
    TQ, CK = DSA_QBLOCK, DSA_KCHUNK
    NC = S // CK
    nq = S // TQ
    assert NC % 2 == 0 and TQ == CK == 8 * 32
    ksel = min(DSA_TOPK, S // 4)
    ik = ik_rot.reshape(B, NC, CK, DSA_IDX_DIM)
    k = k_rot.reshape(B, NC, CK, DSA_HD)
    vt = vt.reshape(B, NC, DSA_HD, CK)
    return pl.pallas_call(
        functools.partial(_dsa_kernel, ksel=ksel),
        grid=(B, nq),
        in_specs=[pl.BlockSpec((DSA_IDX_HEADS, DSA_IDX_DIM, TQ), lambda b, i: (0, 0, b * nq + i)),
                  pl.BlockSpec((DSA_IDX_HEADS, TQ), lambda b, i: (0, b * nq + i)),
                  pl.BlockSpec((DSA_HEADS, DSA_HD, TQ), lambda b, i: (0, 0, b * nq + i)),
                  pl.BlockSpec((1, NC, CK, DSA_IDX_DIM), lambda b, i: (b, 0, 0, 0)),
                  pl.BlockSpec((1, NC, CK, DSA_HD), lambda b, i: (b, 0, 0, 0)),
                  pl.BlockSpec((1, NC, DSA_HD, CK), lambda b, i: (b, 0, 0, 0))],
        out_specs=pl.BlockSpec((TQ, DSA_HEADS * DSA_HD), lambda b, i: (b * nq + i, 0)),
        out_shape=jax.ShapeDtypeStruct((B * S, DSA_HEADS * DSA_HD), BF16),
        scratch_shapes=[pltpu.VMEM((NC, 32, 8, TQ), jnp.int32),
                        pltpu.VMEM((NC, 8, TQ), jnp.int32),
                        pltpu.VMEM((NC, 8, TQ), jnp.int32),
                        pltpu.VMEM((NC, 8, TQ), jnp.int32),
                        pltpu.VMEM((CK, TQ), F32),
                        pltpu.VMEM((2, DSA_HEADS, CK, TQ), F32),
                        pltpu.VMEM((DSA_HEADS, CK, TQ), BF16),
                        pltpu.VMEM((DSA_HEADS, DSA_HD, TQ), F32)],
        compiler_params=_cparams("arbitrary", "arbitrary"),
        name="dsa_attn",
    )(iqt, wt, qt, ik, k, vt)


def _mem_kv_kernel(mem_ref, g_ref, w_ref, kg_ref, k_out, v_out):
    h = _rms(mem_ref[...], g_ref[...]).astype(BF16)
    kv = _dot(h, w_ref[...])
    width = MEM_HEADS * MEM_HD
    for hd in range(MEM_HEADS):
        sl = slice(hd * MEM_HD, (hd + 1) * MEM_HD)
        k_out[:, sl] = _rms(kv[:, sl], kg_ref[...]).astype(k_out.dtype)
    v_out[...] = kv[:, width:].astype(v_out.dtype)


def _mem_kv(mem2d, gain, w_kv, k_gain, B, l):
    width = MEM_HEADS * MEM_HD
    out = jax.ShapeDtypeStruct((B * MEM_LEN, width), BF16)
    return pl.pallas_call(
        _mem_kv_kernel,
        grid=(B,),
        in_specs=[pl.BlockSpec((MEM_LEN, D_MODEL), lambda b: (b, 0)),
                  pl.BlockSpec((1, D_MODEL), lambda b: (0, 0)),
                  _layer((D_MODEL, 2 * width), l),
                  pl.BlockSpec((1, MEM_HD), lambda b: (0, 0))],
        out_specs=[pl.BlockSpec((MEM_LEN, width), lambda b: (b, 0))] * 2,
        out_shape=[out, out],
        compiler_params=_cparams("arbitrary"),
        name="mem_kv",
    )(mem2d, gain, w_kv, k_gain)


def _mem_attn_kernel(q_ref, k_ref, v_ref, qg_ref, o_ref):
    for h in range(MEM_HEADS):
        sl = slice(h * MEM_HD, (h + 1) * MEM_HD)
        q = _rms(q_ref[:, sl].astype(F32), qg_ref[...]) * (MEM_HD ** -0.5)
        s = _dot_nt(q.astype(BF16), k_ref[:, sl])
        pr = jnp.exp(s - jnp.max(s, axis=1, keepdims=True))
        o = _dot(pr.astype(BF16), v_ref[:, sl]) / jnp.sum(pr, axis=1, keepdims=True)
        o_ref[:, sl] = o.astype(o_ref.dtype)


def _mem_attn(p, mk, mv, q_gain, B, S):
    tq = min(512, S)
    nq = S // tq
    width = MEM_HEADS * MEM_HD
    return pl.pallas_call(
        _mem_attn_kernel,
        grid=(B, nq),
        in_specs=[pl.BlockSpec((tq, width), lambda b, i: (b * nq + i, _COL["mq"] // width)),
                  pl.BlockSpec((MEM_LEN, width), lambda b, i: (b, 0)),
                  pl.BlockSpec((MEM_LEN, width), lambda b, i: (b, 0)),
                  pl.BlockSpec((1, MEM_HD), lambda b, i: (0, 0))],
        out_specs=pl.BlockSpec((tq, width), lambda b, i: (b * nq + i, 0)),
        out_shape=jax.ShapeDtypeStruct((B * S, width), BF16),
        compiler_params=_cparams("arbitrary", "arbitrary"),
        name="mem_attn",
    )(p, mk, mv, q_gain)


def _out_proj_kernel(og_ref, ys_ref, od_ref, om_ref, z_ref, x_ref, gn_ref, wglu_ref, w_ref, o_ref, lhs_ref):
    y = jax.nn.gelu(jnp.concatenate([ys_ref[q] for q in range(ys_ref.shape[0])], axis=1))
    o_s5 = y * jax.nn.sigmoid(_dot(y.astype(BF16), wglu_ref[...]))
    branches = (og_ref[...].astype(F32), o_s5, od_ref[...].astype(F32), om_ref[...].astype(F32))
    for bi, o in enumerate(branches):
        for h in range(o.shape[1] // HEAD):
            c0 = bi * 512 + h * HEAD
            z = z_ref[:, c0:c0 + HEAD].astype(F32)
            n = _rms(o[:, h * HEAD:(h + 1) * HEAD], gn_ref[:, c0:c0 + HEAD])
            lhs_ref[:, c0:c0 + HEAD] = (n * (z * jax.nn.sigmoid(z))).astype(BF16)
    o_ref[...] = x_ref[...] + _dot(lhs_ref[...], w_ref[...])


def _out_proj(o_gla, y_s5, o_dsa, o_mem, p, x2d, out_gain, w_glu, w_out, l):
    T = x2d.shape[0]
    tm = min(512, T)
    branch = pl.BlockSpec((tm, 512), lambda i: (i, 0))
    return pl.pallas_call(
        _out_proj_kernel,
        grid=(T // tm,),
        in_specs=[branch, pl.BlockSpec((S5_WIDTH // 128, tm, 128), lambda i: (0, i, 0)), branch, branch,
                  pl.BlockSpec((tm, MIX_WIDTH), lambda i: (i, 0)),
                  pl.BlockSpec((tm, D_MODEL), lambda i: (i, 0)),
                  _resident((1, MIX_WIDTH)),
                  _layer((S5_WIDTH, S5_WIDTH), l),
                  _layer((MIX_WIDTH, D_MODEL), l)],
        out_specs=pl.BlockSpec((tm, D_MODEL), lambda i: (i, 0)),
        out_shape=jax.ShapeDtypeStruct((T, D_MODEL), F32),
        scratch_shapes=[pltpu.VMEM((tm, MIX_WIDTH), BF16)],
        compiler_params=_cparams("arbitrary"),
        name="out_proj",
    )(o_gla, y_s5, o_dsa, o_mem, p, x2d, out_gain, w_glu, w_out)


def _permute_kernel(w_ref, o_ref):
    c = 0
    for n in _ORDER:
        lo, hi = _ORIG[n]
        o_ref[c:c + hi - lo, :] = w_ref[lo:hi, :].astype(o_ref.dtype)
        c += hi - lo
    o_ref[c:, :] = jnp.zeros((IN_PAD - c, o_ref.shape[1]), o_ref.dtype)


def _permute_w_in(w_in):
    depth, rows, width = w_in.shape
    tl = 256
    return pl.pallas_call(
        _permute_kernel,
        grid=(depth, rows // tl),
        in_specs=[pl.BlockSpec((None, width, tl), lambda l, i: (l, 0, i))],
        out_specs=pl.BlockSpec((None, IN_PAD, tl), lambda l, i: (l, 0, i)),
        out_shape=jax.ShapeDtypeStruct((depth, IN_PAD, rows), BF16),
        compiler_params=_cparams("arbitrary", "arbitrary"),
        name="permute_w_in",
    )(jnp.swapaxes(w_in, 1, 2))


def kernel(x, mem, positions, ln_gain, w_in, gla_w_up, gla_b_up, s5_a_re, s5_a_im, s5_log_step, s5_b_re, s5_b_im, s5_c_re, s5_c_im, s5_d, s5_w_glu, dsa_q_norm, dsa_k_norm, mem_norm, w_mem_kv, mem_q_norm, mem_k_norm, out_norm, w_out):
    B, S, _ = x.shape
    depth = w_in.shape[0]
    T = B * S
    x2d = x.reshape(T, D_MODEL)
    mem2d = mem.reshape(B * mem.shape[1], D_MODEL)
    w_in_p = _permute_w_in(w_in)
    wup_pad = jnp.zeros((depth, 128, GLA_HEADS * GLA_DK), F32)
    wup_pad = wup_pad.at[:, GLR_LANE:GLR_LANE + GLA_GATE_RANK].set(gla_w_up).astype(BF16)
    w_kv = w_mem_kv.astype(BF16)
    w_glu = s5_w_glu.astype(BF16)
    w_o = w_out.astype(BF16)
    tables = _rope_tables(positions)
    for l in range(depth):
        p, su = _in_proj(x2d, ln_gain[l][None], w_in_p, l)
        o_gla = _gla(p, wup_pad[l], gla_b_up[l][None], B, S)
        mats = _s5_mats(s5_a_re[l], s5_a_im[l], s5_log_step[l], s5_b_re[l], s5_b_im[l],
                        s5_c_re[l], s5_c_im[l], s5_d[l])
        y_s5 = _s5(su, mats, B, S)
        qt, k_rot, vt, iqt, ik_rot, wt = _dsa_prep(p, tables, dsa_q_norm[l][:, None], dsa_k_norm[l][None])
        o_dsa = _dsa(qt, k_rot, vt, iqt, ik_rot, wt, B, S)
        mk, mv = _mem_kv(mem2d, mem_norm[l][None], w_kv, mem_k_norm[l][None], B, l)
        o_mem = _mem_attn(p, mk, mv, mem_q_norm[l][None], B, S)
        x2d = _out_proj(o_gla, y_s5, o_dsa, o_mem, p, x2d, out_norm[l][None], w_glu, w_o, l)
    return x2d.reshape(B, S, D_MODEL)
```

```python
import functools

import numpy as np
import jax
import jax.numpy as jnp
from jax import lax
from jax.experimental import pallas as pl
from jax.experimental.pallas import tpu as pltpu

F32 = jnp.float32
BF16 = jnp.bfloat16

D_MODEL = 2048
MEM_LEN = 256
ROPE_THETA = 500000.0
ROPE_FRAC = 4
NORM_EPS = 1e-6

GLA_HEADS = 4
GLA_DK = 64
GLA_DV = 128
GLA_GATE_RANK = 16
GLA_TAU = 16.0
GLA_CHUNK = 64

S5_GROUPS = 32
S5_CH = 16
S5_STATE = 64
S5_WIDTH = S5_GROUPS * S5_CH
S5_BLOCK = 16

DSA_HEADS = 4
DSA_HD = 128
DSA_IDX_HEADS = 16
DSA_IDX_DIM = 64
DSA_TOPK = 256
DSA_QBLOCK = 256
DSA_KCHUNK = 256

MEM_HEADS = 4
MEM_HD = 128
HEAD = 128
MIX_WIDTH = 2048

VMEM_LIMIT_BYTES = 52 * 1024 * 1024

_ORIG = dict(gq=(0, 256), gk=(256, 512), gv=(512, 1024), glr=(1024, 1040), gz=(1040, 1552),
             su=(1552, 2064), sz=(2064, 2576), dq=(2576, 3088), dk=(3088, 3216), dv=(3216, 3344),
             diq=(3344, 4368), dik=(4368, 4432), diw=(4432, 4448), dz=(4448, 4960),
             mq=(4960, 5472), mz=(5472, 5984))
_ORDER = ("gz", "sz", "dz", "mz", "diq", "gv", "su", "dq", "mq", "gq", "gk", "dk", "dv",
          "dik", "glr", "diw")
_COL = {}
_c = 0
for _n in _ORDER:
    _COL[_n] = _c
    _c += _ORIG[_n][1] - _ORIG[_n][0]
IN_PAD = 6144
MISC_COL = _COL["dik"]
GLR_LANE = 64
DIW_LANE = 80
INT_MIN = -(2 ** 31)
LOG2E = 1.4426950408889634


def _cparams(*sem):
    return pltpu.CompilerParams(dimension_semantics=sem, vmem_limit_bytes=VMEM_LIMIT_BYTES)


def _dot(a, b):
    return jnp.dot(a, b, preferred_element_type=F32)


def _dot_nt(a, b):
    return lax.dot_general(a, b, (((1,), (1,)), ((), ())), preferred_element_type=F32)


def _rms(x, gain):
    return x * lax.rsqrt(jnp.mean(x * x, axis=-1, keepdims=True) + NORM_EPS) * gain


def _in_proj_kernel(x_ref, g_ref, w_ref, o_ref, su_ref, h_ref):
    h_ref[...] = _rms(x_ref[...], g_ref[...]).astype(BF16)
    tn = S5_WIDTH
    for j in range(IN_PAD // tn):
        acc = _dot_nt(h_ref[...], w_ref[j * tn:(j + 1) * tn, :])
        o_ref[:, j * tn:(j + 1) * tn] = acc.astype(o_ref.dtype)
        if j == _COL["su"] // tn:
            for q in range(tn // 128):
                su_ref[q] = acc[:, q * 128:(q + 1) * 128]


def _resident(shape):
    return pl.BlockSpec(shape, lambda *_: (0,) * len(shape), pipeline_mode=pl.Buffered(1))


def _layer(shape, l):
    return pl.BlockSpec((None,) + shape, lambda *_: (l,) + (0,) * len(shape), pipeline_mode=pl.Buffered(1))


def _in_proj(x2d, gain, w, l):
    T = x2d.shape[0]
    tm = min(256, T)
    return pl.pallas_call(
        _in_proj_kernel,
        grid=(T // tm,),
        in_specs=[pl.BlockSpec((tm, D_MODEL), lambda i: (i, 0)),
                  _resident((1, D_MODEL)),
                  _layer((IN_PAD, D_MODEL), l)],
        out_specs=[pl.BlockSpec((tm, IN_PAD), lambda i: (i, 0)),
                   pl.BlockSpec((S5_WIDTH // 128, tm, 128), lambda i: (0, i, 0))],
        out_shape=[jax.ShapeDtypeStruct((T, IN_PAD), BF16),
                   jax.ShapeDtypeStruct((S5_WIDTH // 128, T, 128), F32)],
        scratch_shapes=[pltpu.VMEM((tm, D_MODEL), BF16)],
        compiler_params=_cparams("arbitrary"),
        name="in_proj",
    )(x2d, gain, w)


def _gla_kernel(q_ref, k_ref, v_ref, misc_ref, wup_ref, bup_ref, o_ref, st_ref):
    @pl.when(pl.program_id(1) == 0)
    def _():
        st_ref[...] = jnp.zeros_like(st_ref)

    C = GLA_CHUNK
    tc = q_ref.shape[0]
    nchunk = tc // C
    x = _dot(misc_ref[...], wup_ref[...]) + bup_ref[...]
    log_a = (jnp.minimum(x, 0.0) - jnp.log(1.0 + jnp.exp(-jnp.abs(x)))) * (1.0 / GLA_TAU)
    row = lax.broadcasted_iota(jnp.int32, (tc, tc), 0)
    col = lax.broadcasted_iota(jnp.int32, (tc, tc), 1)
    tril = (row >= col) & ((row // C) == (col // C))
    la_hi = log_a.astype(BF16)
    r1 = log_a - la_hi.astype(F32)
    la_mid = r1.astype(BF16)
    la_lo = (r1 - la_mid.astype(F32)).astype(BF16)
    tri3 = jnp.concatenate([tril.astype(BF16)] * 3, axis=1)
    b = _dot(tri3, jnp.concatenate([la_hi, la_mid, la_lo], axis=0))
    lane = lax.broadcasted_iota(jnp.int32, (1, 2 * GLA_DK), 1)
    low = lane < GLA_DK
    srow = lax.broadcasted_iota(jnp.int32, (2 * GLA_DV, 1), 0)
    diag = (srow < GLA_DV) == low
    for j in range(GLA_HEADS // 2):
        ls = slice(j * 128, (j + 1) * 128)
        bp = b[:, ls]
        mid = jnp.concatenate([jnp.broadcast_to(bp[c * C + C // 2:c * C + C // 2 + 1], (C, 128))
                               for c in range(nchunk)], axis=0)
        last = jnp.concatenate([jnp.broadcast_to(bp[c * C + C - 1:c * C + C], (C, 128))
                                for c in range(nchunk)], axis=0)
        qf = q_ref[:, ls].astype(F32) * (GLA_DK ** -0.5)
        kf = k_ref[:, ls].astype(F32)
        qe = qf * jnp.exp(bp - mid)
        ke_p = (kf * jnp.exp(mid - bp)).astype(BF16)
        kd_p = (kf * jnp.exp(last - bp)).astype(BF16)
        qi_p = (qf * jnp.exp(bp)).astype(BF16)
        intra = []
        for hh in range(2):
            h = 2 * j + hh
            sel = low if hh == 0 else jnp.logical_not(low)
            qe_h = jnp.where(sel, qe, 0.0).astype(BF16)
            attn = jnp.where(tril, _dot_nt(qe_h, ke_p), 0.0)
            intra.append(_dot(attn.astype(BF16), v_ref[:, h * GLA_DV:(h + 1) * GLA_DV]))
        intra = jnp.concatenate(intra, axis=1)
        vs = slice(2 * j * GLA_DV, (2 * j + 2) * GLA_DV)
        st = st_ref[j]
        for c in range(nchunk):
            rs = slice(c * C, (c + 1) * C)
            o = intra[rs] + _dot_nt(qi_p[rs], st.astype(BF16))
            o_ref[rs, vs] = o.astype(o_ref.dtype)
            v_t = v_ref[rs, vs].astype(F32).T.astype(BF16)
            dec = jnp.exp(bp[c * C + C - 1:c * C + C])
            st = jnp.where(diag, st * dec + _dot(v_t, kd_p[rs]), 0.0)
        st_ref[j] = st


def _gla(p, wup_pad, bup, B, S):
    tc = min(256, S)
    nc = S // tc
    row = lambda b, c: b * nc + c
    return pl.pallas_call(
        _gla_kernel,
        grid=(B, nc),
        in_specs=[pl.BlockSpec((tc, 256), lambda b, c: (row(b, c), _COL["gq"] // 256)),
                  pl.BlockSpec((tc, 256), lambda b, c: (row(b, c), _COL["gk"] // 256)),
                  pl.BlockSpec((tc, 512), lambda b, c: (row(b, c), _COL["gv"] // 512)),
                  pl.BlockSpec((tc, 128), lambda b, c: (row(b, c), MISC_COL // 128)),
                  _resident((128, 256)),
                  _resident((1, 256))],
        out_specs=pl.BlockSpec((tc, 512), lambda b, c: (row(b, c), 0)),
        out_shape=jax.ShapeDtypeStruct((B * S, GLA_HEADS * GLA_DV), BF16),
        scratch_shapes=[pltpu.VMEM((GLA_HEADS // 2, 2 * GLA_DV, 2 * GLA_DK), F32)],
        compiler_params=_cparams("arbitrary", "arbitrary"),
        name="gla",
    )(p, p, p, p, wup_pad, bup)


def _s5_mats(a_re, a_im, log_step, b_re, b_im, c_re, c_im, d_skip):
    L, P, CH, G = S5_BLOCK, S5_STATE, S5_CH, S5_GROUPS
    hi = lax.Precision.HIGHEST
    step = jnp.exp(log_step.astype(F32))[:, None]
    are, aim = a_re.astype(F32), a_im.astype(F32)

    def power(j):
        mag = jnp.exp(are[:, None, :] * step[:, None, :] * j[None, :, None])
        ang = aim[:, None, :] * step[:, None, :] * j[None, :, None]
        return mag * jnp.cos(ang), mag * jnp.sin(ang)

    t = jnp.arange(L, dtype=F32)
    lb_re, lb_im = (v[:, 0] for v in power(jnp.ones((1,), F32)))
    den = are * are + aim * aim
    f_re = ((lb_re - 1.0) * are + lb_im * aim) / den
    f_im = (lb_im * are - (lb_re - 1.0) * aim) / den
    bt_re, bt_im = jnp.swapaxes(b_re, 1, 2), jnp.swapaxes(b_im, 1, 2)
    bb_re = f_re[:, None] * bt_re - f_im[:, None] * bt_im
    bb_im = f_re[:, None] * bt_im + f_im[:, None] * bt_re
    n_re, n_im = power(-t)
    a_r = (n_re[:, :, None] * bb_re[:, None] - n_im[:, :, None] * bb_im[:, None]).reshape(G, L * CH, P)
    a_i = (n_re[:, :, None] * bb_im[:, None] + n_im[:, :, None] * bb_re[:, None]).reshape(G, L * CH, P)
    p_re, p_im = power(t)
    ct_re, ct_im = jnp.swapaxes(c_re, 1, 2), jnp.swapaxes(c_im, 1, 2)
    pt_re, pt_im = jnp.swapaxes(p_re, 1, 2), jnp.swapaxes(p_im, 1, 2)
    b_r = (pt_re[..., None] * ct_re[:, :, None] - pt_im[..., None] * ct_im[:, :, None]).reshape(G, P, L * CH)
    b_i = (pt_re[..., None] * ct_im[:, :, None] + pt_im[..., None] * ct_re[:, :, None]).reshape(G, P, L * CH)
    m = (jnp.einsum("gxp,gpy->gxy", a_r, b_r, precision=hi) - jnp.einsum("gxp,gpy->gxy", a_i, b_i, precision=hi))
    blk = np.arange(L * CH) // CH
    m = jnp.where((blk[None, :] >= blk[:, None])[None], m, 0.0)
    m = m + jnp.eye(L * CH, dtype=F32)[None] * jnp.tile(d_skip.reshape(G, 1, CH), (1, 1, L))
    e_re, e_im = (v[:, 0][:, None] for v in power(jnp.full((1,), L - 1.0, F32)))
    pr, pi = a_r * e_re - a_i * e_im, a_r * e_im + a_i * e_re
    pm = jnp.concatenate([pr, pi, pi, pr], axis=-1)
    l_re, l_im = lb_re[..., None], lb_im[..., None]
    qm = jnp.concatenate([b_r * l_re - b_i * l_im, -(b_r * l_im + b_i * l_re)], axis=1)
    al_re, al_im = (v[:, 0] for v in power(jnp.full((1,), float(L), F32)))
    ar = jnp.concatenate([al_re, al_re], axis=-1).reshape(1, G * 2 * P)
    ai_a = jnp.concatenate([-al_im, al_im], axis=-1).reshape(1, G * 2 * P)
    ai_b = jnp.concatenate([al_im, -al_im], axis=-1).reshape(1, G * 2 * P)
    return m.astype(BF16), pm.astype(BF16), qm.astype(BF16), ar, ai_a, ai_b


def _s5_perm():
    src = np.arange(1024).reshape(8, 8, S5_CH).transpose(1, 0, 2).reshape(-1)
    m = np.zeros((1024, 1024), np.float32)
    m[np.arange(1024), src] = 1.0
    return jnp.asarray(m, BF16)


def _s5_kernel(su_ref, perm_ref, m_ref, p_ref, q_ref, ar_ref, aia_ref, aib_ref, y_ref,
               xcat_ref, u_ref, va_ref, vb_ref, x_ref, ycat_ref):
    L = S5_BLOCK
    NB = su_ref.shape[1] // L
    GH = 8
    W = 2 * S5_STATE
    for s in range(L):
        xcat_ref[:, s * 128:(s + 1) * 128] = su_ref[0, pl.ds(s, NB, stride=L), :].astype(BF16)
    for half in range(2):
        u_ref[half] = _dot(xcat_ref[:, half * 1024:(half + 1) * 1024], perm_ref[...]).astype(BF16)

    def u_of(g):
        return jnp.concatenate([u_ref[0, :, g * 128:(g + 1) * 128], u_ref[1, :, g * 128:(g + 1) * 128]], axis=1)

    for g in range(GH):
        v2 = _dot(u_of(g), p_ref[g])
        va_ref[:, g * W:(g + 1) * W] = v2[:, :W]
        vb_ref[:, g * W:(g + 1) * W] = v2[:, W:]
    ar, aia, aib = ar_ref[...], aia_ref[...], aib_ref[...]

    def step(n, carry):
        xa, xb = carry
        x_ref[pl.ds(n, 1), :] = xa
        na = xa * ar + xb * aia + va_ref[pl.ds(n, 1), :]
        nb = xb * ar + xa * aib + vb_ref[pl.ds(n, 1), :]
        return na, nb

    z = jnp.zeros((1, GH * W), F32)
    lax.fori_loop(0, NB, step, (z, z), unroll=8)
    for g in range(GH):
        y = _dot(u_of(g), m_ref[g]) + _dot(x_ref[:, g * W:(g + 1) * W].astype(BF16), q_ref[g])
        for half in range(2):
            ycat_ref[half, :, g * 128:(g + 1) * 128] = y[:, half * 128:(half + 1) * 128].astype(BF16)
    for half in range(2):
        out = _dot(ycat_ref[half], perm_ref[...])
        for t in range(8):
            y_ref[0, pl.ds(half * 8 + t, NB, stride=L), :] = out[:, t * 128:(t + 1) * 128]


def _s5(su, mats, B, S):
    m, pm, qm, ar, aia, aib = mats
    L, G, CH = S5_BLOCK, S5_GROUPS, S5_CH
    NB = S // L
    GH = 8
    W = 2 * S5_STATE
    slab = pl.BlockSpec((1, S, 128), lambda b, h: (h, b, 0))
    coef = pl.BlockSpec((1, GH * W), lambda b, h: (0, h))
    return pl.pallas_call(
        _s5_kernel,
        grid=(B, G // GH),
        in_specs=[slab,
                  pl.BlockSpec((1024, 1024), lambda b, h: (0, 0)),
                  pl.BlockSpec((GH, L * CH, L * CH), lambda b, h: (h, 0, 0)),
                  pl.BlockSpec((GH, L * CH, 2 * W), lambda b, h: (h, 0, 0)),
                  pl.BlockSpec((GH, W, L * CH), lambda b, h: (h, 0, 0)),
                  coef, coef, coef],
        out_specs=slab,
        out_shape=jax.ShapeDtypeStruct((G // GH, B * S, 128), F32),
        scratch_shapes=[pltpu.VMEM((NB, L * 128), BF16), pltpu.VMEM((2, NB, 1024), BF16),
                        pltpu.VMEM((NB, GH * W), F32), pltpu.VMEM((NB, GH * W), F32),
                        pltpu.VMEM((NB, GH * W), F32), pltpu.VMEM((2, NB, 1024), BF16)],
        compiler_params=_cparams("arbitrary", "arbitrary"),
        name="s5_scan",
    )(su, _s5_perm(), m, pm, qm, ar, aia, aib)


def _rope_kernel(pos_ref, prow_ref, f128_ref, f64_ref, i128_ref, i64_ref,
                 c128_ref, s128_ref, c64_ref, s64_ref, ct128_ref, st128_ref, ct64_ref, st64_ref):
    pos = pos_ref[...]
    a = pos * f128_ref[...]
    c128_ref[...] = jnp.cos(a)
    s128_ref[...] = jnp.sin(a)
    a = pos * f64_ref[...]
    c64_ref[...] = jnp.cos(a)
    s64_ref[...] = jnp.sin(a)
    a = i128_ref[...] * prow_ref[...]
    ct128_ref[...] = jnp.cos(a)
    st128_ref[...] = jnp.sin(a)
    a = i64_ref[...] * prow_ref[...]
    ct64_ref[...] = jnp.cos(a)
    st64_ref[...] = jnp.sin(a)


def _rope_inv(d):
    rd = d // ROPE_FRAC
    return ROPE_THETA ** (-jnp.arange(rd // 2, dtype=F32) * 2.0 / rd)


def _rope_freq(d, width):
    inv = _rope_inv(d)
    per_head = jnp.concatenate([inv, inv, jnp.zeros((d - d // ROPE_FRAC,), F32)])
    return jnp.tile(per_head, width // d).reshape(1, width)


def _rope_tables(positions):
    T = positions.size
    pos = positions.astype(F32)
    tq = min(512, T)
    h128, h64 = DSA_HD // ROPE_FRAC // 2, DSA_IDX_DIM // ROPE_FRAC // 2
    full = lambda shape: pl.BlockSpec(shape, lambda i: (0, 0))
    return pl.pallas_call(
        _rope_kernel,
        grid=(T // tq,),
        in_specs=[pl.BlockSpec((tq, 1), lambda i: (i, 0)), pl.BlockSpec((1, tq), lambda i: (0, i)),
                  full((1, 128)), full((1, 128)), full((h128, 1)), full((h64, 1))],
        out_specs=[pl.BlockSpec((tq, 128), lambda i: (i, 0))] * 4
        + [pl.BlockSpec((h128, tq), lambda i: (0, i))] * 2 + [pl.BlockSpec((h64, tq), lambda i: (0, i))] * 2,
        out_shape=[jax.ShapeDtypeStruct((T, 128), F32)] * 4
        + [jax.ShapeDtypeStruct((h128, T), F32)] * 2 + [jax.ShapeDtypeStruct((h64, T), F32)] * 2,
        compiler_params=_cparams("arbitrary"),
        name="rope_tables",
    )(pos.reshape(T, 1), pos.reshape(1, T), _rope_freq(DSA_HD, 128), _rope_freq(DSA_IDX_DIM, 128),
      _rope_inv(DSA_HD).reshape(h128, 1), _rope_inv(DSA_IDX_DIM).reshape(h64, 1))


def _rope(x, cos, sin, head_dim):
    width = x.shape[-1]
    half = head_dim // ROPE_FRAC // 2
    lane = lax.broadcasted_iota(jnp.int32, (1, width), 1)
    per = lane % head_dim
    sgn_lo = jnp.where(per < half, -1.0, 0.0)
    sgn_hi = jnp.where((per >= half) & (per < 2 * half), 1.0, 0.0)
    up = pltpu.roll(x, width - half, axis=1)
    dn = pltpu.roll(x, half, axis=1)
    return x * cos + sin * (up * sgn_lo + dn * sgn_hi)


def _rope_t(xt, cos, sin):
    half = cos.shape[0]
    x1, x2 = xt[:half], xt[half:2 * half]
    return jnp.concatenate([x1 * cos - x2 * sin, x2 * cos + x1 * sin, xt[2 * half:]], axis=0)


def _dsa_prep_kernel(dq_ref, dk_ref, dv_ref, diq_ref, misc_ref, c128_ref, s128_ref, c64_ref, s64_ref,
                     ct128_ref, st128_ref, ct64_ref, st64_ref, qg_ref, kg_ref,
                     qt_out, k_out, vt_out, iqt_out, ik_out, wt_out):
    ct128, st128 = ct128_ref[...], st128_ref[...]
    ct64, st64 = ct64_ref[...], st64_ref[...]
    for h in range(DSA_HEADS):
        xt = dq_ref[:, h * DSA_HD:(h + 1) * DSA_HD].astype(F32).T
        xt = xt * lax.rsqrt(jnp.mean(xt * xt, axis=0, keepdims=True) + NORM_EPS) * qg_ref[...]
        qt_out[h] = (_rope_t(xt, ct128, st128) * (DSA_HD ** -0.5 * LOG2E)).astype(qt_out.dtype)
    k = _rope(_rms(dk_ref[...].astype(F32), kg_ref[...]), c128_ref[...], s128_ref[...], DSA_HD)
    k_out[...] = k.astype(k_out.dtype)
    for c in range(vt_out.shape[0]):
        vt_out[c] = dv_ref[c * DSA_KCHUNK:(c + 1) * DSA_KCHUNK, :].astype(F32).T.astype(vt_out.dtype)
    for j in range(DSA_IDX_HEADS // 2):
        xt = diq_ref[:, j * 128:(j + 1) * 128].astype(F32).T
        for hh in range(2):
            head = xt[hh * DSA_IDX_DIM:(hh + 1) * DSA_IDX_DIM]
            iqt_out[2 * j + hh] = _rope_t(head, ct64, st64).astype(iqt_out.dtype)
    misc = misc_ref[...].astype(F32)
    ik_out[...] = _rope(misc, c64_ref[...], s64_ref[...], DSA_IDX_DIM)[:, :DSA_IDX_DIM].astype(ik_out.dtype)
    wt_out[...] = misc.T[DIW_LANE:DIW_LANE + DSA_IDX_HEADS]


def _dsa_prep(p, tables, q_gain_col, k_gain):
    T = p.shape[0]
    tq = min(512, T)
    assert tq % DSA_KCHUNK == 0
    tab = pl.BlockSpec((tq, 128), lambda i: (i, 0))
    h128, h64 = DSA_HD // ROPE_FRAC // 2, DSA_IDX_DIM // ROPE_FRAC // 2
    tab128 = pl.BlockSpec((h128, tq), lambda i: (0, i))
    tab64 = pl.BlockSpec((h64, tq), lambda i: (0, i))
    return pl.pallas_call(
        _dsa_prep_kernel,
        grid=(T // tq,),
        in_specs=[pl.BlockSpec((tq, 512), lambda i: (i, _COL["dq"] // 512)),
                  pl.BlockSpec((tq, 128), lambda i: (i, _COL["dk"] // 128)),
                  pl.BlockSpec((tq, 128), lambda i: (i, _COL["dv"] // 128)),
                  pl.BlockSpec((tq, 1024), lambda i: (i, _COL["diq"] // 1024)),
                  pl.BlockSpec((tq, 128), lambda i: (i, MISC_COL // 128)),
                  tab, tab, tab, tab, tab128, tab128, tab64, tab64,
                  pl.BlockSpec((DSA_HD, 1), lambda i: (0, 0)),
                  pl.BlockSpec((1, DSA_HD), lambda i: (0, 0))],
        out_specs=[pl.BlockSpec((DSA_HEADS, DSA_HD, tq), lambda i: (0, 0, i)),
                   pl.BlockSpec((tq, DSA_HD), lambda i: (i, 0)),
                   pl.BlockSpec((tq // DSA_KCHUNK, DSA_HD, DSA_KCHUNK), lambda i: (i, 0, 0)),
                   pl.BlockSpec((DSA_IDX_HEADS, DSA_IDX_DIM, tq), lambda i: (0, 0, i)),
                   pl.BlockSpec((tq, DSA_IDX_DIM), lambda i: (i, 0)),
                   pl.BlockSpec((DSA_IDX_HEADS, tq), lambda i: (0, i))],
        out_shape=[jax.ShapeDtypeStruct((DSA_HEADS, DSA_HD, T), BF16),
                   jax.ShapeDtypeStruct((T, DSA_HD), BF16),
                   jax.ShapeDtypeStruct((T // DSA_KCHUNK, DSA_HD, DSA_KCHUNK), BF16),
                   jax.ShapeDtypeStruct((DSA_IDX_HEADS, DSA_IDX_DIM, T), BF16),
                   jax.ShapeDtypeStruct((T, DSA_IDX_DIM), BF16),
                   jax.ShapeDtypeStruct((DSA_IDX_HEADS, T), F32)],
        compiler_params=_cparams("arbitrary"),
        name="dsa_prep",
    )(p, p, p, p, p, *tables, q_gain_col, k_gain)


_SWAP_MASKS = {16: 0x0000FFFF, 8: 0x00FF00FF, 4: 0x0F0F0F0F, 2: 0x33333333, 1: 0x55555555}


def _bit_swap_stage(a, j):
    a = list(a)
    k = 0
    while k < len(a):
        t = (a[k] ^ lax.shift_right_logical(a[k + j], jnp.int32(j))) & jnp.int32(_SWAP_MASKS[j])
        a[k] = a[k] ^ t
        a[k + j] = a[k + j] ^ (t << j)
        k = (k + j + 1) & ~j
    return a


def _dsa_kernel(iqt_ref, wt_ref, qt_ref, ik_ref, k_ref, vt_ref, o_ref,
                plane_ref, valid_ref, m_ref, g_ref, bias_ref, s_ref, p_ref, acc_ref, *, ksel):
    TQ, CK = DSA_QBLOCK, DSA_KCHUNK
    WORD = 32
    i = pl.program_id(1)
    nk = (i * TQ + TQ + CK - 1) // CK
    t_idx = i * TQ + lax.broadcasted_iota(jnp.int32, (1, TQ), 1)
    w_all = wt_ref[...] * (DSA_IDX_HEADS ** -0.5 * DSA_IDX_DIM ** -0.5)
    valid_ref[...] = jnp.zeros_like(valid_ref)
    sub = lax.broadcasted_iota(jnp.int32, (8, 1), 0)
    off = t_idx - (nk - 1) * CK
    diag_valid = jnp.zeros((8, TQ), jnp.int32)
    for p in range(WORD):
        bit = (1 << p) if p < WORD - 1 else INT_MIN
        diag_valid = diag_valid | jnp.where(8 * p + sub <= off, bit, 0)

    def score_chunk(kc, carry):
        QK = CK // 4
        quarters = []
        for part in range(4):
            lhs = ik_ref[0, kc, part * QK:(part + 1) * QK, :]
            acc = None
            for h in range(DSA_IDX_HEADS):
                term = w_all[h:h + 1] * jnp.maximum(_dot(lhs, iqt_ref[h]), 0.0)
                acc = term if acc is None else acc + term
            bits = lax.bitcast_convert_type(acc, jnp.int32)
            ukey = bits ^ ((bits >> 31) | INT_MIN)
            words = [ukey[8 * (QK // 8 - 1 - j):8 * (QK // 8 - j)] for j in range(QK // 8)]
            for j in (4, 2, 1):
                words = _bit_swap_stage(words, j)
            quarters.append(words)
        halves = [_bit_swap_stage(quarters[1] + quarters[0], 8), _bit_swap_stage(quarters[3] + quarters[2], 8)]
        planes = _bit_swap_stage(halves[1] + halves[0], 16)
        for p in range(WORD):
            plane_ref[kc, p] = planes[p]
        valid_ref[kc] = jnp.where(kc == nk - 1, diag_valid, -1)
        return carry

    lax.fori_loop(0, nk, score_chunk, 0)

    step = 4 if plane_ref.shape[0] % 4 == 0 else 2
    nstep = (nk + step - 1) // step

    def blank(kc, carry):
        plane_ref[kc] = jnp.zeros(plane_ref.shape[1:], jnp.int32)
        return carry

    lax.fori_loop(nk, nstep * step, blank, 0)

    def total(acc):
        return jnp.sum(acc.astype(F32), axis=0, keepdims=True)

    def first(it, acc):
        for c in range(step):
            kc = it * step + c
            m = valid_ref[kc]
            m_ref[kc] = m
            g_ref[kc] = jnp.zeros_like(m)
            acc = acc + lax.population_count(m & plane_ref[kc, 0])
        return acc

    def narrow(kc, n, take):
        m, g, pp = m_ref[kc], g_ref[kc], plane_ref[kc, n]
        hit = m & pp
        return jnp.where(take, hit, m & ~pp), jnp.where(take, g, g | hit)

    def decide(cnt, need):
        take = cnt >= need
        return take.astype(jnp.int32), jnp.where(take, need, need - cnt)

    zero = jnp.zeros((8, TQ), jnp.int32)
    take0, need0 = decide(total(lax.fori_loop(0, nstep, first, zero)), jnp.full((1, TQ), float(ksel), F32))

    def sweep(n, carry):
        take = carry[0] != 0

        def body(it, acc):
            for c in range(step):
                kc = it * step + c
                m, g = narrow(kc, n - 1, take)
                m_ref[kc] = m
                g_ref[kc] = g
                acc = acc + lax.population_count(m & plane_ref[kc, n])
            return acc

        return decide(total(lax.fori_loop(0, nstep, body, zero)), carry[1])

    take_l, _ = lax.fori_loop(1, WORD, sweep, (take0, need0))

    def finish(it, carry):
        for c in range(step):
            kc = it * step + c
            m, g = narrow(kc, WORD - 1, take_l != 0)
            m_ref[kc] = m | g
        return carry

    lax.fori_loop(0, nstep, finish, 0)

    kg = CK // 16
    unit = 64

    def logits(kc, buf, m_run):
        sel = m_ref[kc]
        for p in range(WORD):
            on = (sel << (WORD - 1 - p)) < 0
            bias_ref[8 * p:8 * (p + 1), :] = jnp.where(on, 0.0, -1e30)
        kk = k_ref[0, kc]
        new = []
        for h in range(DSA_HEADS):
            s = _dot(kk, qt_ref[h]) + bias_ref[...]
            s_ref[buf, h] = s
            mx = jnp.max(s.reshape(kg, 2, 8, TQ), axis=0)
            new.append(jnp.maximum(m_run[h], jnp.max(jnp.max(mx, axis=0), axis=0, keepdims=True)))
        return tuple(new)

    def attn_chunk(kc, rd, wr, carry):
        m_old, m_run, ls = carry
        m_next = logits(jnp.minimum(kc + 1, nk - 1), wr, m_run)
        vt = vt_ref[0, kc]
        new_l = []
        for h in range(DSA_HEADS):
            alpha = jnp.exp2(m_old[h] - m_run[h])
            mb = jnp.broadcast_to(m_run[h], (8, TQ))[None]
            part = ls[h] * alpha
            for u in range(CK // unit):
                rows = slice(u * unit, (u + 1) * unit)
                pr = jnp.exp2(s_ref[rd, h, rows, :].reshape(unit // 8, 8, TQ) - mb)
                part = part + jnp.sum(pr.reshape(unit // 16, 2, 8, TQ), axis=0)
                p_ref[h, rows, :] = pr.reshape(unit, TQ).astype(BF16)
            new_l.append(part)
            acc_ref[h] = acc_ref[h] * alpha + _dot(vt, p_ref[h])
        return m_run, m_next, tuple(new_l)

    def attn_pair(j, carry):
        return attn_chunk(2 * j + 1, 1, 0, attn_chunk(2 * j, 0, 1, carry))

    acc_ref[...] = jnp.zeros_like(acc_ref)
    floor = tuple(jnp.full((1, TQ), -1e30, F32) for _ in range(DSA_HEADS))
    init = (floor, logits(0, 0, floor), tuple(jnp.zeros((2, 8, TQ), F32) for _ in range(DSA_HEADS)))
    carry = lax.fori_loop(0, nk // 2, attn_pair, init)
    _, _, ls = lax.cond(nk % 2 == 1, lambda c: attn_chunk(nk - 1, 0, 1, c), lambda c: c, carry)
    for h in range(DSA_HEADS):
        denom = jnp.sum(jnp.sum(ls[h], axis=0), axis=0, keepdims=True)
        o_ref[:, h * DSA_HD:(h + 1) * DSA_HD] = (acc_ref[h] / denom).T.astype(o_ref.dtype)


def _dsa(qt, k_rot, vt, iqt, ik_rot, wt, B, S):
    TQ, CK = DSA_QBLOCK, DSA_KCHUNK
    NC = S // CK
    nq = S // TQ
    assert NC % 2 == 0 and TQ == CK == 8 * 32
    ksel = min(DSA_TOPK, S // 4)
    ik = ik_rot.reshape(B, NC, CK, DSA_IDX_DIM)
    k = k_rot.reshape(B, NC, CK, DSA_HD)
    vt = vt.reshape(B, NC, DSA_HD, CK)
    return pl.pallas_call(
        functools.partial(_dsa_kernel, ksel=ksel),
        grid=(B, nq),
        in_specs=[pl.BlockSpec((DSA_IDX_HEADS, DSA_IDX_DIM, TQ), lambda b, i: (0, 0, b * nq + i)),
                  pl.BlockSpec((DSA_IDX_HEADS, TQ), lambda b, i: (0, b * nq + i)),
                  pl.BlockSpec((DSA_HEADS, DSA_HD, TQ), lambda b, i: (0, 0, b * nq + i)),
                  pl.BlockSpec((1, NC, CK, DSA_IDX_DIM), lambda b, i: (b, 0, 0, 0)),
                  pl.BlockSpec((1, NC, CK, DSA_HD), lambda b, i: (b, 0, 0, 0)),
                  pl.BlockSpec((1, NC, DSA_HD, CK), lambda b, i: (b, 0, 0, 0))],
        out_specs=pl.BlockSpec((TQ, DSA_HEADS * DSA_HD), lambda b, i: (b * nq + i, 0)),
        out_shape=jax.ShapeDtypeStruct((B * S, DSA_HEADS * DSA_HD), BF16),
        scratch_shapes=[pltpu.VMEM((NC, 32, 8, TQ), jnp.int32),
                        pltpu.VMEM((NC, 8, TQ), jnp.int32),
                        pltpu.VMEM((NC, 8, TQ), jnp.int32),
                        pltpu.VMEM((NC, 8, TQ), jnp.int32),
                        pltpu.VMEM((CK, TQ), F32),
                        pltpu.VMEM((2, DSA_HEADS, CK, TQ), F32),
                        pltpu.VMEM((DSA_HEADS, CK, TQ), BF16),
                        pltpu.VMEM((DSA_HEADS, DSA_HD, TQ), F32)],
        compiler_params=_cparams("arbitrary", "arbitrary"),
        name="dsa_attn",
    )(iqt, wt, qt, ik, k, vt)


def _mem_kv_kernel(mem_ref, g_ref, w_ref, kg_ref, k_out, v_out):
    h = _rms(mem_ref[...], g_ref[...]).astype(BF16)
    kv = _dot(h, w_ref[...])
    width = MEM_HEADS * MEM_HD
    for hd in range(MEM_HEADS):
        sl = slice(hd * MEM_HD, (hd + 1) * MEM_HD)
        k_out[:, sl] = _rms(kv[:, sl], kg_ref[...]).astype(k_out.dtype)
    v_out[...] = kv[:, width:].astype(v_out.dtype)


def _mem_kv(mem2d, gain, w_kv, k_gain, B, l):
    width = MEM_HEADS * MEM_HD
    out = jax.ShapeDtypeStruct((B * MEM_LEN, width), BF16)
    return pl.pallas_call(
        _mem_kv_kernel,
        grid=(B,),
        in_specs=[pl.BlockSpec((MEM_LEN, D_MODEL), lambda b: (b, 0)),
                  pl.BlockSpec((1, D_MODEL), lambda b: (0, 0)),
                  _layer((D_MODEL, 2 * width), l),
                  pl.BlockSpec((1, MEM_HD), lambda b: (0, 0))],
        out_specs=[pl.BlockSpec((MEM_LEN, width), lambda b: (b, 0))] * 2,
        out_shape=[out, out],
        compiler_params=_cparams("arbitrary"),
        name="mem_kv",
    )(mem2d, gain, w_kv, k_gain)


def _mem_attn_kernel(q_ref, k_ref, v_ref, qg_ref, o_ref):
    for h in range(MEM_HEADS):
        sl = slice(h * MEM_HD, (h + 1) * MEM_HD)
        q = _rms(q_ref[:, sl].astype(F32), qg_ref[...]) * (MEM_HD ** -0.5)
        s = _dot_nt(q.astype(BF16), k_ref[:, sl])
        pr = jnp.exp(s - jnp.max(s, axis=1, keepdims=True))
        o = _dot(pr.astype(BF16), v_ref[:, sl]) / jnp.sum(pr, axis=1, keepdims=True)
        o_ref[:, sl] = o.astype(o_ref.dtype)


def _mem_attn(p, mk, mv, q_gain, B, S):
    tq = min(512, S)
    nq = S // tq
    width = MEM_HEADS * MEM_HD
    return pl.pallas_call(
        _mem_attn_kernel,
        grid=(B, nq),
        in_specs=[pl.BlockSpec((tq, width), lambda b, i: (b * nq + i, _COL["mq"] // width)),
                  pl.BlockSpec((MEM_LEN, width), lambda b, i: (b, 0)),
                  pl.BlockSpec((MEM_LEN, width), lambda b, i: (b, 0)),
                  pl.BlockSpec((1, MEM_HD), lambda b, i: (0, 0))],
        out_specs=pl.BlockSpec((tq, width), lambda b, i: (b * nq + i, 0)),
        out_shape=jax.ShapeDtypeStruct((B * S, width), BF16),
        compiler_params=_cparams("arbitrary", "arbitrary"),
        name="mem_attn",
    )(p, mk, mv, q_gain)


def _out_proj_kernel(og_ref, ys_ref, od_ref, om_ref, z_ref, x_ref, gn_ref, wglu_ref, w_ref, o_ref, lhs_ref):
    y = jax.nn.gelu(jnp.concatenate([ys_ref[q] for q in range(ys_ref.shape[0])], axis=1))
    o_s5 = y * jax.nn.sigmoid(_dot(y.astype(BF16), wglu_ref[...]))
    branches = (og_ref[...].astype(F32), o_s5, od_ref[...].astype(F32), om_ref[...].astype(F32))
    for bi, o in enumerate(branches):
        for h in range(o.shape[1] // HEAD):
            c0 = bi * 512 + h * HEAD
            z = z_ref[:, c0:c0 + HEAD].astype(F32)
            n = _rms(o[:, h * HEAD:(h + 1) * HEAD], gn_ref[:, c0:c0 + HEAD])
            lhs_ref[:, c0:c0 + HEAD] = (n * (z * jax.nn.sigmoid(z))).astype(BF16)
    o_ref[...] = x_ref[...] + _dot(lhs_ref[...], w_ref[...])


def _out_proj(o_gla, y_s5, o_dsa, o_mem, p, x2d, out_gain, w_glu, w_out, l):
    T = x2d.shape[0]
    tm = min(512, T)
    branch = pl.BlockSpec((tm, 512), lambda i: (i, 0))
    return pl.pallas_call(
        _out_proj_kernel,
        grid=(T // tm,),
        in_specs=[branch, pl.BlockSpec((S5_WIDTH // 128, tm, 128), lambda i: (0, i, 0)), branch, branch,
                  pl.BlockSpec((tm, MIX_WIDTH), lambda i: (i, 0)),
                  pl.BlockSpec((tm, D_MODEL), lambda i: (i, 0)),
                  _resident((1, MIX_WIDTH)),
                  _layer((S5_WIDTH, S5_WIDTH), l),
                  _layer((MIX_WIDTH, D_MODEL), l)],
        out_specs=pl.BlockSpec((tm, D_MODEL), lambda i: (i, 0)),
        out_shape=jax.ShapeDtypeStruct((T, D_MODEL), F32),
        scratch_shapes=[pltpu.VMEM((tm, MIX_WIDTH), BF16)],
        compiler_params=_cparams("arbitrary"),
        name="out_proj",
    )(o_gla, y_s5, o_dsa, o_mem, p, x2d, out_gain, w_glu, w_out)


def _permute_kernel(w_ref, o_ref):
    c = 0
    for n in _ORDER:
        lo, hi = _ORIG[n]
        o_ref[c:c + hi - lo, :] = w_ref[lo:hi, :].astype(o_ref.dtype)
        c += hi - lo
    o_ref[c:, :] = jnp.zeros((IN_PAD - c, o_ref.shape[1]), o_ref.dtype)


def _permute_w_in(w_in):
    depth, rows, width = w_in.shape
    tl = 256
    return pl.pallas_call(
        _permute_kernel,
        grid=(depth, rows // tl),
        in_specs=[pl.BlockSpec((None, width, tl), lambda l, i: (l, 0, i))],
        out_specs=pl.BlockSpec((None, IN_PAD, tl), lambda l, i: (l, 0, i)),
        out_shape=jax.ShapeDtypeStruct((depth, IN_PAD, rows), BF16),
        compiler_params=_cparams("arbitrary", "arbitrary"),
        name="permute_w_in",
    )(jnp.swapaxes(w_in, 1, 2))


def kernel(x, mem, positions, ln_gain, w_in, gla_w_up, gla_b_up, s5_a_re, s5_a_im, s5_log_step, s5_b_re, s5_b_im, s5_c_re, s5_c_im, s5_d, s5_w_glu, dsa_q_norm, dsa_k_norm, mem_norm, w_mem_kv, mem_q_norm, mem_k_norm, out_norm, w_out):
    B, S, _ = x.shape
    depth = w_in.shape[0]
    T = B * S
    x2d = x.reshape(T, D_MODEL)
    mem2d = mem.reshape(B * mem.shape[1], D_MODEL)
    w_in_p = _permute_w_in(w_in)
    wup_pad = jnp.zeros((depth, 128, GLA_HEADS * GLA_DK), F32)
    wup_pad = wup_pad.at[:, GLR_LANE:GLR_LANE + GLA_GATE_RANK].set(gla_w_up).astype(BF16)
    w_kv = w_mem_kv.astype(BF16)
    w_glu = s5_w_glu.astype(BF16)
    w_o = w_out.astype(BF16)
    tables = _rope_tables(positions)
    for l in range(depth):
        p, su = _in_proj(x2d, ln_gain[l][None], w_in_p, l)
        o_gla = _gla(p, wup_pad[l], gla_b_up[l][None], B, S)
        mats = _s5_mats(s5_a_re[l], s5_a_im[l], s5_log_step[l], s5_b_re[l], s5_b_im[l],
                        s5_c_re[l], s5_c_im[l], s5_d[l])
        y_s5 = _s5(su, mats, B, S)
        qt, k_rot, vt, iqt, ik_rot, wt = _dsa_prep(p, tables, dsa_q_norm[l][:, None], dsa_k_norm[l][None])
        o_dsa = _dsa(qt, k_rot, vt, iqt, ik_rot, wt, B, S)
        mk, mv = _mem_kv(mem2d, mem_norm[l][None], w_kv, mem_k_norm[l][None], B, l)
        o_mem = _mem_attn(p, mk, mv, mem_q_norm[l][None], B, S)
        x2d = _out_proj(o_gla, y_s5, o_dsa, o_mem, p, x2d, out_norm[l][None], w_glu, w_o, l)
    return x2d.reshape(B, S, D_MODEL)
```

```python
import functools

import numpy as np
import jax
import jax.numpy as jnp
from jax import lax
from jax.experimental import pallas as pl
from jax.experimental.pallas import tpu as pltpu

F32 = jnp.float32
BF16 = jnp.bfloat16

D_MODEL = 2048
MEM_LEN = 256
ROPE_THETA = 500000.0
ROPE_FRAC = 4
NORM_EPS = 1e-6

GLA_HEADS = 4
GLA_DK = 64
GLA_DV = 128
GLA_GATE_RANK = 16
GLA_TAU = 16.0
GLA_CHUNK = 64

S5_GROUPS = 32
S5_CH = 16
S5_STATE = 64
S5_WIDTH = S5_GROUPS * S5_CH
S5_BLOCK = 16

DSA_HEADS = 4
DSA_HD = 128
DSA_IDX_HEADS = 16
DSA_IDX_DIM = 64
DSA_TOPK = 256
DSA_QBLOCK = 256
DSA_KCHUNK = 256

MEM_HEADS = 4
MEM_HD = 128
HEAD = 128
MIX_WIDTH = 2048

VMEM_LIMIT_BYTES = 52 * 1024 * 1024

_ORIG = dict(gq=(0, 256), gk=(256, 512), gv=(512, 1024), glr=(1024, 1040), gz=(1040, 1552),
             su=(1552, 2064), sz=(2064, 2576), dq=(2576, 3088), dk=(3088, 3216), dv=(3216, 3344),
             diq=(3344, 4368), dik=(4368, 4432), diw=(4432, 4448), dz=(4448, 4960),
             mq=(4960, 5472), mz=(5472, 5984))
_ORDER = ("gz", "sz", "dz", "mz", "diq", "gv", "su", "dq", "mq", "gq", "gk", "dk", "dv",
          "dik", "glr", "diw")
_COL = {}
_c = 0
for _n in _ORDER:
    _COL[_n] = _c
    _c += _ORIG[_n][1] - _ORIG[_n][0]
IN_PAD = 6144
MISC_COL = _COL["dik"]
GLR_LANE = 64
DIW_LANE = 80
INT_MIN = -(2 ** 31)
LOG2E = 1.4426950408889634


def _cparams(*sem):
    return pltpu.CompilerParams(dimension_semantics=sem, vmem_limit_bytes=VMEM_LIMIT_BYTES)


def _dot(a, b):
    return jnp.dot(a, b, preferred_element_type=F32)


def _dot_nt(a, b):
    return lax.dot_general(a, b, (((1,), (1,)), ((), ())), preferred_element_type=F32)


def _rms(x, gain):
    return x * lax.rsqrt(jnp.mean(x * x, axis=-1, keepdims=True) + NORM_EPS) * gain


def _in_proj_kernel(x_ref, g_ref, w_ref, o_ref, su_ref, h_ref):
    h_ref[...] = _rms(x_ref[...], g_ref[...]).astype(BF16)
    tn = S5_WIDTH
    for j in range(IN_PAD // tn):
        acc = _dot_nt(h_ref[...], w_ref[j * tn:(j + 1) * tn, :])
        o_ref[:, j * tn:(j + 1) * tn] = acc.astype(o_ref.dtype)
        if j == _COL["su"] // tn:
            for q in range(tn // 128):
                su_ref[q] = acc[:, q * 128:(q + 1) * 128]


def _resident(shape):
    return pl.BlockSpec(shape, lambda *_: (0,) * len(shape), pipeline_mode=pl.Buffered(1))


def _layer(shape, l):
    return pl.BlockSpec((None,) + shape, lambda *_: (l,) + (0,) * len(shape), pipeline_mode=pl.Buffered(1))


def _in_proj(x2d, gain, w, l):
    T = x2d.shape[0]
    tm = min(512, T)
    return pl.pallas_call(
        _in_proj_kernel,
        grid=(T // tm,),
        in_specs=[pl.BlockSpec((tm, D_MODEL), lambda i: (i, 0)),
                  _resident((1, D_MODEL)),
                  _layer((IN_PAD, D_MODEL), l)],
        out_specs=[pl.BlockSpec((tm, IN_PAD), lambda i: (i, 0)),
                   pl.BlockSpec((S5_WIDTH // 128, tm, 128), lambda i: (0, i, 0))],
        out_shape=[jax.ShapeDtypeStruct((T, IN_PAD), BF16),
                   jax.ShapeDtypeStruct((S5_WIDTH // 128, T, 128), F32)],
        scratch_shapes=[pltpu.VMEM((tm, D_MODEL), BF16)],
        compiler_params=_cparams("arbitrary"),
        name="in_proj",
    )(x2d, gain, w)


def _gla_kernel(q_ref, k_ref, v_ref, misc_ref, wup_ref, bup_ref, o_ref, st_ref):
    @pl.when(pl.program_id(1) == 0)
    def _():
        st_ref[...] = jnp.zeros_like(st_ref)

    C = GLA_CHUNK
    tc = q_ref.shape[0]
    nchunk = tc // C
    x = _dot(misc_ref[...], wup_ref[...]) + bup_ref[...]
    log_a = (jnp.minimum(x, 0.0) - jnp.log(1.0 + jnp.exp(-jnp.abs(x)))) * (1.0 / GLA_TAU)
    row = lax.broadcasted_iota(jnp.int32, (tc, tc), 0)
    col = lax.broadcasted_iota(jnp.int32, (tc, tc), 1)
    tril = (row >= col) & ((row // C) == (col // C))
    la_hi = log_a.astype(BF16)
    r1 = log_a - la_hi.astype(F32)
    la_mid = r1.astype(BF16)
    la_lo = (r1 - la_mid.astype(F32)).astype(BF16)
    tri3 = jnp.concatenate([tril.astype(BF16)] * 3, axis=1)
    b = _dot(tri3, jnp.concatenate([la_hi, la_mid, la_lo], axis=0))
    lane = lax.broadcasted_iota(jnp.int32, (1, 2 * GLA_DK), 1)
    low = lane < GLA_DK
    srow = lax.broadcasted_iota(jnp.int32, (2 * GLA_DV, 1), 0)
    rowc = lax.broadcasted_iota(jnp.int32, (tc, 1), 0)
    diag = (srow < GLA_DV) == low
    for j in range(GLA_HEADS // 2):
        ls = slice(j * 128, (j + 1) * 128)
        bp = b[:, ls]
        mid = jnp.concatenate([jnp.broadcast_to(bp[c * C + C // 2:c * C + C // 2 + 1], (C, 128))
                               for c in range(nchunk)], axis=0)
        last = jnp.concatenate([jnp.broadcast_to(bp[c * C + C - 1:c * C + C], (C, 128))
                                for c in range(nchunk)], axis=0)
        qf = q_ref[:, ls].astype(F32) * (GLA_DK ** -0.5)
        kf = k_ref[:, ls].astype(F32)
        qe = qf * jnp.exp(bp - mid)
        ke_p = (kf * jnp.exp(mid - bp)).astype(BF16)
        kd_p = (kf * jnp.exp(last - bp)).astype(BF16)
        qi_p = (qf * jnp.exp(bp)).astype(BF16)
        vs = slice(2 * j * GLA_DV, (2 * j + 2) * GLA_DV)
        q2 = jnp.concatenate([jnp.where(low, qe, 0.0), jnp.where(low, 0.0, qe)], axis=0).astype(BF16)
        sc = _dot_nt(q2, ke_p)
        attn = jnp.concatenate([jnp.where(tril, sc[:tc], 0.0), jnp.where(tril, sc[tc:], 0.0)], axis=1)
        vp = v_ref[:, vs]
        first = lax.broadcasted_iota(jnp.int32, (1, 2 * GLA_DV), 1) < GLA_DV
        v2 = jnp.concatenate([jnp.where(first, vp, jnp.zeros_like(vp)),
                              jnp.where(first, jnp.zeros_like(vp), vp)], axis=0)
        intra = _dot(attn.astype(BF16), v2)
        in_chunk = [(rowc >= c * C) & (rowc < (c + 1) * C) for c in range(nchunk)]
        kd_blk = jnp.concatenate([jnp.where(in_chunk[c], kd_p, jnp.zeros_like(kd_p)) for c in range(nchunk)], axis=1)
        qi_blk = jnp.concatenate([jnp.where(in_chunk[c], qi_p, jnp.zeros_like(qi_p)) for c in range(nchunk)], axis=1)
        v_t = v_ref[:, vs].astype(F32).T.astype(BF16)
        ds = _dot(v_t, kd_blk)
        st = st_ref[j]
        states = []
        for c in range(nchunk):
            states.append(st.astype(BF16))
            dec = jnp.exp(bp[c * C + C - 1:c * C + C])
            st = jnp.where(diag, st * dec + ds[:, c * 128:(c + 1) * 128], 0.0)
        st_ref[j] = st
        o_ref[:, vs] = (intra + _dot_nt(qi_blk, jnp.concatenate(states, axis=1))).astype(o_ref.dtype)


def _gla(p, wup_pad, bup, B, S):
    tc = min(256, S)
    nc = S // tc
    row = lambda b, c: b * nc + c
    return pl.pallas_call(
        _gla_kernel,
        grid=(B, nc),
        in_specs=[pl.BlockSpec((tc, 256), lambda b, c: (row(b, c), _COL["gq"] // 256)),
                  pl.BlockSpec((tc, 256), lambda b, c: (row(b, c), _COL["gk"] // 256)),
                  pl.BlockSpec((tc, 512), lambda b, c: (row(b, c), _COL["gv"] // 512)),
                  pl.BlockSpec((tc, 128), lambda b, c: (row(b, c), MISC_COL // 128)),
                  _resident((128, 256)),
                  _resident((1, 256))],
        out_specs=pl.BlockSpec((tc, 512), lambda b, c: (row(b, c), 0)),
        out_shape=jax.ShapeDtypeStruct((B * S, GLA_HEADS * GLA_DV), BF16),
        scratch_shapes=[pltpu.VMEM((GLA_HEADS // 2, 2 * GLA_DV, 2 * GLA_DK), F32)],
        compiler_params=_cparams("arbitrary", "arbitrary"),
        name="gla",
    )(p, p, p, p, wup_pad, bup)


def _s5_mats(a_re, a_im, log_step, b_re, b_im, c_re, c_im, d_skip):
    L, P, CH, G = S5_BLOCK, S5_STATE, S5_CH, S5_GROUPS
    hi = lax.Precision.HIGHEST
    step = jnp.exp(log_step.astype(F32))[:, None]
    are, aim = a_re.astype(F32), a_im.astype(F32)

    def power(j):
        mag = jnp.exp(are[:, None, :] * step[:, None, :] * j[None, :, None])
        ang = aim[:, None, :] * step[:, None, :] * j[None, :, None]
        return mag * jnp.cos(ang), mag * jnp.sin(ang)

    t = jnp.arange(L, dtype=F32)
    lb_re, lb_im = (v[:, 0] for v in power(jnp.ones((1,), F32)))
    den = are * are + aim * aim
    f_re = ((lb_re - 1.0) * are + lb_im * aim) / den
    f_im = (lb_im * are - (lb_re - 1.0) * aim) / den
    bt_re, bt_im = jnp.swapaxes(b_re, 1, 2), jnp.swapaxes(b_im, 1, 2)
    bb_re = f_re[:, None] * bt_re - f_im[:, None] * bt_im
    bb_im = f_re[:, None] * bt_im + f_im[:, None] * bt_re
    n_re, n_im = power(-t)
    a_r = (n_re[:, :, None] * bb_re[:, None] - n_im[:, :, None] * bb_im[:, None]).reshape(G, L * CH, P)
    a_i = (n_re[:, :, None] * bb_im[:, None] + n_im[:, :, None] * bb_re[:, None]).reshape(G, L * CH, P)
    p_re, p_im = power(t)
    ct_re, ct_im = jnp.swapaxes(c_re, 1, 2), jnp.swapaxes(c_im, 1, 2)
    pt_re, pt_im = jnp.swapaxes(p_re, 1, 2), jnp.swapaxes(p_im, 1, 2)
    b_r = (pt_re[..., None] * ct_re[:, :, None] - pt_im[..., None] * ct_im[:, :, None]).reshape(G, P, L * CH)
    b_i = (pt_re[..., None] * ct_im[:, :, None] + pt_im[..., None] * ct_re[:, :, None]).reshape(G, P, L * CH)
    m = (jnp.einsum("gxp,gpy->gxy", a_r, b_r, precision=hi) - jnp.einsum("gxp,gpy->gxy", a_i, b_i, precision=hi))
    blk = np.arange(L * CH) // CH
    m = jnp.where((blk[None, :] >= blk[:, None])[None], m, 0.0)
    m = m + jnp.eye(L * CH, dtype=F32)[None] * jnp.tile(d_skip.reshape(G, 1, CH), (1, 1, L))
    e_re, e_im = (v[:, 0][:, None] for v in power(jnp.full((1,), L - 1.0, F32)))
    pr, pi = a_r * e_re - a_i * e_im, a_r * e_im + a_i * e_re
    pm = jnp.concatenate([pr, pi, pi, pr], axis=-1)
    l_re, l_im = lb_re[..., None], lb_im[..., None]
    qm = jnp.concatenate([b_r * l_re - b_i * l_im, -(b_r * l_im + b_i * l_re)], axis=1)
    al_re, al_im = (v[:, 0] for v in power(jnp.full((1,), float(L), F32)))
    ar = jnp.concatenate([al_re, al_re], axis=-1).reshape(1, G * 2 * P)
    ai_a = jnp.concatenate([-al_im, al_im], axis=-1).reshape(1, G * 2 * P)
    ai_b = jnp.concatenate([al_im, -al_im], axis=-1).reshape(1, G * 2 * P)
    return m.astype(BF16), pm.astype(BF16), qm.astype(BF16), ar, ai_a, ai_b


def _s5_perm():
    src = np.arange(1024).reshape(8, 8, S5_CH).transpose(1, 0, 2).reshape(-1)
    m = np.zeros((1024, 1024), np.float32)
    m[np.arange(1024), src] = 1.0
    return jnp.asarray(m, BF16)


def _s5_kernel(su_ref, perm_ref, m_ref, p_ref, q_ref, ar_ref, aia_ref, aib_ref, y_ref,
               xcat_ref, u_ref, va_ref, vb_ref, x_ref, ycat_ref):
    L = S5_BLOCK
    NB = su_ref.shape[1] // L
    GH = 8
    W = 2 * S5_STATE
    for s in range(L):
        xcat_ref[:, s * 128:(s + 1) * 128] = su_ref[0, pl.ds(s, NB, stride=L), :].astype(BF16)
    for half in range(2):
        u_ref[half] = _dot(xcat_ref[:, half * 1024:(half + 1) * 1024], perm_ref[...]).astype(BF16)

    def u_of(g):
        return jnp.concatenate([u_ref[0, :, g * 128:(g + 1) * 128], u_ref[1, :, g * 128:(g + 1) * 128]], axis=1)

    for g in range(GH):
        v2 = _dot(u_of(g), p_ref[g])
        va_ref[:, g * W:(g + 1) * W] = v2[:, :W]
        vb_ref[:, g * W:(g + 1) * W] = v2[:, W:]
    ar, aia, aib = ar_ref[...], aia_ref[...], aib_ref[...]

    def step(n, carry):
        xa, xb = carry
        x_ref[pl.ds(n, 1), :] = xa
        na = xa * ar + xb * aia + va_ref[pl.ds(n, 1), :]
        nb = xb * ar + xa * aib + vb_ref[pl.ds(n, 1), :]
        return na, nb

    z = jnp.zeros((1, GH * W), F32)
    lax.fori_loop(0, NB, step, (z, z), unroll=8)
    for g in range(GH):
        y = _dot(u_of(g), m_ref[g]) + _dot(x_ref[:, g * W:(g + 1) * W].astype(BF16), q_ref[g])
        for half in range(2):
            ycat_ref[half, :, g * 128:(g + 1) * 128] = y[:, half * 128:(half + 1) * 128].astype(BF16)
    for half in range(2):
        out = _dot(ycat_ref[half], perm_ref[...])
        for t in range(8):
            y_ref[0, pl.ds(half * 8 + t, NB, stride=L), :] = out[:, t * 128:(t + 1) * 128]


def _s5(su, mats, B, S):
    m, pm, qm, ar, aia, aib = mats
    L, G, CH = S5_BLOCK, S5_GROUPS, S5_CH
    NB = S // L
    GH = 8
    W = 2 * S5_STATE
    slab = pl.BlockSpec((1, S, 128), lambda b, h: (h, b, 0))
    coef = pl.BlockSpec((1, GH * W), lambda b, h: (0, h))
    return pl.pallas_call(
        _s5_kernel,
        grid=(B, G // GH),
        in_specs=[slab,
                  pl.BlockSpec((1024, 1024), lambda b, h: (0, 0)),
                  pl.BlockSpec((GH, L * CH, L * CH), lambda b, h: (h, 0, 0)),
                  pl.BlockSpec((GH, L * CH, 2 * W), lambda b, h: (h, 0, 0)),
                  pl.BlockSpec((GH, W, L * CH), lambda b, h: (h, 0, 0)),
                  coef, coef, coef],
        out_specs=slab,
        out_shape=jax.ShapeDtypeStruct((G // GH, B * S, 128), F32),
        scratch_shapes=[pltpu.VMEM((NB, L * 128), BF16), pltpu.VMEM((2, NB, 1024), BF16),
                        pltpu.VMEM((NB, GH * W), F32), pltpu.VMEM((NB, GH * W), F32),
                        pltpu.VMEM((NB, GH * W), F32), pltpu.VMEM((2, NB, 1024), BF16)],
        compiler_params=_cparams("arbitrary", "arbitrary"),
        name="s5_scan",
    )(su, _s5_perm(), m, pm, qm, ar, aia, aib)


def _rope_kernel(pos_ref, prow_ref, f128_ref, f64_ref, i128_ref, i64_ref,
                 c128_ref, s128_ref, c64_ref, s64_ref, ct128_ref, st128_ref, ct64_ref, st64_ref):
    pos = pos_ref[...]
    a = pos * f128_ref[...]
    c128_ref[...] = jnp.cos(a)
    s128_ref[...] = jnp.sin(a)
    a = pos * f64_ref[...]
    c64_ref[...] = jnp.cos(a)
    s64_ref[...] = jnp.sin(a)
    a = i128_ref[...] * prow_ref[...]
    ct128_ref[...] = jnp.cos(a)
    st128_ref[...] = jnp.sin(a)
    a = i64_ref[...] * prow_ref[...]
    ct64_ref[...] = jnp.cos(a)
    st64_ref[...] = jnp.sin(a)


def _rope_inv(d):
    rd = d // ROPE_FRAC
    return ROPE_THETA ** (-jnp.arange(rd // 2, dtype=F32) * 2.0 / rd)


def _rope_freq(d, width):
    inv = _rope_inv(d)
    per_head = jnp.concatenate([inv, inv, jnp.zeros((d - d // ROPE_FRAC,), F32)])
    return jnp.tile(per_head, width // d).reshape(1, width)


def _rope_tables(positions):
    T = positions.size
    pos = positions.astype(F32)
    tq = min(512, T)
    h128, h64 = DSA_HD // ROPE_FRAC // 2, DSA_IDX_DIM // ROPE_FRAC // 2
    full = lambda shape: pl.BlockSpec(shape, lambda i: (0, 0))
    return pl.pallas_call(
        _rope_kernel,
        grid=(T // tq,),
        in_specs=[pl.BlockSpec((tq, 1), lambda i: (i, 0)), pl.BlockSpec((1, tq), lambda i: (0, i)),
                  full((1, 128)), full((1, 128)), full((h128, 1)), full((h64, 1))],
        out_specs=[pl.BlockSpec((tq, 128), lambda i: (i, 0))] * 4
        + [pl.BlockSpec((h128, tq), lambda i: (0, i))] * 2 + [pl.BlockSpec((h64, tq), lambda i: (0, i))] * 2,
        out_shape=[jax.ShapeDtypeStruct((T, 128), F32)] * 4
        + [jax.ShapeDtypeStruct((h128, T), F32)] * 2 + [jax.ShapeDtypeStruct((h64, T), F32)] * 2,
        compiler_params=_cparams("arbitrary"),
        name="rope_tables",
    )(pos.reshape(T, 1), pos.reshape(1, T), _rope_freq(DSA_HD, 128), _rope_freq(DSA_IDX_DIM, 128),
      _rope_inv(DSA_HD).reshape(h128, 1), _rope_inv(DSA_IDX_DIM).reshape(h64, 1))


def _rope(x, cos, sin, head_dim):
    width = x.shape[-1]
    half = head_dim // ROPE_FRAC // 2
    lane = lax.broadcasted_iota(jnp.int32, (1, width), 1)
    per = lane % head_dim
    sgn_lo = jnp.where(per < half, -1.0, 0.0)
    sgn_hi = jnp.where((per >= half) & (per < 2 * half), 1.0, 0.0)
    up = pltpu.roll(x, width - half, axis=1)
    dn = pltpu.roll(x, half, axis=1)
    return x * cos + sin * (up * sgn_lo + dn * sgn_hi)


def _rope_t(xt, cos, sin):
    half = cos.shape[0]
    x1, x2 = xt[:half], xt[half:2 * half]
    return jnp.concatenate([x1 * cos - x2 * sin, x2 * cos + x1 * sin, xt[2 * half:]], axis=0)


def _dsa_prep_kernel(dq_ref, dk_ref, dv_ref, diq_ref, misc_ref, c128_ref, s128_ref, c64_ref, s64_ref,
                     ct128_ref, st128_ref, ct64_ref, st64_ref, qg_ref, kg_ref,
                     qt_out, k_out, vt_out, iqt_out, ik_out, wt_out):
    ct128, st128 = ct128_ref[...], st128_ref[...]
    ct64, st64 = ct64_ref[...], st64_ref[...]
    for h in range(DSA_HEADS):
        xt = dq_ref[:, h * DSA_HD:(h + 1) * DSA_HD].astype(F32).T
        xt = xt * lax.rsqrt(jnp.mean(xt * xt, axis=0, keepdims=True) + NORM_EPS) * qg_ref[...]
        qt_out[h] = (_rope_t(xt, ct128, st128) * (DSA_HD ** -0.5 * LOG2E)).astype(qt_out.dtype)
    k = _rope(_rms(dk_ref[...].astype(F32), kg_ref[...]), c128_ref[...], s128_ref[...], DSA_HD)
    k_out[...] = k.astype(k_out.dtype)
    for c in range(vt_out.shape[0]):
        vt_out[c] = dv_ref[c * DSA_KCHUNK:(c + 1) * DSA_KCHUNK, :].astype(F32).T.astype(vt_out.dtype)
    for j in range(DSA_IDX_HEADS // 2):
        xt = diq_ref[:, j * 128:(j + 1) * 128].astype(F32).T
        for hh in range(2):
            head = xt[hh * DSA_IDX_DIM:(hh + 1) * DSA_IDX_DIM]
            iqt_out[2 * j + hh] = _rope_t(head, ct64, st64).astype(iqt_out.dtype)
    misc = misc_ref[...].astype(F32)
    ik_out[...] = _rope(misc, c64_ref[...], s64_ref[...], DSA_IDX_DIM)[:, :DSA_IDX_DIM].astype(ik_out.dtype)
    wt_out[...] = misc.T[DIW_LANE:DIW_LANE + DSA_IDX_HEADS]


def _dsa_prep(p, tables, q_gain_col, k_gain):
    T = p.shape[0]
    tq = min(512, T)
    assert tq % DSA_KCHUNK == 0
    tab = pl.BlockSpec((tq, 128), lambda i: (i, 0))
    h128, h64 = DSA_HD // ROPE_FRAC // 2, DSA_IDX_DIM // ROPE_FRAC // 2
    tab128 = pl.BlockSpec((h128, tq), lambda i: (0, i))
    tab64 = pl.BlockSpec((h64, tq), lambda i: (0, i))
    return pl.pallas_call(
        _dsa_prep_kernel,
        grid=(T // tq,),
        in_specs=[pl.BlockSpec((tq, 512), lambda i: (i, _COL["dq"] // 512)),
                  pl.BlockSpec((tq, 128), lambda i: (i, _COL["dk"] // 128)),
                  pl.BlockSpec((tq, 128), lambda i: (i, _COL["dv"] // 128)),
                  pl.BlockSpec((tq, 1024), lambda i: (i, _COL["diq"] // 1024)),
                  pl.BlockSpec((tq, 128), lambda i: (i, MISC_COL // 128)),
                  tab, tab, tab, tab, tab128, tab128, tab64, tab64,
                  pl.BlockSpec((DSA_HD, 1), lambda i: (0, 0)),
                  pl.BlockSpec((1, DSA_HD), lambda i: (0, 0))],
        out_specs=[pl.BlockSpec((DSA_HEADS, DSA_HD, tq), lambda i: (0, 0, i)),
                   pl.BlockSpec((tq, DSA_HD), lambda i: (i, 0)),
                   pl.BlockSpec((tq // DSA_KCHUNK, DSA_HD, DSA_KCHUNK), lambda i: (i, 0, 0)),
                   pl.BlockSpec((DSA_IDX_HEADS, DSA_IDX_DIM, tq), lambda i: (0, 0, i)),
                   pl.BlockSpec((tq, DSA_IDX_DIM), lambda i: (i, 0)),
                   pl.BlockSpec((DSA_IDX_HEADS, tq), lambda i: (0, i))],
        out_shape=[jax.ShapeDtypeStruct((DSA_HEADS, DSA_HD, T), BF16),
                   jax.ShapeDtypeStruct((T, DSA_HD), BF16),
                   jax.ShapeDtypeStruct((T // DSA_KCHUNK, DSA_HD, DSA_KCHUNK), BF16),
                   jax.ShapeDtypeStruct((DSA_IDX_HEADS, DSA_IDX_DIM, T), BF16),
                   jax.ShapeDtypeStruct((T, DSA_IDX_DIM), BF16),
                   jax.ShapeDtypeStruct((DSA_IDX_HEADS, T), F32)],
        compiler_params=_cparams("arbitrary"),
        name="dsa_prep",
    )(p, p, p, p, p, *tables, q_gain_col, k_gain)


_SWAP_MASKS = {16: 0x0000FFFF, 8: 0x00FF00FF, 4: 0x0F0F0F0F, 2: 0x33333333, 1: 0x55555555}


def _bit_swap_stage(a, j):
    a = list(a)
    k = 0
    while k < len(a):
        t = (a[k] ^ lax.shift_right_logical(a[k + j], jnp.int32(j))) & jnp.int32(_SWAP_MASKS[j])
        a[k] = a[k] ^ t
        a[k + j] = a[k + j] ^ (t << j)
        k = (k + j + 1) & ~j
    return a


def _dsa_kernel(iqt_ref, wt_ref, qt_ref, ik_ref, k_ref, vt_ref, o_ref,
                score_ref, plane_ref, valid_ref, m_ref, bias_ref, s_ref, p_ref, acc_ref, *, ksel):
    TQ, CK = DSA_QBLOCK, DSA_KCHUNK
    WORD = 32
    i = pl.program_id(1)
    nk = (i * TQ + TQ + CK - 1) // CK
    t_idx = i * TQ + lax.broadcasted_iota(jnp.int32, (1, TQ), 1)
    w_all = wt_ref[...] * (DSA_IDX_HEADS ** -0.5 * DSA_IDX_DIM ** -0.5)
    valid_ref[...] = jnp.zeros_like(valid_ref)
    sub = lax.broadcasted_iota(jnp.int32, (8, 1), 0)
    off = t_idx - (nk - 1) * CK
    diag_valid = jnp.zeros((8, TQ), jnp.int32)
    for p in range(WORD):
        bit = (1 << p) if p < WORD - 1 else INT_MIN
        diag_valid = diag_valid | jnp.where(8 * p + sub <= off, bit, 0)

    def score_chunk(kc, carry):
        QK = CK // 4
        quarters = []
        for part in range(4):
            lhs = ik_ref[0, kc, part * QK:(part + 1) * QK, :]
            acc = None
            for h in range(DSA_IDX_HEADS):
                term = w_all[h:h + 1] * jnp.maximum(_dot(lhs, iqt_ref[h]), 0.0)
                acc = term if acc is None else acc + term
            s_idx = kc * CK + part * QK + lax.broadcasted_iota(jnp.int32, (QK, 1), 0)
            score_ref[kc, part * QK:(part + 1) * QK, :] = jnp.where(s_idx <= t_idx, acc, -jnp.inf)
            bits = lax.bitcast_convert_type(acc, jnp.int32)
            ukey = bits ^ ((bits >> 31) | INT_MIN)
            words = [ukey[8 * (QK // 8 - 1 - j):8 * (QK // 8 - j)] for j in range(QK // 8)]
            for j in (4, 2, 1):
                words = _bit_swap_stage(words, j)
            quarters.append(words)
        halves = [_bit_swap_stage(quarters[1] + quarters[0], 8), _bit_swap_stage(quarters[3] + quarters[2], 8)]
        planes = _bit_swap_stage(halves[1] + halves[0], 16)
        for p in range(WORD):
            plane_ref[kc, p] = planes[p]
        valid_ref[kc] = jnp.where(kc == nk - 1, diag_valid, -1)
        return carry

    lax.fori_loop(0, nk, score_chunk, 0)

    step = 4 if plane_ref.shape[0] % 4 == 0 else 2
    nstep = (nk + step - 1) // step

    def blank(kc, carry):
        plane_ref[kc] = jnp.zeros(plane_ref.shape[1:], jnp.int32)
        score_ref[kc] = jnp.full((CK, TQ), -jnp.inf, F32)
        return carry

    lax.fori_loop(nk, nstep * step, blank, 0)

    def total(acc):
        return jnp.sum(acc.astype(F32), axis=0, keepdims=True)

    def first(it, acc):
        for c in range(step):
            kc = it * step + c
            m = valid_ref[kc]
            m_ref[kc] = m
            acc = acc + lax.population_count(m & plane_ref[kc, 0])
        return acc

    def decide(n, cnt, need, ubits):
        take = cnt >= need
        bit = jnp.left_shift(jnp.int32(1), WORD - 1 - n)
        return take.astype(jnp.int32), jnp.where(take, need, need - cnt), jnp.where(take, ubits | bit, ubits)

    zero = jnp.zeros((8, TQ), jnp.int32)
    state = decide(0, total(lax.fori_loop(0, nstep, first, zero)), jnp.full((1, TQ), float(ksel), F32),
                   jnp.zeros((1, TQ), jnp.int32))

    def sweep(n, carry):
        take = carry[0] != 0

        def body(it, acc):
            for c in range(step):
                kc = it * step + c
                m, pp = m_ref[kc], plane_ref[kc, n - 1]
                m = jnp.where(take, m & pp, m & ~pp)
                m_ref[kc] = m
                acc = acc + lax.population_count(m & plane_ref[kc, n])
            return acc

        return decide(n, total(lax.fori_loop(0, nstep, body, zero)), carry[1], carry[2])

    _, _, ubits = lax.fori_loop(1, WORD, sweep, state)
    fbits = jnp.where(ubits < 0, ubits ^ INT_MIN, ~ubits)
    thr0 = jnp.where((ubits >= 0) & (ubits < 0x007FFFFF), -jnp.inf, lax.bitcast_convert_type(fbits, F32))

    def scan(thr, init, fold):
        tb = jnp.broadcast_to(thr, (8, TQ))[None]

        def body(it, carry):
            for c in range(step):
                carry = fold(carry, score_ref[it * step + c].reshape(CK // 8, 8, TQ), tb)
            return carry

        return lax.fori_loop(0, nstep, body, tuple(jnp.full((8, TQ), v, F32) for v in init))

    def misplaced(thr):
        def fold(carry, sc, tb):
            return (carry[0] + jnp.sum(jnp.where(sc < tb, 0.0, 1.0), axis=0),
                    carry[1] + jnp.sum(jnp.where(sc > tb, 1.0, 0.0), axis=0))

        ge, gt = scan(thr, (0.0, 0.0), fold)
        low = jnp.where(jnp.sum(gt, axis=0, keepdims=True) >= ksel, 1.0, 0.0)
        high = jnp.where(jnp.sum(ge, axis=0, keepdims=True) < ksel, 1.0, 0.0)
        return low, high, jnp.max(low + high)

    def settle(carry):
        thr, low, high, _, tries = carry

        def fold(carry, sc, tb):
            return (jnp.minimum(carry[0], jnp.min(jnp.where(sc > tb, sc, jnp.inf), axis=0)),
                    jnp.maximum(carry[1], jnp.max(jnp.where(sc < tb, sc, -jnp.inf), axis=0)))

        up, dn = scan(thr, (jnp.inf, -jnp.inf), fold)
        thr = jnp.where(low > 0.0, jnp.min(up, axis=0, keepdims=True),
                        jnp.where(high > 0.0, jnp.max(dn, axis=0, keepdims=True), thr))
        return (thr,) + misplaced(thr) + (tries + 1,)

    thr = lax.while_loop(lambda c: (c[3] > 0.0) & (c[4] < ksel), settle,
                         (thr0,) + misplaced(thr0) + (jnp.int32(0),))[0]

    kg = CK // 16
    unit = 64

    def logits(kc, buf, m_run):
        sc = score_ref[kc]
        bias_ref[...] = jnp.where((sc >= thr) & (sc > -jnp.inf), 0.0, -1e30)
        kk = k_ref[0, kc]
        new = []
        for h in range(DSA_HEADS):
            s = _dot(kk, qt_ref[h]) + bias_ref[...]
            s_ref[buf, h] = s
            mx = jnp.max(s.reshape(kg, 2, 8, TQ), axis=0)
            new.append(jnp.maximum(m_run[h], jnp.max(jnp.max(mx, axis=0), axis=0, keepdims=True)))
        return tuple(new)

    def attn_chunk(kc, rd, wr, carry):
        m_old, m_run, ls = carry
        m_next = logits(jnp.minimum(kc + 1, nk - 1), wr, m_run)
        vt = vt_ref[0, kc]
        new_l = []
        for h in range(DSA_HEADS):
            alpha = jnp.exp2(m_old[h] - m_run[h])
            mb = jnp.broadcast_to(m_run[h], (8, TQ))[None]
            part = ls[h] * alpha
            for u in range(CK // unit):
                rows = slice(u * unit, (u + 1) * unit)
                pr = jnp.exp2(s_ref[rd, h, rows, :].reshape(unit // 8, 8, TQ) - mb)
                part = part + jnp.sum(pr.reshape(unit // 16, 2, 8, TQ), axis=0)
                p_ref[h, rows, :] = pr.reshape(unit, TQ).astype(BF16)
            new_l.append(part)
            acc_ref[h] = acc_ref[h] * alpha + _dot(vt, p_ref[h])
        return m_run, m_next, tuple(new_l)

    def attn_pair(j, carry):
        return attn_chunk(2 * j + 1, 1, 0, attn_chunk(2 * j, 0, 1, carry))

    acc_ref[...] = jnp.zeros_like(acc_ref)
    floor = tuple(jnp.full((1, TQ), -1e30, F32) for _ in range(DSA_HEADS))
    init = (floor, logits(0, 0, floor), tuple(jnp.zeros((2, 8, TQ), F32) for _ in range(DSA_HEADS)))
    carry = lax.fori_loop(0, nk // 2, attn_pair, init)
    _, _, ls = lax.cond(nk % 2 == 1, lambda c: attn_chunk(nk - 1, 0, 1, c), lambda c: c, carry)
    for h in range(DSA_HEADS):
        denom = jnp.sum(jnp.sum(ls[h], axis=0), axis=0, keepdims=True)
        o_ref[:, h * DSA_HD:(h + 1) * DSA_HD] = (acc_ref[h] / denom).T.astype(o_ref.dtype)


def _dsa(qt, k_rot, vt, iqt, ik_rot, wt, B, S):
    TQ, CK = DSA_QBLOCK, DSA_KCHUNK
    NC = S // CK
    nq = S // TQ
    assert NC % 2 == 0 and TQ == CK == 8 * 32
    ksel = min(DSA_TOPK, S // 4)
    ik = ik_rot.reshape(B, NC, CK, DSA_IDX_DIM)
    k = k_rot.reshape(B, NC, CK, DSA_HD)
    vt = vt.reshape(B, NC, DSA_HD, CK)
    return pl.pallas_call(
        functools.partial(_dsa_kernel, ksel=ksel),
        grid=(B, nq),
        in_specs=[pl.BlockSpec((DSA_IDX_HEADS, DSA_IDX_DIM, TQ), lambda b, i: (0, 0, b * nq + i)),
                  pl.BlockSpec((DSA_IDX_HEADS, TQ), lambda b, i: (0, b * nq + i)),
                  pl.BlockSpec((DSA_HEADS, DSA_HD, TQ), lambda b, i: (0, 0, b * nq + i)),
                  pl.BlockSpec((1, NC, CK, DSA_IDX_DIM), lambda b, i: (b, 0, 0, 0)),
                  pl.BlockSpec((1, NC, CK, DSA_HD), lambda b, i: (b, 0, 0, 0)),
                  pl.BlockSpec((1, NC, DSA_HD, CK), lambda b, i: (b, 0, 0, 0))],
        out_specs=pl.BlockSpec((TQ, DSA_HEADS * DSA_HD), lambda b, i: (b * nq + i, 0)),
        out_shape=jax.ShapeDtypeStruct((B * S, DSA_HEADS * DSA_HD), BF16),
        scratch_shapes=[pltpu.VMEM((NC, CK, TQ), F32),
                        pltpu.VMEM((NC, 32, 8, TQ), jnp.int32),
                        pltpu.VMEM((NC, 8, TQ), jnp.int32),
                        pltpu.VMEM((NC, 8, TQ), jnp.int32),
                        pltpu.VMEM((CK, TQ), F32),
                        pltpu.VMEM((2, DSA_HEADS, CK, TQ), F32),
                        pltpu.VMEM((DSA_HEADS, CK, TQ), BF16),
                        pltpu.VMEM((DSA_HEADS, DSA_HD, TQ), F32)],
        compiler_params=_cparams("arbitrary", "arbitrary"),
        name="dsa_attn",
    )(iqt, wt, qt, ik, k, vt)


def _mem_kv_kernel(mem_ref, g_ref, w_ref, kg_ref, k_out, v_out):
    h = _rms(mem_ref[...], g_ref[...]).astype(BF16)
    kv = _dot(h, w_ref[...])
    width = MEM_HEADS * MEM_HD
    for hd in range(MEM_HEADS):
        sl = slice(hd * MEM_HD, (hd + 1) * MEM_HD)
        k_out[:, sl] = _rms(kv[:, sl], kg_ref[...]).astype(k_out.dtype)
    v_out[...] = kv[:, width:].astype(v_out.dtype)


def _mem_kv(mem2d, gain, w_kv, k_gain, B, l):
    width = MEM_HEADS * MEM_HD
    out = jax.ShapeDtypeStruct((B * MEM_LEN, width), BF16)
    return pl.pallas_call(
        _mem_kv_kernel,
        grid=(B,),
        in_specs=[pl.BlockSpec((MEM_LEN, D_MODEL), lambda b: (b, 0)),
                  pl.BlockSpec((1, D_MODEL), lambda b: (0, 0)),
                  _layer((D_MODEL, 2 * width), l),
                  pl.BlockSpec((1, MEM_HD), lambda b: (0, 0))],
        out_specs=[pl.BlockSpec((MEM_LEN, width), lambda b: (b, 0))] * 2,
        out_shape=[out, out],
        compiler_params=_cparams("arbitrary"),
        name="mem_kv",
    )(mem2d, gain, w_kv, k_gain)


def _mem_attn_kernel(q_ref, k_ref, v_ref, qg_ref, o_ref):
    for h in range(MEM_HEADS):
        sl = slice(h * MEM_HD, (h + 1) * MEM_HD)
        q = _rms(q_ref[:, sl].astype(F32), qg_ref[...]) * (MEM_HD ** -0.5)
        s = _dot_nt(q.astype(BF16), k_ref[:, sl])
        pr = jnp.exp(s - jnp.max(s, axis=1, keepdims=True))
        o = _dot(pr.astype(BF16), v_ref[:, sl]) / jnp.sum(pr, axis=1, keepdims=True)
        o_ref[:, sl] = o.astype(o_ref.dtype)


def _mem_attn(p, mk, mv, q_gain, B, S):
    tq = min(512, S)
    nq = S // tq
    width = MEM_HEADS * MEM_HD
    return pl.pallas_call(
        _mem_attn_kernel,
        grid=(B, nq),
        in_specs=[pl.BlockSpec((tq, width), lambda b, i: (b * nq + i, _COL["mq"] // width)),
                  pl.BlockSpec((MEM_LEN, width), lambda b, i: (b, 0)),
                  pl.BlockSpec((MEM_LEN, width), lambda b, i: (b, 0)),
                  pl.BlockSpec((1, MEM_HD), lambda b, i: (0, 0))],
        out_specs=pl.BlockSpec((tq, width), lambda b, i: (b * nq + i, 0)),
        out_shape=jax.ShapeDtypeStruct((B * S, width), BF16),
        compiler_params=_cparams("arbitrary", "arbitrary"),
        name="mem_attn",
    )(p, mk, mv, q_gain)


def _out_proj_kernel(og_ref, ys_ref, od_ref, om_ref, z_ref, x_ref, gn_ref, wglu_ref, w_ref, o_ref, lhs_ref):
    y = jax.nn.gelu(jnp.concatenate([ys_ref[q] for q in range(ys_ref.shape[0])], axis=1))
    o_s5 = y * jax.nn.sigmoid(_dot(y.astype(BF16), wglu_ref[...]))
    branches = (og_ref[...].astype(F32), o_s5, od_ref[...].astype(F32), om_ref[...].astype(F32))
    for bi, o in enumerate(branches):
        for h in range(o.shape[1] // HEAD):
            c0 = bi * 512 + h * HEAD
            z = z_ref[:, c0:c0 + HEAD].astype(F32)
            n = _rms(o[:, h * HEAD:(h + 1) * HEAD], gn_ref[:, c0:c0 + HEAD])
            lhs_ref[:, c0:c0 + HEAD] = (n * (z * jax.nn.sigmoid(z))).astype(BF16)
    o_ref[...] = x_ref[...] + _dot(lhs_ref[...], w_ref[...])


def _out_proj(o_gla, y_s5, o_dsa, o_mem, p, x2d, out_gain, w_glu, w_out, l):
    T = x2d.shape[0]
    tm = min(512, T)
    branch = pl.BlockSpec((tm, 512), lambda i: (i, 0))
    return pl.pallas_call(
        _out_proj_kernel,
        grid=(T // tm,),
        in_specs=[branch, pl.BlockSpec((S5_WIDTH // 128, tm, 128), lambda i: (0, i, 0)), branch, branch,
                  pl.BlockSpec((tm, MIX_WIDTH), lambda i: (i, 0)),
                  pl.BlockSpec((tm, D_MODEL), lambda i: (i, 0)),
                  _resident((1, MIX_WIDTH)),
                  _layer((S5_WIDTH, S5_WIDTH), l),
                  _layer((MIX_WIDTH, D_MODEL), l)],
        out_specs=pl.BlockSpec((tm, D_MODEL), lambda i: (i, 0)),
        out_shape=jax.ShapeDtypeStruct((T, D_MODEL), F32),
        scratch_shapes=[pltpu.VMEM((tm, MIX_WIDTH), BF16)],
        compiler_params=_cparams("arbitrary"),
        name="out_proj",
    )(o_gla, y_s5, o_dsa, o_mem, p, x2d, out_gain, w_glu, w_out)


def _permute_kernel(w_ref, o_ref):
    c = 0
    for n in _ORDER:
        lo, hi = _ORIG[n]
        o_ref[c:c + hi - lo, :] = w_ref[lo:hi, :].astype(o_ref.dtype)
        c += hi - lo
    o_ref[c:, :] = jnp.zeros((IN_PAD - c, o_ref.shape[1]), o_ref.dtype)


def _permute_w_in(w_in):
    depth, rows, width = w_in.shape
    tl = 256
    return pl.pallas_call(
        _permute_kernel,
        grid=(depth, rows // tl),
        in_specs=[pl.BlockSpec((None, width, tl), lambda l, i: (l, 0, i))],
        out_specs=pl.BlockSpec((None, IN_PAD, tl), lambda l, i: (l, 0, i)),
        out_shape=jax.ShapeDtypeStruct((depth, IN_PAD, rows), BF16),
        compiler_params=_cparams("arbitrary", "arbitrary"),
        name="permute_w_in",
    )(jnp.swapaxes(w_in, 1, 2))


def kernel(x, mem, positions, ln_gain, w_in, gla_w_up, gla_b_up, s5_a_re, s5_a_im, s5_log_step, s5_b_re, s5_b_im, s5_c_re, s5_c_im, s5_d, s5_w_glu, dsa_q_norm, dsa_k_norm, mem_norm, w_mem_kv, mem_q_norm, mem_k_norm, out_norm, w_out):
    B, S, _ = x.shape
    depth = w_in.shape[0]
    T = B * S
    x2d = x.reshape(T, D_MODEL)
    mem2d = mem.reshape(B * mem.shape[1], D_MODEL)
    w_in_p = _permute_w_in(w_in)
    wup_pad = jnp.zeros((depth, 128, GLA_HEADS * GLA_DK), F32)
    wup_pad = wup_pad.at[:, GLR_LANE:GLR_LANE + GLA_GATE_RANK].set(gla_w_up).astype(BF16)
    w_kv = w_mem_kv.astype(BF16)
    w_glu = s5_w_glu.astype(BF16)
    w_o = w_out.astype(BF16)
    tables = _rope_tables(positions)
    for l in range(depth):
        p, su = _in_proj(x2d, ln_gain[l][None], w_in_p, l)
        o_gla = _gla(p, wup_pad[l], gla_b_up[l][None], B, S)
        mats = _s5_mats(s5_a_re[l], s5_a_im[l], s5_log_step[l], s5_b_re[l], s5_b_im[l],
                        s5_c_re[l], s5_c_im[l], s5_d[l])
        y_s5 = _s5(su, mats, B, S)
        qt, k_rot, vt, iqt, ik_rot, wt = _dsa_prep(p, tables, dsa_q_norm[l][:, None], dsa_k_norm[l][None])
        o_dsa = _dsa(qt, k_rot, vt, iqt, ik_rot, wt, B, S)
        mk, mv = _mem_kv(mem2d, mem_norm[l][None], w_kv, mem_k_norm[l][None], B, l)
        o_mem = _mem_attn(p, mk, mv, mem_q_norm[l][None], B, S)
        x2d = _out_proj(o_gla, y_s5, o_dsa, o_mem, p, x2d, out_norm[l][None], w_glu, w_o, l)
    return x2d.reshape(B, S, D_MODEL)
```

```python
import functools

import numpy as np
import jax
import jax.numpy as jnp
from jax import lax
from jax.experimental import pallas as pl
from jax.experimental.pallas import tpu as pltpu

F32 = jnp.float32
BF16 = jnp.bfloat16

D_MODEL = 2048
MEM_LEN = 256
ROPE_THETA = 500000.0
ROPE_FRAC = 4
NORM_EPS = 1e-6

GLA_HEADS = 4
GLA_DK = 64
GLA_DV = 128
GLA_GATE_RANK = 16
GLA_TAU = 16.0
GLA_CHUNK = 64

S5_GROUPS = 32
S5_CH = 16
S5_STATE = 64
S5_WIDTH = S5_GROUPS * S5_CH
S5_BLOCK = 16

DSA_HEADS = 4
DSA_HD = 128
DSA_IDX_HEADS = 16
DSA_IDX_DIM = 64
DSA_TOPK = 256
DSA_QBLOCK = 256
DSA_KCHUNK = 256

MEM_HEADS = 4
MEM_HD = 128
HEAD = 128
MIX_WIDTH = 2048

VMEM_LIMIT_BYTES = 52 * 1024 * 1024

_ORIG = dict(gq=(0, 256), gk=(256, 512), gv=(512, 1024), glr=(1024, 1040), gz=(1040, 1552),
             su=(1552, 2064), sz=(2064, 2576), dq=(2576, 3088), dk=(3088, 3216), dv=(3216, 3344),
             diq=(3344, 4368), dik=(4368, 4432), diw=(4432, 4448), dz=(4448, 4960),
             mq=(4960, 5472), mz=(5472, 5984))
_ORDER = ("gz", "sz", "dz", "mz", "diq", "gv", "su", "dq", "mq", "gq", "gk", "dk", "dv",
          "dik", "glr", "diw")
_COL = {}
_c = 0
for _n in _ORDER:
    _COL[_n] = _c
    _c += _ORIG[_n][1] - _ORIG[_n][0]
IN_PAD = 6144
MISC_COL = _COL["dik"]
GLR_LANE = 64
DIW_LANE = 80
INT_MIN = -(2 ** 31)
LOG2E = 1.4426950408889634


def _cparams(*sem):
    return pltpu.CompilerParams(dimension_semantics=sem, vmem_limit_bytes=VMEM_LIMIT_BYTES)


def _dot(a, b):
    return jnp.dot(a, b, preferred_element_type=F32)


def _dot_nt(a, b):
    return lax.dot_general(a, b, (((1,), (1,)), ((), ())), preferred_element_type=F32)


def _rms(x, gain):
    return x * lax.rsqrt(jnp.mean(x * x, axis=-1, keepdims=True) + NORM_EPS) * gain


def _in_proj_kernel(x_ref, g_ref, w_ref, o_ref, su_ref, h_ref):
    h_ref[...] = _rms(x_ref[...], g_ref[...]).astype(BF16)
    tn = S5_WIDTH
    for j in range(IN_PAD // tn):
        acc = _dot_nt(h_ref[...], w_ref[j * tn:(j + 1) * tn, :])
        o_ref[:, j * tn:(j + 1) * tn] = acc.astype(o_ref.dtype)
        if j == _COL["su"] // tn:
            for q in range(tn // 128):
                su_ref[q] = acc[:, q * 128:(q + 1) * 128]


def _resident(shape):
    return pl.BlockSpec(shape, lambda *_: (0,) * len(shape), pipeline_mode=pl.Buffered(1))


def _layer(shape, l):
    return pl.BlockSpec((None,) + shape, lambda *_: (l,) + (0,) * len(shape), pipeline_mode=pl.Buffered(1))


def _in_proj(x2d, gain, w, l):
    T = x2d.shape[0]
    tm = min(512, T)
    return pl.pallas_call(
        _in_proj_kernel,
        grid=(T // tm,),
        in_specs=[pl.BlockSpec((tm, D_MODEL), lambda i: (i, 0)),
                  _resident((1, D_MODEL)),
                  _layer((IN_PAD, D_MODEL), l)],
        out_specs=[pl.BlockSpec((tm, IN_PAD), lambda i: (i, 0)),
                   pl.BlockSpec((S5_WIDTH // 128, tm, 128), lambda i: (0, i, 0))],
        out_shape=[jax.ShapeDtypeStruct((T, IN_PAD), BF16),
                   jax.ShapeDtypeStruct((S5_WIDTH // 128, T, 128), F32)],
        scratch_shapes=[pltpu.VMEM((tm, D_MODEL), BF16)],
        compiler_params=_cparams("arbitrary"),
        name="in_proj",
    )(x2d, gain, w)


def _gla_kernel(q_ref, k_ref, v_ref, misc_ref, wup_ref, bup_ref, o_ref, st_ref):
    @pl.when(pl.program_id(1) == 0)
    def _():
        st_ref[...] = jnp.zeros_like(st_ref)

    C = GLA_CHUNK
    tc = q_ref.shape[0]
    nchunk = tc // C
    x = _dot(misc_ref[...], wup_ref[...]) + bup_ref[...]
    log_a = (jnp.minimum(x, 0.0) - jnp.log(1.0 + jnp.exp(-jnp.abs(x)))) * (1.0 / GLA_TAU)
    row = lax.broadcasted_iota(jnp.int32, (tc, tc), 0)
    col = lax.broadcasted_iota(jnp.int32, (tc, tc), 1)
    tril = (row >= col) & ((row // C) == (col // C))
    la_hi = log_a.astype(BF16)
    r1 = log_a - la_hi.astype(F32)
    la_mid = r1.astype(BF16)
    la_lo = (r1 - la_mid.astype(F32)).astype(BF16)
    tri3 = jnp.concatenate([tril.astype(BF16)] * 3, axis=1)
    b = _dot(tri3, jnp.concatenate([la_hi, la_mid, la_lo], axis=0))
    lane = lax.broadcasted_iota(jnp.int32, (1, 2 * GLA_DK), 1)
    low = lane < GLA_DK
    srow = lax.broadcasted_iota(jnp.int32, (2 * GLA_DV, 1), 0)
    rowc = lax.broadcasted_iota(jnp.int32, (tc, 1), 0)
    diag = (srow < GLA_DV) == low
    for j in range(GLA_HEADS // 2):
        ls = slice(j * 128, (j + 1) * 128)
        bp = b[:, ls]
        mid = jnp.concatenate([jnp.broadcast_to(bp[c * C + C // 2:c * C + C // 2 + 1], (C, 128))
                               for c in range(nchunk)], axis=0)
        last = jnp.concatenate([jnp.broadcast_to(bp[c * C + C - 1:c * C + C], (C, 128))
                                for c in range(nchunk)], axis=0)
        qf = q_ref[:, ls].astype(F32) * (GLA_DK ** -0.5)
        kf = k_ref[:, ls].astype(F32)
        qe = qf * jnp.exp(bp - mid)
        ke_p = (kf * jnp.exp(mid - bp)).astype(BF16)
        kd_p = (kf * jnp.exp(last - bp)).astype(BF16)
        qi_p = (qf * jnp.exp(bp)).astype(BF16)
        vs = slice(2 * j * GLA_DV, (2 * j + 2) * GLA_DV)
        q2 = jnp.concatenate([jnp.where(low, qe, 0.0), jnp.where(low, 0.0, qe)], axis=0).astype(BF16)
        sc = _dot_nt(q2, ke_p)
        attn = jnp.concatenate([jnp.where(tril, sc[:tc], 0.0), jnp.where(tril, sc[tc:], 0.0)], axis=1)
        vp = v_ref[:, vs]
        first = lax.broadcasted_iota(jnp.int32, (1, 2 * GLA_DV), 1) < GLA_DV
        v2 = jnp.concatenate([jnp.where(first, vp, jnp.zeros_like(vp)),
                              jnp.where(first, jnp.zeros_like(vp), vp)], axis=0)
        intra = _dot(attn.astype(BF16), v2)
        in_chunk = [(rowc >= c * C) & (rowc < (c + 1) * C) for c in range(nchunk)]
        kd_blk = jnp.concatenate([jnp.where(in_chunk[c], kd_p, jnp.zeros_like(kd_p)) for c in range(nchunk)], axis=1)
        qi_blk = jnp.concatenate([jnp.where(in_chunk[c], qi_p, jnp.zeros_like(qi_p)) for c in range(nchunk)], axis=1)
        v_t = v_ref[:, vs].astype(F32).T.astype(BF16)
        ds = _dot(v_t, kd_blk)
        st = st_ref[j]
        states = []
        for c in range(nchunk):
            states.append(st.astype(BF16))
            dec = jnp.exp(bp[c * C + C - 1:c * C + C])
            st = jnp.where(diag, st * dec + ds[:, c * 128:(c + 1) * 128], 0.0)
        st_ref[j] = st
        o_ref[:, vs] = (intra + _dot_nt(qi_blk, jnp.concatenate(states, axis=1))).astype(o_ref.dtype)


def _gla(p, wup_pad, bup, B, S):
    tc = min(256, S)
    nc = S // tc
    row = lambda b, c: b * nc + c
    return pl.pallas_call(
        _gla_kernel,
        grid=(B, nc),
        in_specs=[pl.BlockSpec((tc, 256), lambda b, c: (row(b, c), _COL["gq"] // 256)),
                  pl.BlockSpec((tc, 256), lambda b, c: (row(b, c), _COL["gk"] // 256)),
                  pl.BlockSpec((tc, 512), lambda b, c: (row(b, c), _COL["gv"] // 512)),
                  pl.BlockSpec((tc, 128), lambda b, c: (row(b, c), MISC_COL // 128)),
                  _resident((128, 256)),
                  _resident((1, 256))],
        out_specs=pl.BlockSpec((tc, 512), lambda b, c: (row(b, c), 0)),
        out_shape=jax.ShapeDtypeStruct((B * S, GLA_HEADS * GLA_DV), BF16),
        scratch_shapes=[pltpu.VMEM((GLA_HEADS // 2, 2 * GLA_DV, 2 * GLA_DK), F32)],
        compiler_params=_cparams("arbitrary", "arbitrary"),
        name="gla",
    )(p, p, p, p, wup_pad, bup)


def _s5_mats(a_re, a_im, log_step, b_re, b_im, c_re, c_im, d_skip):
    L, P, CH, G = S5_BLOCK, S5_STATE, S5_CH, S5_GROUPS
    hi = lax.Precision.HIGHEST
    step = jnp.exp(log_step.astype(F32))[:, None]
    are, aim = a_re.astype(F32), a_im.astype(F32)

    def power(j):
        mag = jnp.exp(are[:, None, :] * step[:, None, :] * j[None, :, None])
        ang = aim[:, None, :] * step[:, None, :] * j[None, :, None]
        return mag * jnp.cos(ang), mag * jnp.sin(ang)

    t = jnp.arange(L, dtype=F32)
    lb_re, lb_im = (v[:, 0] for v in power(jnp.ones((1,), F32)))
    den = are * are + aim * aim
    f_re = ((lb_re - 1.0) * are + lb_im * aim) / den
    f_im = (lb_im * are - (lb_re - 1.0) * aim) / den
    bt_re, bt_im = jnp.swapaxes(b_re, 1, 2), jnp.swapaxes(b_im, 1, 2)
    bb_re = f_re[:, None] * bt_re - f_im[:, None] * bt_im
    bb_im = f_re[:, None] * bt_im + f_im[:, None] * bt_re
    n_re, n_im = power(-t)
    a_r = (n_re[:, :, None] * bb_re[:, None] - n_im[:, :, None] * bb_im[:, None]).reshape(G, L * CH, P)
    a_i = (n_re[:, :, None] * bb_im[:, None] + n_im[:, :, None] * bb_re[:, None]).reshape(G, L * CH, P)
    p_re, p_im = power(t)
    ct_re, ct_im = jnp.swapaxes(c_re, 1, 2), jnp.swapaxes(c_im, 1, 2)
    pt_re, pt_im = jnp.swapaxes(p_re, 1, 2), jnp.swapaxes(p_im, 1, 2)
    b_r = (pt_re[..., None] * ct_re[:, :, None] - pt_im[..., None] * ct_im[:, :, None]).reshape(G, P, L * CH)
    b_i = (pt_re[..., None] * ct_im[:, :, None] + pt_im[..., None] * ct_re[:, :, None]).reshape(G, P, L * CH)
    m = (jnp.einsum("gxp,gpy->gxy", a_r, b_r, precision=hi) - jnp.einsum("gxp,gpy->gxy", a_i, b_i, precision=hi))
    blk = np.arange(L * CH) // CH
    m = jnp.where((blk[None, :] >= blk[:, None])[None], m, 0.0)
    m = m + jnp.eye(L * CH, dtype=F32)[None] * jnp.tile(d_skip.reshape(G, 1, CH), (1, 1, L))
    e_re, e_im = (v[:, 0][:, None] for v in power(jnp.full((1,), L - 1.0, F32)))
    pr, pi = a_r * e_re - a_i * e_im, a_r * e_im + a_i * e_re
    pm = jnp.concatenate([pr, pi, pi, pr], axis=-1)
    l_re, l_im = lb_re[..., None], lb_im[..., None]
    qm = jnp.concatenate([b_r * l_re - b_i * l_im, -(b_r * l_im + b_i * l_re)], axis=1)
    al_re, al_im = (v[:, 0] for v in power(jnp.full((1,), float(L), F32)))
    ar = jnp.concatenate([al_re, al_re], axis=-1).reshape(1, G * 2 * P)
    ai_a = jnp.concatenate([-al_im, al_im], axis=-1).reshape(1, G * 2 * P)
    ai_b = jnp.concatenate([al_im, -al_im], axis=-1).reshape(1, G * 2 * P)
    return m.astype(BF16), pm.astype(BF16), qm.astype(BF16), ar, ai_a, ai_b


def _s5_perm():
    src = np.arange(1024).reshape(8, 8, S5_CH).transpose(1, 0, 2).reshape(-1)
    m = np.zeros((1024, 1024), np.float32)
    m[np.arange(1024), src] = 1.0
    return jnp.asarray(m, BF16)


def _s5_kernel(su_ref, perm_ref, m_ref, p_ref, q_ref, ar_ref, aia_ref, aib_ref, y_ref,
               xcat_ref, u_ref, va_ref, vb_ref, x_ref, ycat_ref):
    L = S5_BLOCK
    NB = su_ref.shape[1] // L
    GH = 8
    W = 2 * S5_STATE
    for s in range(L):
        xcat_ref[:, s * 128:(s + 1) * 128] = su_ref[0, pl.ds(s, NB, stride=L), :].astype(BF16)
    for half in range(2):
        u_ref[half] = _dot(xcat_ref[:, half * 1024:(half + 1) * 1024], perm_ref[...]).astype(BF16)

    def u_of(g):
        return jnp.concatenate([u_ref[0, :, g * 128:(g + 1) * 128], u_ref[1, :, g * 128:(g + 1) * 128]], axis=1)

    for g in range(GH):
        v2 = _dot(u_of(g), p_ref[g])
        va_ref[:, g * W:(g + 1) * W] = v2[:, :W]
        vb_ref[:, g * W:(g + 1) * W] = v2[:, W:]
    ar, aia, aib = ar_ref[...], aia_ref[...], aib_ref[...]

    def step(n, carry):
        xa, xb = carry
        x_ref[pl.ds(n, 1), :] = xa
        na = xa * ar + xb * aia + va_ref[pl.ds(n, 1), :]
        nb = xb * ar + xa * aib + vb_ref[pl.ds(n, 1), :]
        return na, nb

    z = jnp.zeros((1, GH * W), F32)
    lax.fori_loop(0, NB, step, (z, z), unroll=8)
    for g in range(GH):
        y = _dot(u_of(g), m_ref[g]) + _dot(x_ref[:, g * W:(g + 1) * W].astype(BF16), q_ref[g])
        for half in range(2):
            ycat_ref[half, :, g * 128:(g + 1) * 128] = y[:, half * 128:(half + 1) * 128].astype(BF16)
    for half in range(2):
        out = _dot(ycat_ref[half], perm_ref[...])
        for t in range(8):
            y_ref[0, pl.ds(half * 8 + t, NB, stride=L), :] = out[:, t * 128:(t + 1) * 128]


def _s5(su, mats, B, S):
    m, pm, qm, ar, aia, aib = mats
    L, G, CH = S5_BLOCK, S5_GROUPS, S5_CH
    NB = S // L
    GH = 8
    W = 2 * S5_STATE
    slab = pl.BlockSpec((1, S, 128), lambda b, h: (h, b, 0))
    coef = pl.BlockSpec((1, GH * W), lambda b, h: (0, h))
    return pl.pallas_call(
        _s5_kernel,
        grid=(B, G // GH),
        in_specs=[slab,
                  pl.BlockSpec((1024, 1024), lambda b, h: (0, 0)),
                  pl.BlockSpec((GH, L * CH, L * CH), lambda b, h: (h, 0, 0)),
                  pl.BlockSpec((GH, L * CH, 2 * W), lambda b, h: (h, 0, 0)),
                  pl.BlockSpec((GH, W, L * CH), lambda b, h: (h, 0, 0)),
                  coef, coef, coef],
        out_specs=slab,
        out_shape=jax.ShapeDtypeStruct((G // GH, B * S, 128), F32),
        scratch_shapes=[pltpu.VMEM((NB, L * 128), BF16), pltpu.VMEM((2, NB, 1024), BF16),
                        pltpu.VMEM((NB, GH * W), F32), pltpu.VMEM((NB, GH * W), F32),
                        pltpu.VMEM((NB, GH * W), F32), pltpu.VMEM((2, NB, 1024), BF16)],
        compiler_params=_cparams("arbitrary", "arbitrary"),
        name="s5_scan",
    )(su, _s5_perm(), m, pm, qm, ar, aia, aib)


def _rope_kernel(pos_ref, prow_ref, f128_ref, f64_ref, i128_ref, i64_ref,
                 c128_ref, s128_ref, c64_ref, s64_ref, ct128_ref, st128_ref, ct64_ref, st64_ref):
    pos = pos_ref[...]
    a = pos * f128_ref[...]
    c128_ref[...] = jnp.cos(a)
    s128_ref[...] = jnp.sin(a)
    a = pos * f64_ref[...]
    c64_ref[...] = jnp.cos(a)
    s64_ref[...] = jnp.sin(a)
    a = i128_ref[...] * prow_ref[...]
    ct128_ref[...] = jnp.cos(a)
    st128_ref[...] = jnp.sin(a)
    a = i64_ref[...] * prow_ref[...]
    ct64_ref[...] = jnp.cos(a)
    st64_ref[...] = jnp.sin(a)


def _rope_inv(d):
    rd = d // ROPE_FRAC
    return ROPE_THETA ** (-jnp.arange(rd // 2, dtype=F32) * 2.0 / rd)


def _rope_freq(d, width):
    inv = _rope_inv(d)
    per_head = jnp.concatenate([inv, inv, jnp.zeros((d - d // ROPE_FRAC,), F32)])
    return jnp.tile(per_head, width // d).reshape(1, width)


def _rope_tables(positions):
    T = positions.size
    pos = positions.astype(F32)
    tq = min(512, T)
    h128, h64 = DSA_HD // ROPE_FRAC // 2, DSA_IDX_DIM // ROPE_FRAC // 2
    full = lambda shape: pl.BlockSpec(shape, lambda i: (0, 0))
    return pl.pallas_call(
        _rope_kernel,
        grid=(T // tq,),
        in_specs=[pl.BlockSpec((tq, 1), lambda i: (i, 0)), pl.BlockSpec((1, tq), lambda i: (0, i)),
                  full((1, 128)), full((1, 128)), full((h128, 1)), full((h64, 1))],
        out_specs=[pl.BlockSpec((tq, 128), lambda i: (i, 0))] * 4
        + [pl.BlockSpec((h128, tq), lambda i: (0, i))] * 2 + [pl.BlockSpec((h64, tq), lambda i: (0, i))] * 2,
        out_shape=[jax.ShapeDtypeStruct((T, 128), F32)] * 4
        + [jax.ShapeDtypeStruct((h128, T), F32)] * 2 + [jax.ShapeDtypeStruct((h64, T), F32)] * 2,
        compiler_params=_cparams("arbitrary"),
        name="rope_tables",
    )(pos.reshape(T, 1), pos.reshape(1, T), _rope_freq(DSA_HD, 128), _rope_freq(DSA_IDX_DIM, 128),
      _rope_inv(DSA_HD).reshape(h128, 1), _rope_inv(DSA_IDX_DIM).reshape(h64, 1))


def _rope(x, cos, sin, head_dim):
    width = x.shape[-1]
    half = head_dim // ROPE_FRAC // 2
    lane = lax.broadcasted_iota(jnp.int32, (1, width), 1)
    per = lane % head_dim
    sgn_lo = jnp.where(per < half, -1.0, 0.0)
    sgn_hi = jnp.where((per >= half) & (per < 2 * half), 1.0, 0.0)
    up = pltpu.roll(x, width - half, axis=1)
    dn = pltpu.roll(x, half, axis=1)
    return x * cos + sin * (up * sgn_lo + dn * sgn_hi)


def _rope_t(xt, cos, sin):
    half = cos.shape[0]
    x1, x2 = xt[:half], xt[half:2 * half]
    return jnp.concatenate([x1 * cos - x2 * sin, x2 * cos + x1 * sin, xt[2 * half:]], axis=0)


def _dsa_prep_kernel(dq_ref, dk_ref, dv_ref, diq_ref, misc_ref, c128_ref, s128_ref, c64_ref, s64_ref,
                     ct128_ref, st128_ref, ct64_ref, st64_ref, qg_ref, kg_ref,
                     qt_out, k_out, vt_out, iqt_out, ik_out, wt_out):
    ct128, st128 = ct128_ref[...], st128_ref[...]
    ct64, st64 = ct64_ref[...], st64_ref[...]
    for h in range(DSA_HEADS):
        xt = dq_ref[:, h * DSA_HD:(h + 1) * DSA_HD].astype(F32).T
        xt = xt * lax.rsqrt(jnp.mean(xt * xt, axis=0, keepdims=True) + NORM_EPS) * qg_ref[...]
        qt_out[h] = (_rope_t(xt, ct128, st128) * (DSA_HD ** -0.5 * LOG2E)).astype(qt_out.dtype)
    k = _rope(_rms(dk_ref[...].astype(F32), kg_ref[...]), c128_ref[...], s128_ref[...], DSA_HD)
    k_out[...] = k.astype(k_out.dtype)
    for c in range(vt_out.shape[0]):
        vt_out[c] = dv_ref[c * DSA_KCHUNK:(c + 1) * DSA_KCHUNK, :].astype(F32).T.astype(vt_out.dtype)
    for j in range(DSA_IDX_HEADS // 2):
        xt = diq_ref[:, j * 128:(j + 1) * 128].astype(F32).T
        for hh in range(2):
            head = xt[hh * DSA_IDX_DIM:(hh + 1) * DSA_IDX_DIM]
            iqt_out[2 * j + hh] = _rope_t(head, ct64, st64).astype(iqt_out.dtype)
    misc = misc_ref[...].astype(F32)
    ik_out[...] = _rope(misc, c64_ref[...], s64_ref[...], DSA_IDX_DIM)[:, :DSA_IDX_DIM].astype(ik_out.dtype)
    wt_out[...] = misc.T[DIW_LANE:DIW_LANE + DSA_IDX_HEADS]


def _dsa_prep(p, tables, q_gain_col, k_gain):
    T = p.shape[0]
    tq = min(512, T)
    assert tq % DSA_KCHUNK == 0
    tab = pl.BlockSpec((tq, 128), lambda i: (i, 0))
    h128, h64 = DSA_HD // ROPE_FRAC // 2, DSA_IDX_DIM // ROPE_FRAC // 2
    tab128 = pl.BlockSpec((h128, tq), lambda i: (0, i))
    tab64 = pl.BlockSpec((h64, tq), lambda i: (0, i))
    return pl.pallas_call(
        _dsa_prep_kernel,
        grid=(T // tq,),
        in_specs=[pl.BlockSpec((tq, 512), lambda i: (i, _COL["dq"] // 512)),
                  pl.BlockSpec((tq, 128), lambda i: (i, _COL["dk"] // 128)),
                  pl.BlockSpec((tq, 128), lambda i: (i, _COL["dv"] // 128)),
                  pl.BlockSpec((tq, 1024), lambda i: (i, _COL["diq"] // 1024)),
                  pl.BlockSpec((tq, 128), lambda i: (i, MISC_COL // 128)),
                  tab, tab, tab, tab, tab128, tab128, tab64, tab64,
                  pl.BlockSpec((DSA_HD, 1), lambda i: (0, 0)),
                  pl.BlockSpec((1, DSA_HD), lambda i: (0, 0))],
        out_specs=[pl.BlockSpec((DSA_HEADS, DSA_HD, tq), lambda i: (0, 0, i)),
                   pl.BlockSpec((tq, DSA_HD), lambda i: (i, 0)),
                   pl.BlockSpec((tq // DSA_KCHUNK, DSA_HD, DSA_KCHUNK), lambda i: (i, 0, 0)),
                   pl.BlockSpec((DSA_IDX_HEADS, DSA_IDX_DIM, tq), lambda i: (0, 0, i)),
                   pl.BlockSpec((tq, DSA_IDX_DIM), lambda i: (i, 0)),
                   pl.BlockSpec((DSA_IDX_HEADS, tq), lambda i: (0, i))],
        out_shape=[jax.ShapeDtypeStruct((DSA_HEADS, DSA_HD, T), BF16),
                   jax.ShapeDtypeStruct((T, DSA_HD), BF16),
                   jax.ShapeDtypeStruct((T // DSA_KCHUNK, DSA_HD, DSA_KCHUNK), BF16),
                   jax.ShapeDtypeStruct((DSA_IDX_HEADS, DSA_IDX_DIM, T), BF16),
                   jax.ShapeDtypeStruct((T, DSA_IDX_DIM), BF16),
                   jax.ShapeDtypeStruct((DSA_IDX_HEADS, T), F32)],
        compiler_params=_cparams("arbitrary"),
        name="dsa_prep",
    )(p, p, p, p, p, *tables, q_gain_col, k_gain)


_SWAP_MASKS = {16: 0x0000FFFF, 8: 0x00FF00FF, 4: 0x0F0F0F0F, 2: 0x33333333, 1: 0x55555555}


def _bit_swap_stage(a, j):
    a = list(a)
    k = 0
    while k < len(a):
        t = (a[k] ^ lax.shift_right_logical(a[k + j], jnp.int32(j))) & jnp.int32(_SWAP_MASKS[j])
        a[k] = a[k] ^ t
        a[k + j] = a[k + j] ^ (t << j)
        k = (k + j + 1) & ~j
    return a


def _dsa_kernel(iqt_ref, wt_ref, qt_ref, ik_ref, k_ref, vt_ref, o_ref,
                score_ref, plane_ref, valid_ref, m_ref, bias_ref, qcat_ref, s_ref, p_ref, acc_ref, *, ksel):
    TQ, CK = DSA_QBLOCK, DSA_KCHUNK
    WORD = 32
    i = pl.program_id(1)
    nk = (i * TQ + TQ + CK - 1) // CK
    t_idx = i * TQ + lax.broadcasted_iota(jnp.int32, (1, TQ), 1)
    w_all = wt_ref[...] * (DSA_IDX_HEADS ** -0.5 * DSA_IDX_DIM ** -0.5)
    valid_ref[...] = jnp.zeros_like(valid_ref)
    sub = lax.broadcasted_iota(jnp.int32, (8, 1), 0)
    off = t_idx - (nk - 1) * CK
    diag_valid = jnp.zeros((8, TQ), jnp.int32)
    for p in range(WORD):
        bit = (1 << p) if p < WORD - 1 else INT_MIN
        diag_valid = diag_valid | jnp.where(8 * p + sub <= off, bit, 0)

    def score_chunk(kc, carry):
        QK = CK // 4
        quarters = []
        for part in range(4):
            lhs = ik_ref[0, kc, part * QK:(part + 1) * QK, :]
            acc = None
            for h in range(DSA_IDX_HEADS):
                term = w_all[h:h + 1] * jnp.maximum(_dot(lhs, iqt_ref[h]), 0.0)
                acc = term if acc is None else acc + term
            s_idx = kc * CK + part * QK + lax.broadcasted_iota(jnp.int32, (QK, 1), 0)
            score_ref[kc, part * QK:(part + 1) * QK, :] = jnp.where(s_idx <= t_idx, acc, -jnp.inf)
            bits = lax.bitcast_convert_type(acc, jnp.int32)
            ukey = bits ^ ((bits >> 31) | INT_MIN)
            words = [ukey[8 * (QK // 8 - 1 - j):8 * (QK // 8 - j)] for j in range(QK // 8)]
            for j in (4, 2, 1):
                words = _bit_swap_stage(words, j)
            quarters.append(words)
        halves = [_bit_swap_stage(quarters[1] + quarters[0], 8), _bit_swap_stage(quarters[3] + quarters[2], 8)]
        planes = _bit_swap_stage(halves[1] + halves[0], 16)
        for p in range(WORD):
            plane_ref[kc, p] = planes[p]
        valid_ref[kc] = jnp.where(kc == nk - 1, diag_valid, -1)
        return carry

    lax.fori_loop(0, nk, score_chunk, 0)

    step = 4 if plane_ref.shape[0] % 4 == 0 else 2
    nstep = (nk + step - 1) // step

    def blank(kc, carry):
        plane_ref[kc] = jnp.zeros(plane_ref.shape[1:], jnp.int32)
        score_ref[kc] = jnp.full((CK, TQ), -jnp.inf, F32)
        return carry

    lax.fori_loop(nk, nstep * step, blank, 0)

    def total(acc):
        return jnp.sum(acc.astype(F32), axis=0, keepdims=True)

    def first(it, acc):
        for c in range(step):
            kc = it * step + c
            m = valid_ref[kc]
            m_ref[kc] = m
            acc = acc + lax.population_count(m & plane_ref[kc, 0])
        return acc

    def decide(n, cnt, need, ubits):
        take = cnt >= need
        bit = jnp.left_shift(jnp.int32(1), WORD - 1 - n)
        return take.astype(jnp.int32), jnp.where(take, need, need - cnt), jnp.where(take, ubits | bit, ubits)

    zero = jnp.zeros((8, TQ), jnp.int32)
    state = decide(0, total(lax.fori_loop(0, nstep, first, zero)), jnp.full((1, TQ), float(ksel), F32),
                   jnp.zeros((1, TQ), jnp.int32))

    def sweep(n, carry):
        take = carry[0] != 0

        def body(it, acc):
            for c in range(step):
                kc = it * step + c
                m, pp = m_ref[kc], plane_ref[kc, n - 1]
                m = jnp.where(take, m & pp, m & ~pp)
                m_ref[kc] = m
                acc = acc + lax.population_count(m & plane_ref[kc, n])
            return acc

        return decide(n, total(lax.fori_loop(0, nstep, body, zero)), carry[1], carry[2])

    _, _, ubits = lax.fori_loop(1, WORD, sweep, state)
    fbits = jnp.where(ubits < 0, ubits ^ INT_MIN, ~ubits)
    thr0 = jnp.where((ubits >= 0) & (ubits < 0x007FFFFF), -jnp.inf, lax.bitcast_convert_type(fbits, F32))

    def scan(thr, init, fold):
        tb = jnp.broadcast_to(thr, (8, TQ))[None]

        def body(it, carry):
            for c in range(step):
                carry = fold(carry, score_ref[it * step + c].reshape(CK // 8, 8, TQ), tb)
            return carry

        return lax.fori_loop(0, nstep, body, tuple(jnp.full((8, TQ), v, F32) for v in init))

    def misplaced(thr):
        def fold(carry, sc, tb):
            return (carry[0] + jnp.sum(jnp.where(sc < tb, 0.0, 1.0), axis=0),
                    carry[1] + jnp.sum(jnp.where(sc > tb, 1.0, 0.0), axis=0))

        ge, gt = scan(thr, (0.0, 0.0), fold)
        low = jnp.where(jnp.sum(gt, axis=0, keepdims=True) >= ksel, 1.0, 0.0)
        high = jnp.where(jnp.sum(ge, axis=0, keepdims=True) < ksel, 1.0, 0.0)
        return low, high, jnp.max(low + high)

    def settle(carry):
        thr, low, high, _, tries = carry

        def fold(carry, sc, tb):
            return (jnp.minimum(carry[0], jnp.min(jnp.where(sc > tb, sc, jnp.inf), axis=0)),
                    jnp.maximum(carry[1], jnp.max(jnp.where(sc < tb, sc, -jnp.inf), axis=0)))

        up, dn = scan(thr, (jnp.inf, -jnp.inf), fold)
        thr = jnp.where(low > 0.0, jnp.min(up, axis=0, keepdims=True),
                        jnp.where(high > 0.0, jnp.max(dn, axis=0, keepdims=True), thr))
        return (thr,) + misplaced(thr) + (tries + 1,)

    thr = lax.while_loop(lambda c: (c[3] > 0.0) & (c[4] < ksel), settle,
                         (thr0,) + misplaced(thr0) + (jnp.int32(0),))[0]

    kg = CK // 16
    unit = 64

    def logits(kc, buf, m_run):
        sc = score_ref[kc]
        bias_ref[...] = jnp.where((sc >= thr) & (sc > -jnp.inf), 0.0, -1e30)
        s_all = _dot(k_ref[0, kc], qcat_ref[...])
        new = []
        for h in range(DSA_HEADS):
            s = s_all[:, h * TQ:(h + 1) * TQ] + bias_ref[...]
            s_ref[buf, h] = s
            mx = jnp.max(s.reshape(kg, 2, 8, TQ), axis=0)
            new.append(jnp.maximum(m_run[h], jnp.max(jnp.max(mx, axis=0), axis=0, keepdims=True)))
        return tuple(new)

    def attn_chunk(kc, rd, wr, carry):
        m_old, m_run, ls = carry
        m_next = logits(jnp.minimum(kc + 1, nk - 1), wr, m_run)
        vt = vt_ref[0, kc]
        new_l = []
        for h in range(DSA_HEADS):
            alpha = jnp.exp2(m_old[h] - m_run[h])
            mb = jnp.broadcast_to(m_run[h], (8, TQ))[None]
            part = ls[h] * alpha
            for u in range(CK // unit):
                rows = slice(u * unit, (u + 1) * unit)
                pr = jnp.exp2(s_ref[rd, h, rows, :].reshape(unit // 8, 8, TQ) - mb)
                part = part + jnp.sum(pr.reshape(unit // 16, 2, 8, TQ), axis=0)
                p_ref[h, rows, :] = pr.reshape(unit, TQ).astype(BF16)
            new_l.append(part)
            acc_ref[h] = acc_ref[h] * alpha + _dot(vt, p_ref[h])
        return m_run, m_next, tuple(new_l)

    def attn_pair(j, carry):
        return attn_chunk(2 * j + 1, 1, 0, attn_chunk(2 * j, 0, 1, carry))

    acc_ref[...] = jnp.zeros_like(acc_ref)
    for h in range(DSA_HEADS):
        qcat_ref[:, h * TQ:(h + 1) * TQ] = qt_ref[h]
    floor = tuple(jnp.full((1, TQ), -1e30, F32) for _ in range(DSA_HEADS))
    init = (floor, logits(0, 0, floor), tuple(jnp.zeros((2, 8, TQ), F32) for _ in range(DSA_HEADS)))
    carry = lax.fori_loop(0, nk // 2, attn_pair, init)
    _, _, ls = lax.cond(nk % 2 == 1, lambda c: attn_chunk(nk - 1, 0, 1, c), lambda c: c, carry)
    for h in range(DSA_HEADS):
        denom = jnp.sum(jnp.sum(ls[h], axis=0), axis=0, keepdims=True)
        o_ref[:, h * DSA_HD:(h + 1) * DSA_HD] = (acc_ref[h] / denom).T.astype(o_ref.dtype)


def _dsa(qt, k_rot, vt, iqt, ik_rot, wt, B, S):
    TQ, CK = DSA_QBLOCK, DSA_KCHUNK
    NC = S // CK
    nq = S // TQ
    assert NC % 2 == 0 and TQ == CK == 8 * 32
    ksel = min(DSA_TOPK, S // 4)
    ik = ik_rot.reshape(B, NC, CK, DSA_IDX_DIM)
    k = k_rot.reshape(B, NC, CK, DSA_HD)
    vt = vt.reshape(B, NC, DSA_HD, CK)
    return pl.pallas_call(
        functools.partial(_dsa_kernel, ksel=ksel),
        grid=(B, nq),
        in_specs=[pl.BlockSpec((DSA_IDX_HEADS, DSA_IDX_DIM, TQ), lambda b, i: (0, 0, b * nq + i)),
                  pl.BlockSpec((DSA_IDX_HEADS, TQ), lambda b, i: (0, b * nq + i)),
                  pl.BlockSpec((DSA_HEADS, DSA_HD, TQ), lambda b, i: (0, 0, b * nq + i)),
                  pl.BlockSpec((1, NC, CK, DSA_IDX_DIM), lambda b, i: (b, 0, 0, 0)),
                  pl.BlockSpec((1, NC, CK, DSA_HD), lambda b, i: (b, 0, 0, 0)),
                  pl.BlockSpec((1, NC, DSA_HD, CK), lambda b, i: (b, 0, 0, 0))],
        out_specs=pl.BlockSpec((TQ, DSA_HEADS * DSA_HD), lambda b, i: (b * nq + i, 0)),
        out_shape=jax.ShapeDtypeStruct((B * S, DSA_HEADS * DSA_HD), BF16),
        scratch_shapes=[pltpu.VMEM((NC, CK, TQ), F32),
                        pltpu.VMEM((NC, 32, 8, TQ), jnp.int32),
                        pltpu.VMEM((NC, 8, TQ), jnp.int32),
                        pltpu.VMEM((NC, 8, TQ), jnp.int32),
                        pltpu.VMEM((CK, TQ), F32),
                        pltpu.VMEM((DSA_HD, DSA_HEADS * TQ), BF16),
                        pltpu.VMEM((2, DSA_HEADS, CK, TQ), F32),
                        pltpu.VMEM((DSA_HEADS, CK, TQ), BF16),
                        pltpu.VMEM((DSA_HEADS, DSA_HD, TQ), F32)],
        compiler_params=_cparams("arbitrary", "arbitrary"),
        name="dsa_attn",
    )(iqt, wt, qt, ik, k, vt)


def _mem_kv_kernel(mem_ref, g_ref, w_ref, kg_ref, k_out, v_out):
    h = _rms(mem_ref[...], g_ref[...]).astype(BF16)
    kv = _dot(h, w_ref[...])
    width = MEM_HEADS * MEM_HD
    for hd in range(MEM_HEADS):
        sl = slice(hd * MEM_HD, (hd + 1) * MEM_HD)
        k_out[:, sl] = _rms(kv[:, sl], kg_ref[...]).astype(k_out.dtype)
    v_out[...] = kv[:, width:].astype(v_out.dtype)


def _mem_kv(mem2d, gain, w_kv, k_gain, B, l):
    width = MEM_HEADS * MEM_HD
    out = jax.ShapeDtypeStruct((B * MEM_LEN, width), BF16)
    return pl.pallas_call(
        _mem_kv_kernel,
        grid=(B,),
        in_specs=[pl.BlockSpec((MEM_LEN, D_MODEL), lambda b: (b, 0)),
                  pl.BlockSpec((1, D_MODEL), lambda b: (0, 0)),
                  _layer((D_MODEL, 2 * width), l),
                  pl.BlockSpec((1, MEM_HD), lambda b: (0, 0))],
        out_specs=[pl.BlockSpec((MEM_LEN, width), lambda b: (b, 0))] * 2,
        out_shape=[out, out],
        compiler_params=_cparams("arbitrary"),
        name="mem_kv",
    )(mem2d, gain, w_kv, k_gain)


def _mem_attn_kernel(q_ref, k_ref, v_ref, qg_ref, o_ref):
    for h in range(MEM_HEADS):
        sl = slice(h * MEM_HD, (h + 1) * MEM_HD)
        q = _rms(q_ref[:, sl].astype(F32), qg_ref[...]) * (MEM_HD ** -0.5)
        s = _dot_nt(q.astype(BF16), k_ref[:, sl])
        pr = jnp.exp(s - jnp.max(s, axis=1, keepdims=True))
        o = _dot(pr.astype(BF16), v_ref[:, sl]) / jnp.sum(pr, axis=1, keepdims=True)
        o_ref[:, sl] = o.astype(o_ref.dtype)


def _mem_attn(p, mk, mv, q_gain, B, S):
    tq = min(512, S)
    nq = S // tq
    width = MEM_HEADS * MEM_HD
    return pl.pallas_call(
        _mem_attn_kernel,
        grid=(B, nq),
        in_specs=[pl.BlockSpec((tq, width), lambda b, i: (b * nq + i, _COL["mq"] // width)),
                  pl.BlockSpec((MEM_LEN, width), lambda b, i: (b, 0)),
                  pl.BlockSpec((MEM_LEN, width), lambda b, i: (b, 0)),
                  pl.BlockSpec((1, MEM_HD), lambda b, i: (0, 0))],
        out_specs=pl.BlockSpec((tq, width), lambda b, i: (b * nq + i, 0)),
        out_shape=jax.ShapeDtypeStruct((B * S, width), BF16),
        compiler_params=_cparams("arbitrary", "arbitrary"),
        name="mem_attn",
    )(p, mk, mv, q_gain)


def _out_proj_kernel(og_ref, ys_ref, od_ref, om_ref, z_ref, x_ref, gn_ref, wglu_ref, w_ref, o_ref, lhs_ref):
    y = jax.nn.gelu(jnp.concatenate([ys_ref[q] for q in range(ys_ref.shape[0])], axis=1))
    o_s5 = y * jax.nn.sigmoid(_dot(y.astype(BF16), wglu_ref[...]))
    branches = (og_ref[...].astype(F32), o_s5, od_ref[...].astype(F32), om_ref[...].astype(F32))
    for bi, o in enumerate(branches):
        for h in range(o.shape[1] // HEAD):
            c0 = bi * 512 + h * HEAD
            z = z_ref[:, c0:c0 + HEAD].astype(F32)
            n = _rms(o[:, h * HEAD:(h + 1) * HEAD], gn_ref[:, c0:c0 + HEAD])
            lhs_ref[:, c0:c0 + HEAD] = (n * (z * jax.nn.sigmoid(z))).astype(BF16)
    o_ref[...] = x_ref[...] + _dot(lhs_ref[...], w_ref[...])


def _out_proj(o_gla, y_s5, o_dsa, o_mem, p, x2d, out_gain, w_glu, w_out, l):
    T = x2d.shape[0]
    tm = min(512, T)
    branch = pl.BlockSpec((tm, 512), lambda i: (i, 0))
    return pl.pallas_call(
        _out_proj_kernel,
        grid=(T // tm,),
        in_specs=[branch, pl.BlockSpec((S5_WIDTH // 128, tm, 128), lambda i: (0, i, 0)), branch, branch,
                  pl.BlockSpec((tm, MIX_WIDTH), lambda i: (i, 0)),
                  pl.BlockSpec((tm, D_MODEL), lambda i: (i, 0)),
                  _resident((1, MIX_WIDTH)),
                  _layer((S5_WIDTH, S5_WIDTH), l),
                  _layer((MIX_WIDTH, D_MODEL), l)],
        out_specs=pl.BlockSpec((tm, D_MODEL), lambda i: (i, 0)),
        out_shape=jax.ShapeDtypeStruct((T, D_MODEL), F32),
        scratch_shapes=[pltpu.VMEM((tm, MIX_WIDTH), BF16)],
        compiler_params=_cparams("arbitrary"),
        name="out_proj",
    )(o_gla, y_s5, o_dsa, o_mem, p, x2d, out_gain, w_glu, w_out)


def _permute_kernel(w_ref, o_ref):
    c = 0
    for n in _ORDER:
        lo, hi = _ORIG[n]
        o_ref[c:c + hi - lo, :] = w_ref[lo:hi, :].astype(o_ref.dtype)
        c += hi - lo
    o_ref[c:, :] = jnp.zeros((IN_PAD - c, o_ref.shape[1]), o_ref.dtype)


def _permute_w_in(w_in):
    depth, rows, width = w_in.shape
    tl = 256
    return pl.pallas_call(
        _permute_kernel,
        grid=(depth, rows // tl),
        in_specs=[pl.BlockSpec((None, width, tl), lambda l, i: (l, 0, i))],
        out_specs=pl.BlockSpec((None, IN_PAD, tl), lambda l, i: (l, 0, i)),
        out_shape=jax.ShapeDtypeStruct((depth, IN_PAD, rows), BF16),
        compiler_params=_cparams("arbitrary", "arbitrary"),
        name="permute_w_in",
    )(jnp.swapaxes(w_in, 1, 2))


def kernel(x, mem, positions, ln_gain, w_in, gla_w_up, gla_b_up, s5_a_re, s5_a_im, s5_log_step, s5_b_re, s5_b_im, s5_c_re, s5_c_im, s5_d, s5_w_glu, dsa_q_norm, dsa_k_norm, mem_norm, w_mem_kv, mem_q_norm, mem_k_norm, out_norm, w_out):
    B, S, _ = x.shape
    depth = w_in.shape[0]
    T = B * S
    x2d = x.reshape(T, D_MODEL)
    mem2d = mem.reshape(B * mem.shape[1], D_MODEL)
    w_in_p = _permute_w_in(w_in)
    wup_pad = jnp.zeros((depth, 128, GLA_HEADS * GLA_DK), F32)
    wup_pad = wup_pad.at[:, GLR_LANE:GLR_LANE + GLA_GATE_RANK].set(gla_w_up).astype(BF16)
    w_kv = w_mem_kv.astype(BF16)
    w_glu = s5_w_glu.astype(BF16)
    w_o = w_out.astype(BF16)
    tables = _rope_tables(positions)
    for l in range(depth):
        p, su = _in_proj(x2d, ln_gain[l][None], w_in_p, l)
        o_gla = _gla(p, wup_pad[l], gla_b_up[l][None], B, S)
        mats = _s5_mats(s5_a_re[l], s5_a_im[l], s5_log_step[l], s5_b_re[l], s5_b_im[l],
                        s5_c_re[l], s5_c_im[l], s5_d[l])
        y_s5 = _s5(su, mats, B, S)
        qt, k_rot, vt, iqt, ik_rot, wt = _dsa_prep(p, tables, dsa_q_norm[l][:, None], dsa_k_norm[l][None])
        o_dsa = _dsa(qt, k_rot, vt, iqt, ik_rot, wt, B, S)
        mk, mv = _mem_kv(mem2d, mem_norm[l][None], w_kv, mem_k_norm[l][None], B, l)
        o_mem = _mem_attn(p, mk, mv, mem_q_norm[l][None], B, S)
        x2d = _out_proj(o_gla, y_s5, o_dsa, o_mem, p, x2d, out_norm[l][None], w_glu, w_o, l)
    return x2d.reshape(B, S, D_MODEL)
```
